```python
import math
import jax
import jax.numpy as jnp
from jax import lax
import numpy as np

D_MODEL = 4096
BATCH = 2
SEQ = 4096
DEPTH = 2

GRID_W = 64
CTX_LEN = 256
HEAD_DIM = 128
ATTN_WIDTH = D_MODEL // 2
N_HEADS = ATTN_WIDTH // HEAD_DIM
N_KV_HEADS = N_HEADS // 4
GROUP = N_HEADS // N_KV_HEADS
KV_WIDTH = N_KV_HEADS * HEAD_DIM
HYENA_WIDTH = D_MODEL - ATTN_WIDTH
HYENA_ORDER = 2
FILTER_BANDS = 16
FILTER_EMB = 1 + 2 * FILTER_BANDS
FILTER_HIDDEN = 64
DECAY_TARGET = 1e-2
FAST_DECAY_PCT = 0.3
SLOW_DECAY_PCT = 1.5
ROPE_THETA = 10000.0
Q_BLOCK = 128
N_EXPERTS = 32
TOP_K = 4
D_EXPERT = D_MODEL // 8
SWIGLU_LIMIT = 7.0
SWIGLU_ALPHA = 1.702
EXPERT_BLOCK = 128
EPS = 1e-6
IN_WIDTH = ATTN_WIDTH + 2 * KV_WIDTH + (HYENA_ORDER + 1) * HYENA_WIDTH
N_FILTER_OUT = HYENA_ORDER * 2 * HYENA_WIDTH

kernel_name = 'hybrid_gqa_hyena_moe_dit'

F32 = jnp.float32


def rmsnorm(x, g):
    xf = x.astype(F32)
    y = xf * lax.rsqrt(jnp.mean(xf * xf, axis=-1, keepdims=True) + EPS)
    return (y * g.astype(F32)).astype(x.dtype)


def adaln(cvec, w_ada, b_ada):
    m = jax.nn.silu(cvec) @ w_ada + b_ada
    return jnp.split(m[..., None, :], 6, axis=-1)


def modulate(h, shift, scale):
    return h * (1 + scale) + shift


def axial_rope_tables(n_tokens):
    rows = n_tokens // GRID_W
    r, col = jnp.meshgrid(jnp.arange(rows), jnp.arange(GRID_W), indexing='ij')
    n_freq = HEAD_DIM // 4
    inv = ROPE_THETA ** (-jnp.arange(n_freq, dtype=F32) / n_freq)
    ang_r = r.reshape(-1, 1).astype(F32) * inv
    ang_c = col.reshape(-1, 1).astype(F32) * inv
    ang = jnp.stack([ang_r, ang_c], axis=1)
    return jnp.cos(ang), jnp.sin(ang)


def apply_axial_rope(x, cos, sin):
    B, T, H, _ = x.shape
    xr = x.astype(F32).reshape(B, T, H, 2, 2, HEAD_DIM // 4)
    cs = cos[None, :, None]
    sn = sin[None, :, None]
    x1, x2 = xr[..., 0, :], xr[..., 1, :]
    out = jnp.stack([x1 * cs - x2 * sn, x2 * cs + x1 * sn], axis=-2)
    return out.reshape(B, T, H, HEAD_DIM).astype(x.dtype)


def gqa_block(q5, k, v):
    s = jnp.einsum('bqkgd,bskd->bkgqs', q5, k) * (HEAD_DIM ** -0.5)
    p = jax.nn.softmax(s, axis=-1)
    return jnp.einsum('bkgqs,bskd->bqkgd', p, v)


def latent_attention(q, k, v):
    B, T = q.shape[:2]
    nb = T // Q_BLOCK
    kf, vf = k.astype(F32), v.astype(F32)
    qb = q.astype(F32).reshape(B, nb, Q_BLOCK, N_KV_HEADS, GROUP, HEAD_DIM).transpose(1, 0, 2, 3, 4, 5)
    ob = lax.map(lambda qblk: gqa_block(qblk, kf, vf), qb)
    return ob.transpose(1, 0, 2, 3, 4, 5).reshape(B, T, ATTN_WIDTH).astype(q.dtype)


def context_attention(q, k, v):
    B, C = q.shape[:2]
    q5 = q.astype(F32).reshape(B, C, N_KV_HEADS, GROUP, HEAD_DIM)
    o = gqa_block(q5, k.astype(F32), v.astype(F32))
    return o.reshape(B, C, ATTN_WIDTH).astype(q.dtype)


def short_conv(u, w, b):
    up = jnp.pad(u, ((0, 0), (1, 1), (0, 0)))
    return up[:, :-2] * w[0] + up[:, 1:-1] * w[1] + up[:, 2:] * w[2] + b


def hyena_filters(L, w1, b1, w2, b2, w3, freq):
    w1, b1, w2, b2, w3, freq = (a.astype(F32) for a in (w1, b1, w2, b2, w3, freq))
    pos = jnp.arange(L, dtype=F32)
    t = pos / max(L - 1, 1)
    bands = jnp.linspace(1e-4, FILTER_BANDS - 1, FILTER_BANDS, dtype=F32)
    ang = (2 * math.pi / L) * pos[:, None] * bands
    feats = jnp.concatenate([t[:, None], jnp.cos(ang), jnp.sin(ang)], axis=-1)
    hid = jnp.sin(freq * (feats @ w1 + b1))
    hid = jnp.sin(freq * (hid @ w2 + b2))
    h = (hid @ w3).reshape(L, HYENA_ORDER, 2, HYENA_WIDTH)
    max_decay = math.log(DECAY_TARGET) / FAST_DECAY_PCT
    min_decay = math.log(DECAY_TARGET) / SLOW_DECAY_PCT
    deltas = jnp.linspace(min_decay, max_decay, HYENA_WIDTH, dtype=F32)
    decay = jnp.exp(-t[:, None] * jnp.abs(deltas))
    h = h * decay[:, None, None, :]
    fwd, bwd = h[:, :, 0], h[:, :, 1]
    k = jnp.concatenate([fwd, jnp.zeros_like(fwd[:1]), bwd[1:][::-1]], axis=0)
    return jnp.fft.rfft(k, axis=0)


def hyena_mixer(u, conv_w, conv_b, w1, b1, w2, b2, w3, freq, bias):
    L = u.shape[1]
    uc = short_conv(u, conv_w, conv_b).astype(F32)
    v, x1, x2 = jnp.split(uc, 3, axis=-1)
    kf = hyena_filters(L, w1, b1, w2, b2, w3, freq)
    bias = bias.astype(F32)
    z = v
    for o, gate in enumerate((x1, x2)):
        zf = jnp.fft.rfft(z, n=2 * L, axis=1)
        conv = jnp.fft.irfft(zf * kf[None, :, o], n=2 * L, axis=1)[:, :L]
        z = gate * (conv + bias[o] * z)
    return z.astype(u.dtype)


def mix_out(y_attn, y_hy, g, w):
    y = jnp.concatenate([rmsnorm(y_attn, g[:ATTN_WIDTH]), rmsnorm(y_hy, g[ATTN_WIDTH:])], axis=-1)
    return y @ w


def moe_ffn(h, w_router, b_router, w1, b1, w2, b2):
    N, D = h.shape
    logits = h.astype(F32) @ w_router.astype(F32) + b_router.astype(F32)
    top_val, top_idx = lax.top_k(logits, TOP_K)
    gates = jax.nn.softmax(top_val, axis=-1)
    n_assign = N * TOP_K
    flat_e = top_idx.reshape(-1).astype(jnp.int32)
    flat_tok = jnp.arange(n_assign, dtype=jnp.int32) // TOP_K
    order = jnp.argsort(flat_e)
    sorted_e = flat_e[order]
    counts = jnp.zeros((N_EXPERTS,), jnp.int32).at[flat_e].add(1)
    padded = (counts + EXPERT_BLOCK - 1) // EXPERT_BLOCK * EXPERT_BLOCK
    start = jnp.cumsum(counts) - counts
    pad_end = jnp.cumsum(padded)
    pad_start = pad_end - padded
    dest = pad_start[sorted_e] + (jnp.arange(n_assign, dtype=jnp.int32) - start[sorted_e])
    n_blocks = -(-n_assign // EXPERT_BLOCK) + N_EXPERTS
    n_rows = n_blocks * EXPERT_BLOCK
    row_tok = jnp.full((n_rows,), N, jnp.int32).at[dest].set(flat_tok[order])
    row_gate = jnp.zeros((n_rows,), F32).at[dest].set(gates.reshape(-1)[order])
    block_e = jnp.minimum(
        jnp.searchsorted(pad_end, jnp.arange(n_blocks, dtype=jnp.int32) * EXPERT_BLOCK, side='right'),
        N_EXPERTS - 1)
    h_pad = jnp.concatenate([h, jnp.zeros((1, D), h.dtype)], axis=0)

    def expert_block(args):
        toks, e = args
        xb = h_pad[toks]
        gu = (xb @ w1[e] + b1[e]).astype(F32)
        g, up = jnp.split(gu, 2, axis=-1)
        g = jnp.minimum(g, SWIGLU_LIMIT)
        up = jnp.clip(up, -SWIGLU_LIMIT, SWIGLU_LIMIT)
        act = (up + 1) * g * jax.nn.sigmoid(SWIGLU_ALPHA * g)
        return act.astype(h.dtype) @ w2[e] + b2[e]

    ys = lax.map(expert_block, (row_tok.reshape(n_blocks, EXPERT_BLOCK), block_e))
    ys = ys.reshape(n_rows, D).astype(F32) * row_gate[:, None]
    out = jnp.zeros((N + 1, D), F32).at[row_tok].add(ys)[:N]
    return out.astype(h.dtype)


def setup_inputs(seed: int = 0) -> dict:
    key = jax.random.key(seed)
    keys = iter(jax.random.split(key, 40))

    def nrm(shape, scale):
        return jax.random.normal(next(keys), shape, F32) * scale

    L = DEPTH
    return {
        'x': nrm((BATCH, SEQ, D_MODEL), 1.0),
        'c': nrm((BATCH, D_MODEL), 1.0),
        'ctx': nrm((BATCH, CTX_LEN, D_MODEL), 1.0),
        'c_ctx': nrm((D_MODEL,), 1.0),
        'w_ada': nrm((L, D_MODEL, 6 * D_MODEL), D_MODEL ** -0.5),
        'b_ada': nrm((L, 6 * D_MODEL), 0.02),
        'g_mix': 1.0 + nrm((L, D_MODEL), 0.02),
        'g_ffn': 1.0 + nrm((L, D_MODEL), 0.02),
        'w_in': nrm((L, D_MODEL, IN_WIDTH), D_MODEL ** -0.5),
        'q_norm': 1.0 + nrm((L, HEAD_DIM), 0.02),
        'k_norm': 1.0 + nrm((L, HEAD_DIM), 0.02),
        'hy_conv_w': nrm((L, 3, (HYENA_ORDER + 1) * HYENA_WIDTH), 3 ** -0.5),
        'hy_conv_b': nrm((L, (HYENA_ORDER + 1) * HYENA_WIDTH), 0.02),
        'hy_w1': nrm((L, FILTER_EMB, FILTER_HIDDEN), FILTER_EMB ** -0.5),
        'hy_b1': nrm((L, FILTER_HIDDEN), 0.02),
        'hy_w2': nrm((L, FILTER_HIDDEN, FILTER_HIDDEN), FILTER_HIDDEN ** -0.5),
        'hy_b2': nrm((L, FILTER_HIDDEN), 0.02),
        'hy_w3': nrm((L, FILTER_HIDDEN, N_FILTER_OUT), FILTER_HIDDEN ** -0.5),
        'hy_freq': 1.0 + nrm((L, FILTER_HIDDEN), 0.02),
        'hy_bias': nrm((L, HYENA_ORDER, HYENA_WIDTH), 0.5),
        'g_out': 1.0 + nrm((L, D_MODEL), 0.02),
        'w_out': nrm((L, D_MODEL, D_MODEL), D_MODEL ** -0.5),
        'w_router': nrm((L, D_MODEL, N_EXPERTS), D_MODEL ** -0.5),
        'b_router': nrm((L, N_EXPERTS), 0.01),
        'moe_w1': nrm((L, N_EXPERTS, D_MODEL, 2 * D_EXPERT), D_MODEL ** -0.5),
        'moe_b1': nrm((L, N_EXPERTS, 2 * D_EXPERT), 0.02),
        'moe_w2': nrm((L, N_EXPERTS, D_EXPERT, D_MODEL), D_EXPERT ** -0.5),
        'moe_b2': nrm((L, N_EXPERTS, D_MODEL), 0.02),
        'g_final': 1.0 + nrm((D_MODEL,), 0.02),
    }


def reference(x, c, ctx, c_ctx, w_ada, b_ada, g_mix, g_ffn, w_in, q_norm, k_norm,
              hy_conv_w, hy_conv_b, hy_w1, hy_b1, hy_w2, hy_b2, hy_w3, hy_freq, hy_bias,
              g_out, w_out, w_router, b_router, moe_w1, moe_b1, moe_w2, moe_b2, g_final):
    B, T, D = x.shape
    C = ctx.shape[1]
    cos, sin = axial_rope_tables(T)
    splits = (ATTN_WIDTH, ATTN_WIDTH + KV_WIDTH, ATTN_WIDTH + 2 * KV_WIDTH)
    xc = ctx
    for i in range(DEPTH):
        last = i == DEPTH - 1
        sh_a, sc_a, gt_a, sh_f, sc_f, gt_f = adaln(c, w_ada[i], b_ada[i])
        cmod = adaln(c_ctx, w_ada[i], b_ada[i])
        hp = (hy_conv_w[i], hy_conv_b[i], hy_w1[i], hy_b1[i], hy_w2[i], hy_b2[i],
              hy_w3[i], hy_freq[i], hy_bias[i])

        h = modulate(rmsnorm(x, g_mix[i]), sh_a, sc_a)
        hc = modulate(rmsnorm(xc, g_mix[i]), cmod[0], cmod[1])
        q, k, v, u = jnp.split(h @ w_in[i], splits, axis=-1)
        q = apply_axial_rope(rmsnorm(q.reshape(B, T, N_HEADS, HEAD_DIM), q_norm[i]), cos, sin)
        k = apply_axial_rope(rmsnorm(k.reshape(B, T, N_KV_HEADS, HEAD_DIM), k_norm[i]), cos, sin)
        v = v.reshape(B, T, N_KV_HEADS, HEAD_DIM)
        if last:
            kc, vc = jnp.split(hc @ w_in[i][:, splits[0]:splits[2]], 2, axis=-1)
        else:
            qc, kc, vc, uc = jnp.split(hc @ w_in[i], splits, axis=-1)
        kc = rmsnorm(kc.reshape(B, C, N_KV_HEADS, HEAD_DIM), k_norm[i])
        vc = vc.reshape(B, C, N_KV_HEADS, HEAD_DIM)
        y_attn = latent_attention(q, jnp.concatenate([kc, k], axis=1), jnp.concatenate([vc, v], axis=1))
        y_hy = hyena_mixer(u, *hp)
        x = x + gt_a * mix_out(y_attn, y_hy, g_out[i], w_out[i])
        if not last:
            qc = rmsnorm(qc.reshape(B, C, N_HEADS, HEAD_DIM), q_norm[i])
            yc_attn = context_attention(qc, kc, vc)
            yc_hy = hyena_mixer(uc, *hp)
            xc = xc + cmod[2] * mix_out(yc_attn, yc_hy, g_out[i], w_out[i])

        moe = (w_router[i], b_router[i], moe_w1[i], moe_b1[i], moe_w2[i], moe_b2[i])
        hf = modulate(rmsnorm(x, g_ffn[i]), sh_f, sc_f).reshape(B * T, D)
        if last:
            x = x + gt_f * moe_ffn(hf, *moe).reshape(B, T, D)
        else:
            hfc = modulate(rmsnorm(xc, g_ffn[i]), cmod[3], cmod[4]).reshape(B * C, D)
            y = moe_ffn(jnp.concatenate([hf, hfc], axis=0), *moe)
            x = x + gt_f * y[:B * T].reshape(B, T, D)
            xc = xc + cmod[5] * y[B * T:].reshape(B, C, D)
    return rmsnorm(x, g_final)
```

```python
import functools
import math

import numpy as np
import jax
import jax.numpy as jnp
from jax import lax
from jax.experimental import pallas as pl
from jax.experimental.pallas import tpu as pltpu

F32 = jnp.float32
BF16 = jnp.bfloat16

GRID_W = 64
HEAD_DIM = 128
N_HEADS = 16
N_KV_HEADS = 4
GROUP = N_HEADS // N_KV_HEADS
HYENA_ORDER = 2
FILTER_BANDS = 16
FILTER_EMB = 1 + 2 * FILTER_BANDS
FILTER_HIDDEN = 64
DECAY_TARGET = 1e-2
FAST_DECAY_PCT = 0.3
SLOW_DECAY_PCT = 1.5
ROPE_THETA = 10000.0
N_EXPERTS = 32
TOP_K = 4
SWIGLU_LIMIT = 7.0
SWIGLU_ALPHA = 1.702
EPS = 1e-6

LANES = 128
V7X_VMEM_BYTES = 64 * 1024 * 1024
VMEM_LIMIT = V7X_VMEM_BYTES - 8 * 1024 * 1024
FFT_TA = 32
MOE_ROWS = 256


def _params(sem):
    return pltpu.CompilerParams(dimension_semantics=sem, vmem_limit_bytes=VMEM_LIMIT)


def _adaln_kernel(c_ref, w_ref, b_ref, o_ref):
    c = c_ref[...]
    s = (c * jax.nn.sigmoid(c)).astype(BF16)
    o_ref[0] = jnp.dot(s, w_ref[0].astype(BF16), preferred_element_type=F32) + b_ref[0]


def adaln_all(cvecs, w_ada, b_ada):
    L, D, N = w_ada.shape
    tn = 512
    return pl.pallas_call(
        _adaln_kernel,
        grid=(L, N // tn),
        in_specs=[
            pl.BlockSpec((8, D), lambda l, j: (0, 0)),
            pl.BlockSpec((1, D, tn), lambda l, j: (l, 0, j)),
            pl.BlockSpec((1, 1, tn), lambda l, j: (l, 0, j)),
        ],
        out_specs=pl.BlockSpec((1, 8, tn), lambda l, j: (l, 0, j)),
        out_shape=jax.ShapeDtypeStruct((L, 8, N), F32),
        compiler_params=_params(("arbitrary", "arbitrary")),
        name="adaln",
    )(cvecs, w_ada, b_ada.reshape(L, 1, N))


def _norm_mod_kernel(x_ref, g_ref, sh_ref, sc_ref, h_ref):
    x = x_ref[0]
    y = x * lax.rsqrt(jnp.mean(x * x, axis=-1, keepdims=True) + EPS) * g_ref[...]
    h_ref[0] = (y * (1.0 + sc_ref[0]) + sh_ref[0]).astype(h_ref.dtype)


def _norm_mod_router_kernel(x_ref, g_ref, sh_ref, sc_ref, wr_ref, br_ref, h_ref, l_ref):
    x = x_ref[0]
    y = x * lax.rsqrt(jnp.mean(x * x, axis=-1, keepdims=True) + EPS) * g_ref[...]
    h = y * (1.0 + sc_ref[0]) + sh_ref[0]
    h_ref[0] = h
    l_ref[0] = jnp.dot(h, wr_ref[...], preferred_element_type=F32,
                       precision=lax.Precision.HIGHEST) + br_ref[...]


def _mod_map(bm):
    return (lambda b, i: (b, 0, 0)) if bm > 1 else (lambda b, i: (0, 0, 0))


def norm_mod(x, g, shift, scale, out_dtype=BF16):
    B, T, D = x.shape
    tm = min(T, 256)
    return pl.pallas_call(
        _norm_mod_kernel,
        grid=(B, T // tm),
        in_specs=[
            pl.BlockSpec((1, tm, D), lambda b, i: (b, i, 0)),
            pl.BlockSpec((1, D), lambda b, i: (0, 0)),
            pl.BlockSpec((1, 1, D), _mod_map(shift.shape[0])),
            pl.BlockSpec((1, 1, D), _mod_map(scale.shape[0])),
        ],
        out_specs=pl.BlockSpec((1, tm, D), lambda b, i: (b, i, 0)),
        out_shape=jax.ShapeDtypeStruct((B, T, D), out_dtype),
        compiler_params=_params(("arbitrary", "arbitrary")),
        name="norm_mod",
    )(x, g.reshape(1, D), shift, scale)


def norm_mod_router(x, g, shift, scale, w_router, b_router):
    B, T, D = x.shape
    E = w_router.shape[1]
    tm = min(T, 256)
    return pl.pallas_call(
        _norm_mod_router_kernel,
        grid=(B, T // tm),
        in_specs=[
            pl.BlockSpec((1, tm, D), lambda b, i: (b, i, 0)),
            pl.BlockSpec((1, D), lambda b, i: (0, 0)),
            pl.BlockSpec((1, 1, D), _mod_map(shift.shape[0])),
            pl.BlockSpec((1, 1, D), _mod_map(scale.shape[0])),
            pl.BlockSpec((D, E), lambda b, i: (0, 0)),
            pl.BlockSpec((1, E), lambda b, i: (0, 0)),
        ],
        out_specs=[
            pl.BlockSpec((1, tm, D), lambda b, i: (b, i, 0)),
            pl.BlockSpec((1, tm, E), lambda b, i: (b, i, 0)),
        ],
        out_shape=[jax.ShapeDtypeStruct((B, T, D), F32), jax.ShapeDtypeStruct((B, T, E), F32)],
        compiler_params=_params(("arbitrary", "arbitrary")),
        name="norm_mod_router",
    )(x, g.reshape(1, D), shift, scale, w_router, b_router.reshape(1, E))


def _mixnorm_kernel(a_ref, hy_ref, g_ref, o_ref):
    wa = a_ref.shape[-1]
    a = a_ref[0]
    b = hy_ref[0]
    g = g_ref[...]
    o_ref[0, :, :wa] = (a * lax.rsqrt(jnp.mean(a * a, axis=-1, keepdims=True) + EPS) * g[:, :wa]).astype(o_ref.dtype)
    o_ref[0, :, wa:] = (b * lax.rsqrt(jnp.mean(b * b, axis=-1, keepdims=True) + EPS) * g[:, wa:]).astype(o_ref.dtype)


def mixnorm(y_attn, y_hy, g):
    B, T, wa = y_attn.shape
    wh = y_hy.shape[-1]
    tm = min(T, 256)
    return pl.pallas_call(
        _mixnorm_kernel,
        grid=(B, T // tm),
        in_specs=[
            pl.BlockSpec((1, tm, wa), lambda b, i: (b, i, 0)),
            pl.BlockSpec((1, tm, wh), lambda b, i: (b, i, 0)),
            pl.BlockSpec((1, wa + wh), lambda b, i: (0, 0)),
        ],
        out_specs=pl.BlockSpec((1, tm, wa + wh), lambda b, i: (b, i, 0)),
        out_shape=jax.ShapeDtypeStruct((B, T, wa + wh), BF16),
        compiler_params=_params(("arbitrary", "arbitrary")),
        name="mixnorm",
    )(y_attn, y_hy, g.reshape(1, wa + wh))


def _final_norm_kernel(x_ref, g_ref, o_ref):
    x = x_ref[0]
    o_ref[0] = x * lax.rsqrt(jnp.mean(x * x, axis=-1, keepdims=True) + EPS) * g_ref[...]


def final_norm(x, g):
    B, T, D = x.shape
    tm = min(T, 256)
    return pl.pallas_call(
        _final_norm_kernel,
        grid=(B, T // tm),
        in_specs=[pl.BlockSpec((1, tm, D), lambda b, i: (b, i, 0)), pl.BlockSpec((1, D), lambda b, i: (0, 0))],
        out_specs=pl.BlockSpec((1, tm, D), lambda b, i: (b, i, 0)),
        out_shape=jax.ShapeDtypeStruct((B, T, D), F32),
        compiler_params=_params(("arbitrary", "arbitrary")),
        name="final_norm",
    )(x, g.reshape(1, D))


def _mm_kernel(x_ref, w_ref, o_ref):
    o_ref[0] = jnp.dot(x_ref[0], w_ref[...], preferred_element_type=F32).astype(o_ref.dtype)


def _mm_res_kernel(x_ref, w_ref, r_ref, g_ref, o_ref):
    acc = jnp.dot(x_ref[0], w_ref[...], preferred_element_type=F32)
    o_ref[0] = r_ref[0] + g_ref[0] * acc


def _mm_tiles(T, N):
    tm = min(T, 1024)
    tn = 512 if N % 512 == 0 else N
    return tm, tn


def matmul(x, w, out_dtype=F32):
    B, T, K = x.shape
    N = w.shape[1]
    tm, tn = _mm_tiles(T, N)
    return pl.pallas_call(
        _mm_kernel,
        grid=(N // tn, B, T // tm),
        in_specs=[
            pl.BlockSpec((1, tm, K), lambda j, b, i: (b, i, 0)),
            pl.BlockSpec((K, tn), lambda j, b, i: (0, j)),
        ],
        out_specs=pl.BlockSpec((1, tm, tn), lambda j, b, i: (b, i, j)),
        out_shape=jax.ShapeDtypeStruct((B, T, N), out_dtype),
        compiler_params=_params(("arbitrary", "arbitrary", "arbitrary")),
        name="matmul",
    )(x, w)


def matmul_residual(x, w, res, gate):
    B, T, K = x.shape
    N = w.shape[1]
    tm, tn = _mm_tiles(T, N)
    gmap = (lambda j, b, i: (b, 0, j)) if gate.shape[0] > 1 else (lambda j, b, i: (0, 0, j))
    return pl.pallas_call(
        _mm_res_kernel,
        grid=(N // tn, B, T // tm),
        in_specs=[
            pl.BlockSpec((1, tm, K), lambda j, b, i: (b, i, 0)),
            pl.BlockSpec((K, tn), lambda j, b, i: (0, j)),
            pl.BlockSpec((1, tm, tn), lambda j, b, i: (b, i, j)),
            pl.BlockSpec((1, 1, tn), gmap),
        ],
        out_specs=pl.BlockSpec((1, tm, tn), lambda j, b, i: (b, i, j)),
        out_shape=jax.ShapeDtypeStruct((B, T, N), F32),
        compiler_params=_params(("arbitrary", "arbitrary", "arbitrary")),
        name="matmul_residual",
    )(x, w, res, gate)


def _head_norm_kernel(x_ref, g_ref, cos_ref, sin_ref, o_ref, *, n_heads, rope, scale):
    g = g_ref[...]
    if rope:
        cs = cos_ref[...]
        sn = sin_ref[...]
        lane = lax.broadcasted_iota(jnp.int32, cs.shape, 1)
        first = (lane % 64) < 32
    for h in range(n_heads):
        xh = x_ref[0, :, h * HEAD_DIM:(h + 1) * HEAD_DIM]
        y = xh * lax.rsqrt(jnp.mean(xh * xh, axis=-1, keepdims=True) + EPS) * g
        if rope:
            swapped = jnp.where(first, pltpu.roll(y, 96, 1), pltpu.roll(y, 32, 1))
            y = y * cs + swapped * sn
        o_ref[0, :, h * HEAD_DIM:(h + 1) * HEAD_DIM] = (y * scale).astype(o_ref.dtype)


def head_norm(x, col0, n_heads, g, cos_t, sin_t, rope, scale):
    B, T, _ = x.shape
    w = n_heads * HEAD_DIM
    tm = min(T, 256)
    cb = col0 // w
    assert cb * w == col0
    kern = functools.partial(_head_norm_kernel, n_heads=n_heads, rope=rope, scale=scale)
    return pl.pallas_call(
        kern,
        grid=(B, T // tm),
        in_specs=[
            pl.BlockSpec((1, tm, w), lambda b, i: (b, i, cb)),
            pl.BlockSpec((1, HEAD_DIM), lambda b, i: (0, 0)),
            pl.BlockSpec((tm, HEAD_DIM), lambda b, i: (i, 0)),
            pl.BlockSpec((tm, HEAD_DIM), lambda b, i: (i, 0)),
        ],
        out_specs=pl.BlockSpec((1, tm, w), lambda b, i: (b, i, 0)),
        out_shape=jax.ShapeDtypeStruct((B, T, w), BF16),
        compiler_params=_params(("arbitrary", "arbitrary")),
        name="head_norm",
    )(x, g.reshape(1, HEAD_DIM), cos_t, sin_t)


def rope_tables(T):
    pos = np.arange(T)
    r, col = pos // GRID_W, pos % GRID_W
    n_freq = HEAD_DIM // 4
    inv = ROPE_THETA ** (-np.arange(n_freq, dtype=np.float64) / n_freq)
    ar, ac = r[:, None] * inv, col[:, None] * inv
    cos_t = np.concatenate([np.cos(ar), np.cos(ar), np.cos(ac), np.cos(ac)], axis=1)
    sin_t = np.concatenate([-np.sin(ar), np.sin(ar), -np.sin(ac), np.sin(ac)], axis=1)
    return jnp.asarray(cos_t, F32), jnp.asarray(sin_t, F32)


def _attn_kernel(q_ref, k_ref, v_ref, o_ref):
    tq = q_ref.shape[1]
    q = q_ref[0]
    qs = jnp.concatenate([q[:, j * HEAD_DIM:(j + 1) * HEAD_DIM] for j in range(GROUP)], axis=0)
    s = lax.dot_general(qs, k_ref[0], (((1,), (1,)), ((), ())), preferred_element_type=F32)
    m = jnp.max(s, axis=-1, keepdims=True)
    p = jnp.exp(s - m)
    l = jnp.sum(p, axis=-1, keepdims=True)
    o = jnp.dot(p.astype(BF16), v_ref[0], preferred_element_type=F32) / l
    for j in range(GROUP):
        o_ref[0, :, j * HEAD_DIM:(j + 1) * HEAD_DIM] = o[j * tq:(j + 1) * tq, :]


def attention(q, k, v):
    B, T, _ = q.shape
    S = k.shape[1]
    tq = min(T, 128)
    gw = GROUP * HEAD_DIM
    return pl.pallas_call(
        _attn_kernel,
        grid=(B, N_KV_HEADS, T // tq),
        in_specs=[
            pl.BlockSpec((1, tq, gw), lambda b, g, i: (b, i, g)),
            pl.BlockSpec((1, S, HEAD_DIM), lambda b, g, i: (b, 0, g)),
            pl.BlockSpec((1, S, HEAD_DIM), lambda b, g, i: (b, 0, g)),
        ],
        out_specs=pl.BlockSpec((1, tq, gw), lambda b, g, i: (b, i, g)),
        out_shape=jax.ShapeDtypeStruct((B, T, N_HEADS * HEAD_DIM), F32),
        compiler_params=_params(("arbitrary", "arbitrary", "arbitrary")),
        name="attention",
    )(q, k, v)


def _short_conv_kernel(u_ref, w_ref, b_ref, o_ref):
    u = u_ref[0]
    L = u.shape[0]
    row = lax.broadcasted_iota(jnp.int32, u.shape, 0)
    prev = jnp.where(row == 0, 0.0, pltpu.roll(u, 1, 0))
    nxt = jnp.where(row == L - 1, 0.0, pltpu.roll(u, L - 1, 0))
    w = w_ref[...]
    o_ref[0] = prev * w[0:1, :] + u * w[1:2, :] + nxt * w[2:3, :] + b_ref[...]


def short_conv(x, col0, width, w, b):
    B, L, _ = x.shape
    cb = 256 if L > 1024 else 512
    assert col0 % cb == 0 and width % cb == 0
    off = col0 // cb
    return pl.pallas_call(
        _short_conv_kernel,
        grid=(B, width // cb),
        in_specs=[
            pl.BlockSpec((1, L, cb), lambda bi, j: (bi, 0, off + j)),
            pl.BlockSpec((3, cb), lambda bi, j: (0, j)),
            pl.BlockSpec((1, cb), lambda bi, j: (0, j)),
        ],
        out_specs=pl.BlockSpec((1, L, cb), lambda bi, j: (bi, 0, j)),
        out_shape=jax.ShapeDtypeStruct((B, L, width), F32),
        compiler_params=_params(("arbitrary", "arbitrary")),
        name="short_conv",
    )(x, w, b.reshape(1, width))


def _filter_hidden_kernel(f_ref, w1_ref, b1_ref, w2_ref, b2_ref, fr_ref, o_ref):
    hp = lax.Precision.HIGHEST
    fr = fr_ref[...]
    h1 = jnp.sin(fr * (jnp.dot(w1_ref[...], f_ref[...], preferred_element_type=F32, precision=hp) + b1_ref[...]))
    o_ref[...] = jnp.sin(fr * (jnp.dot(w2_ref[...], h1, preferred_element_type=F32, precision=hp) + b2_ref[...]))


def filter_hidden(feats_t, w1, b1, w2, b2, freq):
    H = FILTER_HIDDEN
    P = feats_t.shape[1]
    fe = feats_t.shape[0]
    w1t = jnp.zeros((H, fe), F32).at[:, :FILTER_EMB].set(w1.T)
    return pl.pallas_call(
        _filter_hidden_kernel,
        out_shape=jax.ShapeDtypeStruct((H, P), F32),
        compiler_params=pltpu.CompilerParams(vmem_limit_bytes=VMEM_LIMIT),
        name="filter_hidden",
    )(feats_t, w1t, b1.reshape(H, 1), w2.T, b2.reshape(H, 1), freq.reshape(H, 1))


def _filter_k_kernel(w3_ref, hid_ref, tpos_ref, mask_ref, e0_ref, dl_ref, bias_ref, o_ref):
    h = jnp.dot(w3_ref[0], hid_ref[0], preferred_element_type=F32, precision=lax.Precision.HIGHEST)
    decay = jnp.exp(-tpos_ref[0] * dl_ref[...])
    o_ref[0] = mask_ref[0] * decay * h + bias_ref[0] * e0_ref[...]


def filter_k(w3t, hid, tpos, mask, e0, deltas, bias_aug):
    OD, C, H = w3t.shape
    L = hid.shape[-1]
    rc = 256
    return pl.pallas_call(
        _filter_k_kernel,
        grid=(OD, C // rc),
        in_specs=[
            pl.BlockSpec((1, rc, H), lambda od, j: (od, j, 0)),
            pl.BlockSpec((1, H, L), lambda od, j: (od % 2, 0, 0)),
            pl.BlockSpec((1, 1, L), lambda od, j: (od % 2, 0, 0)),
            pl.BlockSpec((1, 1, L), lambda od, j: (od % 2, 0, 0)),
            pl.BlockSpec((1, L), lambda od, j: (0, 0)),
            pl.BlockSpec((rc, 1), lambda od, j: (j, 0)),
            pl.BlockSpec((1, rc, 1), lambda od, j: (od, j, 0)),
        ],
        out_specs=pl.BlockSpec((1, rc, L), lambda od, j: (od, j, 0)),
        out_shape=jax.ShapeDtypeStruct((OD, C, L), F32),
        compiler_params=_params(("arbitrary", "arbitrary")),
        name="filter_k",
    )(w3t, hid, tpos, mask, e0, deltas, bias_aug)


def hyena_filter_taps(L, C, w1, b1, w2, b2, w3, freq, bias):
    pos = np.arange(L, dtype=np.float64)
    posr = np.where(pos == 0, 0.0, L - pos)

    def feats(p):
        t = p / max(L - 1, 1)
        bands = np.linspace(1e-4, FILTER_BANDS - 1, FILTER_BANDS)
        ang = (2 * math.pi / L) * p[:, None] * bands
        return np.concatenate([t[:, None], np.cos(ang), np.sin(ang)], axis=-1), t

    f0, t0 = feats(pos)
    f1, t1 = feats(posr)
    fe = 40
    ft = np.zeros((fe, 2 * L))
    ft[:FILTER_EMB, :L] = f0.T
    ft[:FILTER_EMB, L:] = f1.T
    hid = filter_hidden(jnp.asarray(ft, F32), w1, b1, w2, b2, freq)
    hid = jnp.stack([hid[:, :L], hid[:, L:]], axis=0)
    tpos = jnp.asarray(np.stack([t0, t1])[:, None, :], F32)
    mask = jnp.asarray(np.stack([np.ones(L), (pos > 0).astype(np.float64)])[:, None, :], F32)
    e0 = jnp.asarray((pos == 0).astype(np.float64)[None, :], F32)
    max_decay = math.log(DECAY_TARGET) / FAST_DECAY_PCT
    min_decay = math.log(DECAY_TARGET) / SLOW_DECAY_PCT
    deltas = jnp.asarray(np.abs(np.linspace(min_decay, max_decay, C))[:, None], F32)
    w3t = w3.T.reshape(HYENA_ORDER * 2, C, FILTER_HIDDEN)
    bias_aug = jnp.stack([bias, jnp.zeros_like(bias)], axis=1).reshape(HYENA_ORDER * 2, C, 1)
    k = filter_k(w3t, hid, tpos, mask, e0, deltas, bias_aug)
    return k.reshape(HYENA_ORDER, 2, C, L)


def _fft_consts():
    def emb(fr, fi):
        return np.block([[fr, fi], [-fi, fr]])

    n2 = LANES
    a = 2 * np.pi * np.outer(np.arange(n2), np.arange(n2)) / n2
    g128f = emb(np.cos(a), -np.sin(a))
    g128i = emb(np.cos(a), np.sin(a)) / (2.0 * FFT_TA * LANES)
    b = 2 * np.pi * np.outer(np.arange(FFT_TA), np.arange(FFT_TA)) / FFT_TA
    eye = np.eye(LANES // FFT_TA)
    g32f = emb(np.kron(eye, np.cos(b)), np.kron(eye, -np.sin(b)))
    g32i = emb(np.kron(eye, np.cos(b)), np.kron(eye, np.sin(b)))
    mats = jnp.asarray(np.stack([g128f, g128i, g32f, g32i]), BF16)
    n = FFT_TA * LANES
    fa = np.tile(np.arange(FFT_TA), LANES // FFT_TA)[:, None]
    tb = np.arange(LANES)[None, :]
    th = 2 * np.pi * fa * tb / n
    ph = 2 * np.pi * (fa * LANES + tb) / (2 * n)
    tw = jnp.asarray(np.stack([np.cos(th), np.sin(th), np.cos(ph), np.sin(ph)]), F32)
    return mats, tw


def _dft_fwd(re, im, g32, g128, twc, tws):
    rb = re.shape[0]
    parts = []
    for g in range(rb // LANES):
        sl = slice(g * LANES, (g + 1) * LANES)
        if im is None:
            parts.append(re[sl, :].T)
        else:
            parts.append(jnp.concatenate([re[sl, :].T, im[sl, :].T], axis=1))
    lhs = jnp.concatenate(parts, axis=0).astype(BF16)
    gm = g32 if im is not None else g32[:LANES, :]
    o1 = jnp.dot(lhs, gm, preferred_element_type=F32)
    parts = []
    for g in range(rb // LANES):
        sl = slice(g * LANES, (g + 1) * LANES)
        r = o1[sl, :LANES].T
        i = o1[sl, LANES:].T
        parts.append(jnp.concatenate([r * twc + i * tws, i * twc - r * tws], axis=1))
    lhs2 = jnp.concatenate(parts, axis=0).astype(BF16)
    o2 = jnp.dot(lhs2, g128, preferred_element_type=F32)
    return o2[:, :LANES], o2[:, LANES:]


def _dft_inv(re, im, g128, g32, twc, tws):
    rb = re.shape[0]
    lhs = jnp.concatenate([re, im], axis=1).astype(BF16)
    o1 = jnp.dot(lhs, g128, preferred_element_type=F32)
    parts = []
    for g in range(rb // LANES):
        sl = slice(g * LANES, (g + 1) * LANES)
        r = o1[sl, :LANES]
        i = o1[sl, LANES:]
        parts.append(jnp.concatenate([(r * twc - i * tws).T, (i * twc + r * tws).T], axis=1))
    lhs2 = jnp.concatenate(parts, axis=0).astype(BF16)
    o2 = jnp.dot(lhs2, g32, preferred_element_type=F32)
    rs, is_ = [], []
    for g in range(rb // LANES):
        sl = slice(g * LANES, (g + 1) * LANES)
        rs.append(o2[sl, :LANES].T)
        is_.append(o2[sl, LANES:].T)
    return jnp.concatenate(rs, axis=0), jnp.concatenate(is_, axis=0)


def _hyena_long_kernel(vr_ref, vi_ref, x1r_ref, x1i_ref, x2r_ref, x2i_ref,
                       k0lo_ref, k0hi_ref, k1lo_ref, k1hi_ref, mats_ref, tw_ref, or_ref, oi_ref):
    g128f, g128i, g32f, g32i = mats_ref[0], mats_ref[1], mats_ref[2], mats_ref[3]
    rb = vr_ref.shape[2]
    rep = rb // LANES

    def tile(t):
        return jnp.concatenate([t] * rep, axis=0)

    twc, tws = tw_ref[0], tw_ref[1]
    pc, ps = tile(tw_ref[2]), tile(tw_ref[3])

    def conv(zr, zi, klo, khi):
        ke = klo + khi
        kd = klo - khi
        ker, kei = _dft_fwd(ke, None, g32f, g128f, twc, tws)
        kor, koi = _dft_fwd(kd * pc, -(kd * ps), g32f, g128f, twc, tws)
        er, ei = _dft_fwd(zr, zi, g32f, g128f, twc, tws)
        o_r, o_i = _dft_fwd(zr * pc + zi * ps, zi * pc - zr * ps, g32f, g128f, twc, tws)
        yer, yei = _dft_inv(er * ker - ei * kei, er * kei + ei * ker, g128i, g32i, twc, tws)
        yor, yoi = _dft_inv(o_r * kor - o_i * koi, o_r * koi + o_i * kor, g128i, g32i, twc, tws)
        return yer + yor * pc - yoi * ps, yei + yoi * pc + yor * ps

    yr, yi = conv(vr_ref[0, 0], vi_ref[0, 0], k0lo_ref[0, 0], k0hi_ref[0, 0])
    z1r = x1r_ref[0, 0] * yr
    z1i = x1i_ref[0, 0] * yi
    yr, yi = conv(z1r, z1i, k1lo_ref[0, 0], k1hi_ref[0, 0])
    or_ref[0] = x2r_ref[0, 0] * yr
    oi_ref[0] = x2i_ref[0, 0] * yi


def hyena_long(ut, kt):
    _, _, R, _ = ut.shape
    rb = 32 * FFT_TA
    mats, tw = _fft_consts()

    def uspec(b, p):
        return pl.BlockSpec((1, 1, rb, LANES), lambda i: (b, p, i, 0))

    def kspec(o, d):
        return pl.BlockSpec((1, 1, rb, LANES), lambda i: (o, d, i, 0))

    outr, outi = pl.pallas_call(
        _hyena_long_kernel,
        grid=(R // rb,),
        in_specs=[uspec(0, 0), uspec(1, 0), uspec(0, 1), uspec(1, 1), uspec(0, 2), uspec(1, 2),
                  kspec(0, 0), kspec(0, 1), kspec(1, 0), kspec(1, 1),
                  pl.BlockSpec((4, 2 * LANES, 2 * LANES), lambda i: (0, 0, 0)),
                  pl.BlockSpec((4, LANES, LANES), lambda i: (0, 0, 0))],
        out_specs=[pl.BlockSpec((1, rb, LANES), lambda i: (0, i, 0)),
                   pl.BlockSpec((1, rb, LANES), lambda i: (0, i, 0))],
        out_shape=[jax.ShapeDtypeStruct((1, R, LANES), F32), jax.ShapeDtypeStruct((1, R, LANES), F32)],
        compiler_params=_params(("arbitrary",)),
        name="hyena_long",
    )(ut, ut, ut, ut, ut, ut, kt, kt, kt, kt, mats, tw)
    return jnp.concatenate([outr, outi], axis=0)


def _hyena_short_kernel(vr_ref, vi_ref, x1r_ref, x1i_ref, x2r_ref, x2i_ref,
                        k0lo_ref, k0hi_ref, k1lo_ref, k1hi_ref, gf_ref, gi_ref, or_ref, oi_ref):
    L = vr_ref.shape[-1]
    n = 2 * L

    def conv(zr, zi, klo, khi):
        kk = jnp.concatenate([klo, khi], axis=1).astype(BF16)
        ks = jnp.dot(kk, gf_ref[:n, :], preferred_element_type=F32)
        kr, ki = ks[:, :n], ks[:, n:]
        zz = jnp.concatenate([zr, zi], axis=1).astype(BF16)
        zs = jnp.dot(zz, gf_ref[n:, :], preferred_element_type=F32)
        sr, si = zs[:, :n], zs[:, n:]
        pr = sr * kr - si * ki
        pi = sr * ki + si * kr
        y = jnp.dot(jnp.concatenate([pr, pi], axis=1).astype(BF16), gi_ref[...], preferred_element_type=F32)
        return y[:, :L], y[:, L:]

    yr, yi = conv(vr_ref[0, 0], vi_ref[0, 0], k0lo_ref[0, 0], k0hi_ref[0, 0])
    z1r = x1r_ref[0, 0] * yr
    z1i = x1i_ref[0, 0] * yi
    yr, yi = conv(z1r, z1i, k1lo_ref[0, 0], k1hi_ref[0, 0])
    or_ref[0] = x2r_ref[0, 0] * yr
    oi_ref[0] = x2i_ref[0, 0] * yi


def hyena_short(ut, kt):
    _, _, C, L = ut.shape
    n = 2 * L
    t = np.arange(n)
    a = 2 * np.pi * np.outer(t, t) / n
    co, si = np.cos(a), np.sin(a)
    gf = np.concatenate([
        np.concatenate([co, -si], axis=1),
        np.concatenate([co[:L], -si[:L]], axis=1),
        np.concatenate([si[:L], co[:L]], axis=1)], axis=0)
    gi = np.concatenate([
        np.concatenate([co[:, :L], si[:, :L]], axis=1),
        np.concatenate([-si[:, :L], co[:, :L]], axis=1)], axis=0) / n
    rc = 256

    def uspec(b, p):
        return pl.BlockSpec((1, 1, rc, L), lambda i: (b, p, i, 0))

    outr, outi = pl.pallas_call(
        _hyena_short_kernel,
        grid=(C // rc,),
        in_specs=[uspec(0, 0), uspec(1, 0), uspec(0, 1), uspec(1, 1), uspec(0, 2), uspec(1, 2),
                  uspec(0, 0), uspec(0, 1), uspec(1, 0), uspec(1, 1),
                  pl.BlockSpec((2 * n, 2 * n), lambda i: (0, 0)),
                  pl.BlockSpec((2 * n, n), lambda i: (0, 0))],
        out_specs=[pl.BlockSpec((1, rc, L), lambda i: (0, i, 0)),
                   pl.BlockSpec((1, rc, L), lambda i: (0, i, 0))],
        out_shape=[jax.ShapeDtypeStruct((1, C, L), F32), jax.ShapeDtypeStruct((1, C, L), F32)],
        compiler_params=_params(("arbitrary",)),
        name="hyena_short",
    )(ut, ut, ut, ut, ut, ut, kt, kt, kt, kt, jnp.asarray(gf, BF16), jnp.asarray(gi, BF16))
    return jnp.concatenate([outr, outi], axis=0)


def hyena_mixer(x, col0, conv_w, conv_b, w1, b1, w2, b2, w3, freq, bias):
    B, L, _ = x.shape
    C = bias.shape[-1]
    assert B == 2
    uc = short_conv(x, col0, 3 * C, conv_w, conv_b)
    ut = uc.reshape(B, L, 3, C).transpose(0, 2, 3, 1)
    kt = hyena_filter_taps(L, C, w1, b1, w2, b2, w3, freq, bias)
    if L == FFT_TA * LANES:
        z = hyena_long(ut.reshape(B, 3, C * FFT_TA, LANES), kt.reshape(HYENA_ORDER, 2, C * FFT_TA, LANES))
        z = z.reshape(B, C, L)
    else:
        z = hyena_short(ut, kt)
    return z.transpose(0, 2, 1)


def _topk_kernel(l_ref, idx_ref, gate_ref):
    l = l_ref[...]
    E = l.shape[1]
    lane = lax.broadcasted_iota(jnp.int32, l.shape, 1).astype(F32)
    vals, idxs = [], []
    for _ in range(TOP_K):
        m = jnp.max(l, axis=-1, keepdims=True)
        am = jnp.min(jnp.where(l == m, lane, float(E)), axis=-1, keepdims=True)
        vals.append(m)
        idxs.append(am)
        l = jnp.where(lane == am, -jnp.inf, l)
    v = jnp.concatenate(vals, axis=1)
    e = jnp.exp(v - vals[0])
    gate_ref[...] = e / jnp.sum(e, axis=-1, keepdims=True)
    idx_ref[...] = jnp.concatenate(idxs, axis=1).astype(jnp.int32)


def topk_gates(logits):
    N, E = logits.shape
    tm = 512
    return pl.pallas_call(
        _topk_kernel,
        grid=(N // tm,),
        in_specs=[pl.BlockSpec((tm, E), lambda i: (i, 0))],
        out_specs=[pl.BlockSpec((tm, TOP_K), lambda i: (i, 0)), pl.BlockSpec((tm, TOP_K), lambda i: (i, 0))],
        out_shape=[jax.ShapeDtypeStruct((N, TOP_K), jnp.int32), jax.ShapeDtypeStruct((N, TOP_K), F32)],
        compiler_params=_params(("arbitrary",)),
        name="topk_gates",
    )(logits)


def _rank_kernel(idx_ref, rank_ref, cnt_ref, carry_ref):
    @pl.when(pl.program_id(0) == 0)
    def _():
        carry_ref[...] = jnp.zeros_like(carry_ref)

    idx = idx_ref[...]
    tm = idx.shape[0]
    E = carry_ref.shape[1]
    e_iota = lax.broadcasted_iota(jnp.int32, (tm, E), 1)
    sel = [idx[:, k:k + 1] == e_iota for k in range(TOP_K)]
    m = jnp.zeros((tm, E), F32)
    for s in sel:
        m = m + jnp.where(s, 1.0, 0.0)
    r_i = lax.broadcasted_iota(jnp.int32, (tm, tm), 0)
    c_i = lax.broadcasted_iota(jnp.int32, (tm, tm), 1)
    tri = jnp.where(r_i > c_i, 1.0, 0.0).astype(BF16)
    before = jnp.dot(tri, m.astype(BF16), preferred_element_type=F32) + carry_ref[...]
    ranks = [jnp.sum(jnp.where(s, before, 0.0), axis=-1, keepdims=True) for s in sel]
    rank_ref[...] = jnp.concatenate(ranks, axis=1).astype(jnp.int32)
    carry_ref[...] = carry_ref[...] + jnp.sum(m, axis=0, keepdims=True)
    cnt_ref[...] = carry_ref[...]


def expert_ranks(top_idx):
    N, _ = top_idx.shape
    tm = 512
    return pl.pallas_call(
        _rank_kernel,
        grid=(N // tm,),
        in_specs=[pl.BlockSpec((tm, TOP_K), lambda i: (i, 0))],
        out_specs=[pl.BlockSpec((tm, TOP_K), lambda i: (i, 0)), pl.BlockSpec((1, N_EXPERTS), lambda i: (0, 0))],
        out_shape=[jax.ShapeDtypeStruct((N, TOP_K), jnp.int32), jax.ShapeDtypeStruct((1, N_EXPERTS), F32)],
        scratch_shapes=[pltpu.VMEM((1, N_EXPERTS), F32)],
        compiler_params=_params(("arbitrary",)),
        name="expert_ranks",
    )(top_idx)


def _gather_kernel(rowtok_ref, nreal_ref, *refs, n_src, src_rows):
    srcs = refs[:n_src]
    o_ref, buf, sem = refs[n_src:]
    i = pl.program_id(0)
    tb = buf.shape[0]

    @pl.when(i < nreal_ref[0])
    def _():
        def issue(r, carry):
            tok = rowtok_ref[i * tb + r]
            lo = 0
            for s in range(n_src):
                hi = lo + src_rows[s]

                @pl.when((tok >= lo) & (tok < hi))
                def _(s=s, lo=lo):
                    pltpu.make_async_copy(srcs[s].at[pl.ds(tok - lo, 1), :], buf.at[pl.ds(r, 1), :], sem).start()
                lo = hi
            return carry

        lax.fori_loop(0, tb, issue, 0)
        pltpu.make_async_copy(srcs[0].at[pl.ds(0, tb), :], buf, sem).wait()
        o_ref[...] = buf[...].astype(o_ref.dtype)

    @pl.when(i >= nreal_ref[0])
    def _():
        o_ref[...] = jnp.zeros_like(o_ref)


def gather_rows(row_tok, n_real, srcs, n_blocks):
    D = srcs[0].shape[1]
    tb = MOE_ROWS
    kern = functools.partial(_gather_kernel, n_src=len(srcs), src_rows=tuple(s.shape[0] for s in srcs))
    return pl.pallas_call(
        kern,
        grid_spec=pltpu.PrefetchScalarGridSpec(
            num_scalar_prefetch=2,
            grid=(n_blocks,),
            in_specs=[pl.BlockSpec(memory_space=pl.ANY)] * len(srcs),
            out_specs=pl.BlockSpec((tb, D), lambda i, rt, nr: (i, 0)),
            scratch_shapes=[pltpu.VMEM((tb, D), F32), pltpu.SemaphoreType.DMA(())],
        ),
        out_shape=jax.ShapeDtypeStruct((n_blocks * tb, D), BF16),
        compiler_params=_params(("arbitrary",)),
        name="moe_gather",
    )(row_tok, n_real, *srcs)


def _moe_mm_kernel(be_ref, nreal_ref, xs_ref, w1_ref, b1_ref, w2_ref, b2_ref, o_ref):
    i = pl.program_id(0)
    de = w2_ref.shape[1]

    @pl.when(i < nreal_ref[0])
    def _():
        gu = jnp.dot(xs_ref[...], w1_ref[0], preferred_element_type=F32) + b1_ref[0]
        g = jnp.minimum(gu[:, :de], SWIGLU_LIMIT)
        up = jnp.clip(gu[:, de:], -SWIGLU_LIMIT, SWIGLU_LIMIT)
        act = (up + 1.0) * g * jax.nn.sigmoid(SWIGLU_ALPHA * g)
        o_ref[...] = jnp.dot(act.astype(BF16), w2_ref[0], preferred_element_type=F32) + b2_ref[0]

    @pl.when(i >= nreal_ref[0])
    def _():
        o_ref[...] = jnp.zeros_like(o_ref)


def moe_grouped_mm(block_e, n_real, xs, w1, b1, w2, b2):
    R, D = xs.shape
    E, _, de2 = w1.shape
    de = de2 // 2
    tb = MOE_ROWS
    nb = R // tb
    return pl.pallas_call(
        _moe_mm_kernel,
        grid_spec=pltpu.PrefetchScalarGridSpec(
            num_scalar_prefetch=2,
            grid=(nb,),
            in_specs=[
                pl.BlockSpec((tb, D), lambda i, be, nr: (jnp.minimum(i, nr[0] - 1), 0)),
                pl.BlockSpec((1, D, de2), lambda i, be, nr: (be[i], 0, 0)),
                pl.BlockSpec((1, 1, de2), lambda i, be, nr: (be[i], 0, 0)),
                pl.BlockSpec((1, de, D), lambda i, be, nr: (be[i], 0, 0)),
                pl.BlockSpec((1, 1, D), lambda i, be, nr: (be[i], 0, 0)),
            ],
            out_specs=pl.BlockSpec((tb, D), lambda i, be, nr: (i, 0)),
        ),
        out_shape=jax.ShapeDtypeStruct((R, D), F32),
        compiler_params=_params(("arbitrary",)),
        name="moe_grouped_mm",
    )(block_e, n_real, xs, w1, b1.reshape(E, 1, de2), w2, b2.reshape(E, 1, D))


def _combine_kernel(dest_ref, ys_ref, gates_ref, x_ref, gt_ref, o_ref, buf, sem, *, tok0):
    b = pl.program_id(0)
    i = pl.program_id(1)
    tm = buf.shape[1]
    base = (tok0 + (b * pl.num_programs(1) + i) * tm) * TOP_K

    def issue(r, carry):
        for k in range(TOP_K):
            d = dest_ref[base + r * TOP_K + k]
            pltpu.make_async_copy(ys_ref.at[pl.ds(d, 1), :], buf.at[k, pl.ds(r, 1), :], sem).start()
        return carry

    lax.fori_loop(0, tm, issue, 0)
    for k in range(TOP_K):
        pltpu.make_async_copy(ys_ref.at[pl.ds(0, tm), :], buf.at[k], sem).wait()
    gates = gates_ref[...]
    acc = gates[:, 0:1] * buf[0]
    for k in range(1, TOP_K):
        acc = acc + gates[:, k:k + 1] * buf[k]
    o_ref[0] = x_ref[0] + gt_ref[0] * acc


def moe_combine(dest, ys, gates, x, gate_vec, tok0):
    B, T, D = x.shape
    tm = 128
    nt = T // tm
    g0 = tok0 // tm
    gmap = (lambda b, i, d: (b, 0, 0)) if gate_vec.shape[0] > 1 else (lambda b, i, d: (0, 0, 0))
    return pl.pallas_call(
        functools.partial(_combine_kernel, tok0=tok0),
        grid_spec=pltpu.PrefetchScalarGridSpec(
            num_scalar_prefetch=1,
            grid=(B, nt),
            in_specs=[
                pl.BlockSpec(memory_space=pl.ANY),
                pl.BlockSpec((tm, TOP_K), lambda b, i, d: (g0 + b * nt + i, 0)),
                pl.BlockSpec((1, tm, D), lambda b, i, d: (b, i, 0)),
                pl.BlockSpec((1, 1, D), gmap),
            ],
            out_specs=pl.BlockSpec((1, tm, D), lambda b, i, d: (b, i, 0)),
            scratch_shapes=[pltpu.VMEM((TOP_K, tm, D), F32), pltpu.SemaphoreType.DMA(())],
        ),
        out_shape=jax.ShapeDtypeStruct((B, T, D), F32),
        compiler_params=_params(("arbitrary", "arbitrary")),
        name="moe_combine",
    )(dest, ys, gates, x, gate_vec)


def moe_route(logits):
    N = logits.shape[0]
    tb = MOE_ROWS
    top_idx, gates = topk_gates(logits)
    rank, counts = expert_ranks(top_idx)
    counts = counts.reshape(N_EXPERTS).astype(jnp.int32)
    padded = (counts + tb - 1) // tb * tb
    pad_end = jnp.cumsum(padded)
    pad_start = pad_end - padded
    dest = pad_start[top_idx] + rank
    n_blocks = -(-(N * TOP_K) // tb) + N_EXPERTS
    n_real = (pad_end[-1] // tb).astype(jnp.int32).reshape(1)
    block_e = jnp.minimum(
        jnp.searchsorted(pad_end, jnp.arange(n_blocks, dtype=jnp.int32) * tb, side='right'),
        N_EXPERTS - 1).astype(jnp.int32)
    tok_of = jnp.arange(N * TOP_K, dtype=jnp.int32) // TOP_K
    row_tok = jnp.zeros((n_blocks * tb,), jnp.int32).at[dest.reshape(-1)].set(tok_of)
    return dest.reshape(-1).astype(jnp.int32), gates, row_tok, block_e, n_real, n_blocks


def moe_ffn(hs, logits, w1, b1, w2, b2):
    dest, gates, row_tok, block_e, n_real, n_blocks = moe_route(logits)
    xs = gather_rows(row_tok, n_real, hs, n_blocks)
    ys = moe_grouped_mm(block_e, n_real, xs, w1, b1, w2, b2)
    return dest, gates, ys


def kernel(x, c, ctx, c_ctx, w_ada, b_ada, g_mix, g_ffn, w_in, q_norm, k_norm, hy_conv_w, hy_conv_b, hy_w1, hy_b1, hy_w2, hy_b2, hy_w3, hy_freq, hy_bias, g_out, w_out, w_router, b_router, moe_w1, moe_b1, moe_w2, moe_b2, g_final):
    B, T, D = x.shape
    C = ctx.shape[1]
    depth = w_ada.shape[0]
    wq = N_HEADS * HEAD_DIM
    wkv = N_KV_HEADS * HEAD_DIM
    col_k, col_v, col_u = wq, wq + wkv, wq + 2 * wkv
    cos_t, sin_t = rope_tables(T)
    ones_c = jnp.ones((C, HEAD_DIM), F32)
    scale = HEAD_DIM ** -0.5

    cvecs = jnp.concatenate([c, c_ctx[None, :], jnp.zeros((8 - B - 1, D), F32)], axis=0)
    ada = adaln_all(cvecs, w_ada, b_ada)

    xc = ctx
    for i in range(depth):
        last = i == depth - 1
        mod = [ada[i, :B, j * D:(j + 1) * D].reshape(B, 1, D) for j in range(6)]
        cmod = [ada[i, B:B + 1, j * D:(j + 1) * D].reshape(1, 1, D) for j in range(6)]
        hp = (hy_conv_w[i], hy_conv_b[i], hy_w1[i], hy_b1[i], hy_w2[i], hy_b2[i], hy_w3[i], hy_freq[i], hy_bias[i])
        w_in_b = w_in[i].astype(BF16)
        w_out_b = w_out[i].astype(BF16)

        h = norm_mod(x, g_mix[i], mod[0], mod[1])
        hc = norm_mod(xc, g_mix[i], cmod[0], cmod[1])
        qkvu = matmul(h, w_in_b)
        q = head_norm(qkvu, 0, N_HEADS, q_norm[i], cos_t, sin_t, True, scale)
        k = head_norm(qkvu, col_k, N_KV_HEADS, k_norm[i], cos_t, sin_t, True, 1.0)
        v = qkvu[:, :, col_v:col_u].astype(BF16)
        if last:
            kvc = matmul(hc, w_in_b[:, col_k:col_u])
            kc = head_norm(kvc, 0, N_KV_HEADS, k_norm[i], ones_c, ones_c, False, 1.0)
            vc = kvc[:, :, wkv:].astype(BF16)
        else:
            qkvuc = matmul(hc, w_in_b)
            qc = head_norm(qkvuc, 0, N_HEADS, q_norm[i], ones_c, ones_c, False, scale)
            kc = head_norm(qkvuc, col_k, N_KV_HEADS, k_norm[i], ones_c, ones_c, False, 1.0)
            vc = qkvuc[:, :, col_v:col_u].astype(BF16)
        y_attn = attention(q, jnp.concatenate([kc, k], axis=1), jnp.concatenate([vc, v], axis=1))
        y_hy = hyena_mixer(qkvu, col_u, *hp)
        x = matmul_residual(mixnorm(y_attn, y_hy, g_out[i]), w_out_b, x, mod[2])
        if not last:
            yc_attn = attention(qc, kc, vc)
            yc_hy = hyena_mixer(qkvuc, col_u, *hp)
            xc = matmul_residual(mixnorm(yc_attn, yc_hy, g_out[i]), w_out_b, xc, cmod[2])

        w1_b = moe_w1[i].astype(BF16)
        w2_b = moe_w2[i].astype(BF16)
        hf, lg = norm_mod_router(x, g_ffn[i], mod[3], mod[4], w_router[i], b_router[i])
        hs = [hf.reshape(B * T, D)]
        lgs = [lg.reshape(B * T, N_EXPERTS)]
        if not last:
            hfc, lgc = norm_mod_router(xc, g_ffn[i], cmod[3], cmod[4], w_router[i], b_router[i])
            hs.append(hfc.reshape(B * C, D))
            lgs.append(lgc.reshape(B * C, N_EXPERTS))
        logits = jnp.concatenate(lgs, axis=0) if len(lgs) > 1 else lgs[0]
        dest, gates, ys = moe_ffn(hs, logits, w1_b, moe_b1[i], w2_b, moe_b2[i])
        x = moe_combine(dest, ys, gates, x, mod[5], 0)
        if not last:
            xc = moe_combine(dest, ys, gates, xc, cmod[5], B * T)
    return final_norm(x, g_final)
```

```python
import functools
import math

import numpy as np
import jax
import jax.numpy as jnp
from jax import lax
from jax.experimental import pallas as pl
from jax.experimental.pallas import tpu as pltpu

F32 = jnp.float32
BF16 = jnp.bfloat16

GRID_W = 64
HEAD_DIM = 128
N_HEADS = 16
N_KV_HEADS = 4
GROUP = N_HEADS // N_KV_HEADS
HYENA_ORDER = 2
FILTER_BANDS = 16
FILTER_EMB = 1 + 2 * FILTER_BANDS
FILTER_HIDDEN = 64
DECAY_TARGET = 1e-2
FAST_DECAY_PCT = 0.3
SLOW_DECAY_PCT = 1.5
ROPE_THETA = 10000.0
N_EXPERTS = 32
TOP_K = 4
SWIGLU_LIMIT = 7.0
SWIGLU_ALPHA = 1.702
EPS = 1e-6

LANES = 128
V7X_VMEM_BYTES = 64 * 1024 * 1024
VMEM_LIMIT = V7X_VMEM_BYTES - 8 * 1024 * 1024
FFT_TA = 32
MOE_ROWS = 256


def _params(sem):
    return pltpu.CompilerParams(dimension_semantics=sem, vmem_limit_bytes=VMEM_LIMIT)


def _adaln_kernel(c_ref, w_ref, b_ref, o_ref):
    c = c_ref[...]
    s = (c * jax.nn.sigmoid(c)).astype(BF16)
    o_ref[0] = jnp.dot(s, w_ref[0].astype(BF16), preferred_element_type=F32) + b_ref[0]


def adaln_all(cvecs, w_ada, b_ada):
    L, D, N = w_ada.shape
    tn = 512
    return pl.pallas_call(
        _adaln_kernel,
        grid=(L, N // tn),
        in_specs=[
            pl.BlockSpec((8, D), lambda l, j: (0, 0)),
            pl.BlockSpec((1, D, tn), lambda l, j: (l, 0, j)),
            pl.BlockSpec((1, 1, tn), lambda l, j: (l, 0, j)),
        ],
        out_specs=pl.BlockSpec((1, 8, tn), lambda l, j: (l, 0, j)),
        out_shape=jax.ShapeDtypeStruct((L, 8, N), F32),
        compiler_params=_params(("arbitrary", "arbitrary")),
        name="adaln",
    )(cvecs, w_ada, b_ada.reshape(L, 1, N))


def _norm_mod_kernel(x_ref, g_ref, sh_ref, sc_ref, h_ref):
    x = x_ref[0]
    y = x * lax.rsqrt(jnp.mean(x * x, axis=-1, keepdims=True) + EPS) * g_ref[...]
    h_ref[0] = (y * (1.0 + sc_ref[0]) + sh_ref[0]).astype(h_ref.dtype)


def _norm_mod_router_kernel(x_ref, g_ref, sh_ref, sc_ref, wr_ref, br_ref, h_ref, l_ref):
    x = x_ref[0]
    y = x * lax.rsqrt(jnp.mean(x * x, axis=-1, keepdims=True) + EPS) * g_ref[...]
    h = y * (1.0 + sc_ref[0]) + sh_ref[0]
    h_ref[0] = h
    l_ref[0] = jnp.dot(h, wr_ref[...], preferred_element_type=F32,
                       precision=lax.Precision.HIGHEST) + br_ref[...]


def _mod_map(bm):
    return (lambda b, i: (b, 0, 0)) if bm > 1 else (lambda b, i: (0, 0, 0))


def norm_mod(x, g, shift, scale, out_dtype=BF16):
    B, T, D = x.shape
    tm = min(T, 256)
    return pl.pallas_call(
        _norm_mod_kernel,
        grid=(B, T // tm),
        in_specs=[
            pl.BlockSpec((1, tm, D), lambda b, i: (b, i, 0)),
            pl.BlockSpec((1, D), lambda b, i: (0, 0)),
            pl.BlockSpec((1, 1, D), _mod_map(shift.shape[0])),
            pl.BlockSpec((1, 1, D), _mod_map(scale.shape[0])),
        ],
        out_specs=pl.BlockSpec((1, tm, D), lambda b, i: (b, i, 0)),
        out_shape=jax.ShapeDtypeStruct((B, T, D), out_dtype),
        compiler_params=_params(("arbitrary", "arbitrary")),
        name="norm_mod",
    )(x, g.reshape(1, D), shift, scale)


def norm_mod_router(x, g, shift, scale, w_router, b_router):
    B, T, D = x.shape
    E = w_router.shape[1]
    tm = min(T, 256)
    return pl.pallas_call(
        _norm_mod_router_kernel,
        grid=(B, T // tm),
        in_specs=[
            pl.BlockSpec((1, tm, D), lambda b, i: (b, i, 0)),
            pl.BlockSpec((1, D), lambda b, i: (0, 0)),
            pl.BlockSpec((1, 1, D), _mod_map(shift.shape[0])),
            pl.BlockSpec((1, 1, D), _mod_map(scale.shape[0])),
            pl.BlockSpec((D, E), lambda b, i: (0, 0)),
            pl.BlockSpec((1, E), lambda b, i: (0, 0)),
        ],
        out_specs=[
            pl.BlockSpec((1, tm, D), lambda b, i: (b, i, 0)),
            pl.BlockSpec((1, tm, E), lambda b, i: (b, i, 0)),
        ],
        out_shape=[jax.ShapeDtypeStruct((B, T, D), F32), jax.ShapeDtypeStruct((B, T, E), F32)],
        compiler_params=_params(("arbitrary", "arbitrary")),
        name="norm_mod_router",
    )(x, g.reshape(1, D), shift, scale, w_router, b_router.reshape(1, E))


def _mixnorm_kernel(a_ref, hy_ref, g_ref, o_ref):
    wa = a_ref.shape[-1]
    a = a_ref[0]
    b = hy_ref[0]
    g = g_ref[...]
    o_ref[0, :, :wa] = (a * lax.rsqrt(jnp.mean(a * a, axis=-1, keepdims=True) + EPS) * g[:, :wa]).astype(o_ref.dtype)
    o_ref[0, :, wa:] = (b * lax.rsqrt(jnp.mean(b * b, axis=-1, keepdims=True) + EPS) * g[:, wa:]).astype(o_ref.dtype)


def mixnorm(y_attn, y_hy, g):
    B, T, wa = y_attn.shape
    wh = y_hy.shape[-1]
    tm = min(T, 256)
    return pl.pallas_call(
        _mixnorm_kernel,
        grid=(B, T // tm),
        in_specs=[
            pl.BlockSpec((1, tm, wa), lambda b, i: (b, i, 0)),
            pl.BlockSpec((1, tm, wh), lambda b, i: (b, i, 0)),
            pl.BlockSpec((1, wa + wh), lambda b, i: (0, 0)),
        ],
        out_specs=pl.BlockSpec((1, tm, wa + wh), lambda b, i: (b, i, 0)),
        out_shape=jax.ShapeDtypeStruct((B, T, wa + wh), BF16),
        compiler_params=_params(("arbitrary", "arbitrary")),
        name="mixnorm",
    )(y_attn, y_hy, g.reshape(1, wa + wh))


def _final_norm_kernel(x_ref, g_ref, o_ref):
    x = x_ref[0]
    o_ref[0] = x * lax.rsqrt(jnp.mean(x * x, axis=-1, keepdims=True) + EPS) * g_ref[...]


def final_norm(x, g):
    B, T, D = x.shape
    tm = min(T, 256)
    return pl.pallas_call(
        _final_norm_kernel,
        grid=(B, T // tm),
        in_specs=[pl.BlockSpec((1, tm, D), lambda b, i: (b, i, 0)), pl.BlockSpec((1, D), lambda b, i: (0, 0))],
        out_specs=pl.BlockSpec((1, tm, D), lambda b, i: (b, i, 0)),
        out_shape=jax.ShapeDtypeStruct((B, T, D), F32),
        compiler_params=_params(("arbitrary", "arbitrary")),
        name="final_norm",
    )(x, g.reshape(1, D))


def _cast_weight_once(w_ref, wb_ref):
    @pl.when((pl.program_id(1) == 0) & (pl.program_id(2) == 0))
    def _():
        wb_ref[...] = w_ref[...].astype(BF16)


def _mm_kernel(x_ref, w_ref, o_ref, wb_ref):
    _cast_weight_once(w_ref, wb_ref)
    o_ref[0] = jnp.dot(x_ref[0], wb_ref[...], preferred_element_type=F32).astype(o_ref.dtype)


def _mm_res_kernel(x_ref, w_ref, r_ref, g_ref, o_ref, wb_ref):
    _cast_weight_once(w_ref, wb_ref)
    acc = jnp.dot(x_ref[0], wb_ref[...], preferred_element_type=F32)
    o_ref[0] = r_ref[0] + g_ref[0] * acc


MM_TN = 512


def matmul(x, w, layer, col0=0, n_out=None, out_dtype=F32):
    B, T, K = x.shape
    N = n_out or w.shape[2]
    tm, tn = min(T, 1024), MM_TN
    c0 = col0 // tn
    assert c0 * tn == col0 and N % tn == 0
    return pl.pallas_call(
        _mm_kernel,
        grid=(N // tn, B, T // tm),
        in_specs=[
            pl.BlockSpec((1, tm, K), lambda j, b, i: (b, i, 0)),
            pl.BlockSpec((None, K, tn), lambda j, b, i: (layer, 0, c0 + j)),
        ],
        out_specs=pl.BlockSpec((1, tm, tn), lambda j, b, i: (b, i, j)),
        out_shape=jax.ShapeDtypeStruct((B, T, N), out_dtype),
        scratch_shapes=[pltpu.VMEM((K, tn), BF16)],
        compiler_params=_params(("arbitrary", "arbitrary", "arbitrary")),
        name="matmul",
    )(x, w)


def matmul_residual(x, w, layer, res, gate):
    B, T, K = x.shape
    N = w.shape[2]
    tm, tn = min(T, 1024), MM_TN
    gmap = (lambda j, b, i: (b, 0, j)) if gate.shape[0] > 1 else (lambda j, b, i: (0, 0, j))
    return pl.pallas_call(
        _mm_res_kernel,
        grid=(N // tn, B, T // tm),
        in_specs=[
            pl.BlockSpec((1, tm, K), lambda j, b, i: (b, i, 0)),
            pl.BlockSpec((None, K, tn), lambda j, b, i: (layer, 0, j)),
            pl.BlockSpec((1, tm, tn), lambda j, b, i: (b, i, j)),
            pl.BlockSpec((1, 1, tn), gmap),
        ],
        out_specs=pl.BlockSpec((1, tm, tn), lambda j, b, i: (b, i, j)),
        out_shape=jax.ShapeDtypeStruct((B, T, N), F32),
        scratch_shapes=[pltpu.VMEM((K, tn), BF16)],
        compiler_params=_params(("arbitrary", "arbitrary", "arbitrary")),
        name="matmul_residual",
    )(x, w, res, gate)


def _head_norm_kernel(x_ref, g_ref, cos_ref, sin_ref, o_ref, *, n_heads, rope, scale):
    g = g_ref[...]
    if rope:
        cs = cos_ref[...]
        sn = sin_ref[...]
        lane = lax.broadcasted_iota(jnp.int32, cs.shape, 1)
        first = (lane % 64) < 32
    for h in range(n_heads):
        xh = x_ref[0, :, h * HEAD_DIM:(h + 1) * HEAD_DIM]
        y = xh * lax.rsqrt(jnp.mean(xh * xh, axis=-1, keepdims=True) + EPS) * g
        if rope:
            swapped = jnp.where(first, pltpu.roll(y, 96, 1), pltpu.roll(y, 32, 1))
            y = y * cs + swapped * sn
        o_ref[0, :, h * HEAD_DIM:(h + 1) * HEAD_DIM] = (y * scale).astype(o_ref.dtype)


def head_norm(x, col0, n_heads, g, cos_t, sin_t, rope, scale):
    B, T, _ = x.shape
    w = n_heads * HEAD_DIM
    tm = min(T, 256)
    cb = col0 // w
    assert cb * w == col0
    kern = functools.partial(_head_norm_kernel, n_heads=n_heads, rope=rope, scale=scale)
    return pl.pallas_call(
        kern,
        grid=(B, T // tm),
        in_specs=[
            pl.BlockSpec((1, tm, w), lambda b, i: (b, i, cb)),
            pl.BlockSpec((1, HEAD_DIM), lambda b, i: (0, 0)),
            pl.BlockSpec((tm, HEAD_DIM), lambda b, i: (i, 0)),
            pl.BlockSpec((tm, HEAD_DIM), lambda b, i: (i, 0)),
        ],
        out_specs=pl.BlockSpec((1, tm, w), lambda b, i: (b, i, 0)),
        out_shape=jax.ShapeDtypeStruct((B, T, w), BF16),
        compiler_params=_params(("arbitrary", "arbitrary")),
        name="head_norm",
    )(x, g.reshape(1, HEAD_DIM), cos_t, sin_t)


def rope_tables(T):
    pos = np.arange(T)
    r, col = pos // GRID_W, pos % GRID_W
    n_freq = HEAD_DIM // 4
    inv = ROPE_THETA ** (-np.arange(n_freq, dtype=np.float64) / n_freq)
    ar, ac = r[:, None] * inv, col[:, None] * inv
    cos_t = np.concatenate([np.cos(ar), np.cos(ar), np.cos(ac), np.cos(ac)], axis=1)
    sin_t = np.concatenate([-np.sin(ar), np.sin(ar), -np.sin(ac), np.sin(ac)], axis=1)
    return jnp.asarray(cos_t, F32), jnp.asarray(sin_t, F32)


ATTN_TK = 256


def _attn_kernel(q_ref, k_ref, v_ref, o_ref):
    tq = q_ref.shape[1]
    S = k_ref.shape[1]
    q = q_ref[0]
    qs = jnp.concatenate([q[:, j * HEAD_DIM:(j + 1) * HEAD_DIM] for j in range(GROUP)], axis=0)
    rows = GROUP * tq
    m = jnp.full((rows, 1), -jnp.inf, F32)
    l = jnp.zeros((rows, 1), F32)
    acc = jnp.zeros((rows, HEAD_DIM), F32)
    for c in range(S // ATTN_TK):
        ks = slice(c * ATTN_TK, (c + 1) * ATTN_TK)
        s = lax.dot_general(qs, k_ref[0, ks, :], (((1,), (1,)), ((), ())), preferred_element_type=F32)
        m_new = jnp.maximum(m, jnp.max(s, axis=-1, keepdims=True))
        alpha = jnp.exp(m - m_new)
        p = jnp.exp(s - m_new)
        l = alpha * l + jnp.sum(p, axis=-1, keepdims=True)
        acc = alpha * acc + jnp.dot(p.astype(BF16), v_ref[0, ks, :], preferred_element_type=F32)
        m = m_new
    o = acc / l
    for j in range(GROUP):
        o_ref[0, :, j * HEAD_DIM:(j + 1) * HEAD_DIM] = o[j * tq:(j + 1) * tq, :]


def attention(q, k, v):
    B, T, _ = q.shape
    S = k.shape[1]
    assert S % ATTN_TK == 0
    tq = min(T, 128)
    gw = GROUP * HEAD_DIM
    return pl.pallas_call(
        _attn_kernel,
        grid=(B, N_KV_HEADS, T // tq),
        in_specs=[
            pl.BlockSpec((1, tq, gw), lambda b, g, i: (b, i, g)),
            pl.BlockSpec((1, S, HEAD_DIM), lambda b, g, i: (b, 0, g)),
            pl.BlockSpec((1, S, HEAD_DIM), lambda b, g, i: (b, 0, g)),
        ],
        out_specs=pl.BlockSpec((1, tq, gw), lambda b, g, i: (b, i, g)),
        out_shape=jax.ShapeDtypeStruct((B, T, N_HEADS * HEAD_DIM), F32),
        compiler_params=_params(("arbitrary", "arbitrary", "arbitrary")),
        name="attention",
    )(q, k, v)


def _short_conv_kernel(u_ref, w_ref, b_ref, o_ref):
    u = u_ref[0]
    L = u.shape[0]
    row = lax.broadcasted_iota(jnp.int32, u.shape, 0)
    prev = jnp.where(row == 0, 0.0, pltpu.roll(u, 1, 0))
    nxt = jnp.where(row == L - 1, 0.0, pltpu.roll(u, L - 1, 0))
    w = w_ref[...]
    o_ref[0] = prev * w[0:1, :] + u * w[1:2, :] + nxt * w[2:3, :] + b_ref[...]


def short_conv(x, col0, width, w, b):
    B, L, _ = x.shape
    cb = 256 if L > 1024 else 512
    assert col0 % cb == 0 and width % cb == 0
    off = col0 // cb
    return pl.pallas_call(
        _short_conv_kernel,
        grid=(B, width // cb),
        in_specs=[
            pl.BlockSpec((1, L, cb), lambda bi, j: (bi, 0, off + j)),
            pl.BlockSpec((3, cb), lambda bi, j: (0, j)),
            pl.BlockSpec((1, cb), lambda bi, j: (0, j)),
        ],
        out_specs=pl.BlockSpec((1, L, cb), lambda bi, j: (bi, 0, j)),
        out_shape=jax.ShapeDtypeStruct((B, L, width), F32),
        compiler_params=_params(("arbitrary", "arbitrary")),
        name="short_conv",
    )(x, w, b.reshape(1, width))


def _filter_hidden_kernel(f_ref, w1_ref, b1_ref, w2_ref, b2_ref, fr_ref, o_ref):
    hp = lax.Precision.HIGHEST
    fr = fr_ref[...]
    h1 = jnp.sin(fr * (jnp.dot(w1_ref[...], f_ref[...], preferred_element_type=F32, precision=hp) + b1_ref[...]))
    o_ref[...] = jnp.sin(fr * (jnp.dot(w2_ref[...], h1, preferred_element_type=F32, precision=hp) + b2_ref[...]))


def filter_hidden(feats_t, w1, b1, w2, b2, freq):
    H = FILTER_HIDDEN
    P = feats_t.shape[1]
    fe = feats_t.shape[0]
    w1t = jnp.zeros((H, fe), F32).at[:, :FILTER_EMB].set(w1.T)
    return pl.pallas_call(
        _filter_hidden_kernel,
        out_shape=jax.ShapeDtypeStruct((H, P), F32),
        compiler_params=pltpu.CompilerParams(vmem_limit_bytes=VMEM_LIMIT),
        name="filter_hidden",
    )(feats_t, w1t, b1.reshape(H, 1), w2.T, b2.reshape(H, 1), freq.reshape(H, 1))


def _filter_k_kernel(w3_ref, hid_ref, tpos_ref, mask_ref, e0_ref, dl_ref, bias_ref, o_ref):
    h = jnp.dot(w3_ref[0], hid_ref[0], preferred_element_type=F32, precision=lax.Precision.HIGHEST)
    decay = jnp.exp(-tpos_ref[0] * dl_ref[...])
    o_ref[0] = mask_ref[0] * decay * h + bias_ref[0] * e0_ref[...]


def filter_k(w3t, hid, tpos, mask, e0, deltas, bias_aug):
    OD, C, H = w3t.shape
    L = hid.shape[-1]
    rc = 256
    return pl.pallas_call(
        _filter_k_kernel,
        grid=(OD, C // rc),
        in_specs=[
            pl.BlockSpec((1, rc, H), lambda od, j: (od, j, 0)),
            pl.BlockSpec((1, H, L), lambda od, j: (od % 2, 0, 0)),
            pl.BlockSpec((1, 1, L), lambda od, j: (od % 2, 0, 0)),
            pl.BlockSpec((1, 1, L), lambda od, j: (od % 2, 0, 0)),
            pl.BlockSpec((1, L), lambda od, j: (0, 0)),
            pl.BlockSpec((rc, 1), lambda od, j: (j, 0)),
            pl.BlockSpec((1, rc, 1), lambda od, j: (od, j, 0)),
        ],
        out_specs=pl.BlockSpec((1, rc, L), lambda od, j: (od, j, 0)),
        out_shape=jax.ShapeDtypeStruct((OD, C, L), F32),
        compiler_params=_params(("arbitrary", "arbitrary")),
        name="filter_k",
    )(w3t, hid, tpos, mask, e0, deltas, bias_aug)


def hyena_filter_taps(L, C, w1, b1, w2, b2, w3, freq, bias):
    pos = np.arange(L, dtype=np.float64)
    posr = np.where(pos == 0, 0.0, L - pos)

    def feats(p):
        t = p / max(L - 1, 1)
        bands = np.linspace(1e-4, FILTER_BANDS - 1, FILTER_BANDS)
        ang = (2 * math.pi / L) * p[:, None] * bands
        return np.concatenate([t[:, None], np.cos(ang), np.sin(ang)], axis=-1), t

    f0, t0 = feats(pos)
    f1, t1 = feats(posr)
    fe = 40
    ft = np.zeros((fe, 2 * L))
    ft[:FILTER_EMB, :L] = f0.T
    ft[:FILTER_EMB, L:] = f1.T
    hid = filter_hidden(jnp.asarray(ft, F32), w1, b1, w2, b2, freq)
    hid = jnp.stack([hid[:, :L], hid[:, L:]], axis=0)
    tpos = jnp.asarray(np.stack([t0, t1])[:, None, :], F32)
    mask = jnp.asarray(np.stack([np.ones(L), (pos > 0).astype(np.float64)])[:, None, :], F32)
    e0 = jnp.asarray((pos == 0).astype(np.float64)[None, :], F32)
    max_decay = math.log(DECAY_TARGET) / FAST_DECAY_PCT
    min_decay = math.log(DECAY_TARGET) / SLOW_DECAY_PCT
    deltas = jnp.asarray(np.abs(np.linspace(min_decay, max_decay, C))[:, None], F32)
    w3t = w3.T.reshape(HYENA_ORDER * 2, C, FILTER_HIDDEN)
    bias_aug = jnp.stack([bias, jnp.zeros_like(bias)], axis=1).reshape(HYENA_ORDER * 2, C, 1)
    k = filter_k(w3t, hid, tpos, mask, e0, deltas, bias_aug)
    return k.reshape(HYENA_ORDER, 2, C, L)


def _fft_consts():
    def emb(fr, fi):
        return np.block([[fr, fi], [-fi, fr]])

    n2 = LANES
    a = 2 * np.pi * np.outer(np.arange(n2), np.arange(n2)) / n2
    g128f = emb(np.cos(a), -np.sin(a))
    g128i = emb(np.cos(a), np.sin(a)) / (2.0 * FFT_TA * LANES)
    b = 2 * np.pi * np.outer(np.arange(FFT_TA), np.arange(FFT_TA)) / FFT_TA
    eye = np.eye(LANES // FFT_TA)
    g32f = emb(np.kron(eye, np.cos(b)), np.kron(eye, -np.sin(b)))
    g32i = emb(np.kron(eye, np.cos(b)), np.kron(eye, np.sin(b)))
    mats = jnp.asarray(np.stack([g128f, g128i, g32f, g32i]), BF16)
    n = FFT_TA * LANES
    fa = np.tile(np.arange(FFT_TA), LANES // FFT_TA)[:, None]
    tb = np.arange(LANES)[None, :]
    th = 2 * np.pi * fa * tb / n
    ph = 2 * np.pi * (fa * LANES + tb) / (2 * n)
    tw = jnp.asarray(np.stack([np.cos(th), np.sin(th), np.cos(ph), np.sin(ph)]), F32)
    return mats, tw


def _dft_fwd(re, im, g32, g128, twc, tws):
    rb = re.shape[0]
    parts = []
    for g in range(rb // LANES):
        sl = slice(g * LANES, (g + 1) * LANES)
        if im is None:
            parts.append(re[sl, :].T)
        else:
            parts.append(jnp.concatenate([re[sl, :].T, im[sl, :].T], axis=1))
    lhs = jnp.concatenate(parts, axis=0).astype(BF16)
    gm = g32 if im is not None else g32[:LANES, :]
    o1 = jnp.dot(lhs, gm, preferred_element_type=F32)
    parts = []
    for g in range(rb // LANES):
        sl = slice(g * LANES, (g + 1) * LANES)
        r = o1[sl, :LANES].T
        i = o1[sl, LANES:].T
        parts.append(jnp.concatenate([r * twc + i * tws, i * twc - r * tws], axis=1))
    lhs2 = jnp.concatenate(parts, axis=0).astype(BF16)
    o2 = jnp.dot(lhs2, g128, preferred_element_type=F32)
    return o2[:, :LANES], o2[:, LANES:]


def _dft_inv(re, im, g128, g32, twc, tws):
    rb = re.shape[0]
    lhs = jnp.concatenate([re, im], axis=1).astype(BF16)
    o1 = jnp.dot(lhs, g128, preferred_element_type=F32)
    parts = []
    for g in range(rb // LANES):
        sl = slice(g * LANES, (g + 1) * LANES)
        r = o1[sl, :LANES]
        i = o1[sl, LANES:]
        parts.append(jnp.concatenate([(r * twc - i * tws).T, (i * twc + r * tws).T], axis=1))
    lhs2 = jnp.concatenate(parts, axis=0).astype(BF16)
    o2 = jnp.dot(lhs2, g32, preferred_element_type=F32)
    rs, is_ = [], []
    for g in range(rb // LANES):
        sl = slice(g * LANES, (g + 1) * LANES)
        rs.append(o2[sl, :LANES].T)
        is_.append(o2[sl, LANES:].T)
    return jnp.concatenate(rs, axis=0), jnp.concatenate(is_, axis=0)


def _hyena_long_kernel(vr_ref, vi_ref, x1r_ref, x1i_ref, x2r_ref, x2i_ref,
                       k0lo_ref, k0hi_ref, k1lo_ref, k1hi_ref, mats_ref, tw_ref, or_ref, oi_ref):
    g128f, g128i, g32f, g32i = mats_ref[0], mats_ref[1], mats_ref[2], mats_ref[3]
    rb = vr_ref.shape[2]
    rep = rb // LANES

    def tile(t):
        return jnp.concatenate([t] * rep, axis=0)

    twc, tws = tw_ref[0], tw_ref[1]
    pc, ps = tile(tw_ref[2]), tile(tw_ref[3])

    def conv(zr, zi, klo, khi):
        ke = klo + khi
        kd = klo - khi
        ker, kei = _dft_fwd(ke, None, g32f, g128f, twc, tws)
        kor, koi = _dft_fwd(kd * pc, -(kd * ps), g32f, g128f, twc, tws)
        er, ei = _dft_fwd(zr, zi, g32f, g128f, twc, tws)
        o_r, o_i = _dft_fwd(zr * pc + zi * ps, zi * pc - zr * ps, g32f, g128f, twc, tws)
        yer, yei = _dft_inv(er * ker - ei * kei, er * kei + ei * ker, g128i, g32i, twc, tws)
        yor, yoi = _dft_inv(o_r * kor - o_i * koi, o_r * koi + o_i * kor, g128i, g32i, twc, tws)
        return yer + yor * pc - yoi * ps, yei + yoi * pc + yor * ps

    yr, yi = conv(vr_ref[0, 0], vi_ref[0, 0], k0lo_ref[0, 0], k0hi_ref[0, 0])
    z1r = x1r_ref[0, 0] * yr
    z1i = x1i_ref[0, 0] * yi
    yr, yi = conv(z1r, z1i, k1lo_ref[0, 0], k1hi_ref[0, 0])
    or_ref[0] = x2r_ref[0, 0] * yr
    oi_ref[0] = x2i_ref[0, 0] * yi


def hyena_long(ut, kt):
    _, _, R, _ = ut.shape
    rb = 32 * FFT_TA
    mats, tw = _fft_consts()

    def uspec(b, p):
        return pl.BlockSpec((1, 1, rb, LANES), lambda i: (b, p, i, 0))

    def kspec(o, d):
        return pl.BlockSpec((1, 1, rb, LANES), lambda i: (o, d, i, 0))

    outr, outi = pl.pallas_call(
        _hyena_long_kernel,
        grid=(R // rb,),
        in_specs=[uspec(0, 0), uspec(1, 0), uspec(0, 1), uspec(1, 1), uspec(0, 2), uspec(1, 2),
                  kspec(0, 0), kspec(0, 1), kspec(1, 0), kspec(1, 1),
                  pl.BlockSpec((4, 2 * LANES, 2 * LANES), lambda i: (0, 0, 0)),
                  pl.BlockSpec((4, LANES, LANES), lambda i: (0, 0, 0))],
        out_specs=[pl.BlockSpec((1, rb, LANES), lambda i: (0, i, 0)),
                   pl.BlockSpec((1, rb, LANES), lambda i: (0, i, 0))],
        out_shape=[jax.ShapeDtypeStruct((1, R, LANES), F32), jax.ShapeDtypeStruct((1, R, LANES), F32)],
        compiler_params=_params(("arbitrary",)),
        name="hyena_long",
    )(ut, ut, ut, ut, ut, ut, kt, kt, kt, kt, mats, tw)
    return jnp.concatenate([outr, outi], axis=0)


def _hyena_short_kernel(vr_ref, vi_ref, x1r_ref, x1i_ref, x2r_ref, x2i_ref,
                        k0lo_ref, k0hi_ref, k1lo_ref, k1hi_ref, gf_ref, gi_ref, or_ref, oi_ref):
    L = vr_ref.shape[-1]
    n = 2 * L

    def conv(zr, zi, klo, khi):
        kk = jnp.concatenate([klo, khi], axis=1).astype(BF16)
        ks = jnp.dot(kk, gf_ref[:n, :], preferred_element_type=F32)
        kr, ki = ks[:, :n], ks[:, n:]
        zz = jnp.concatenate([zr, zi], axis=1).astype(BF16)
        zs = jnp.dot(zz, gf_ref[n:, :], preferred_element_type=F32)
        sr, si = zs[:, :n], zs[:, n:]
        pr = sr * kr - si * ki
        pi = sr * ki + si * kr
        y = jnp.dot(jnp.concatenate([pr, pi], axis=1).astype(BF16), gi_ref[...], preferred_element_type=F32)
        return y[:, :L], y[:, L:]

    yr, yi = conv(vr_ref[0, 0], vi_ref[0, 0], k0lo_ref[0, 0], k0hi_ref[0, 0])
    z1r = x1r_ref[0, 0] * yr
    z1i = x1i_ref[0, 0] * yi
    yr, yi = conv(z1r, z1i, k1lo_ref[0, 0], k1hi_ref[0, 0])
    or_ref[0] = x2r_ref[0, 0] * yr
    oi_ref[0] = x2i_ref[0, 0] * yi


def hyena_short(ut, kt):
    _, _, C, L = ut.shape
    n = 2 * L
    t = np.arange(n)
    a = 2 * np.pi * np.outer(t, t) / n
    co, si = np.cos(a), np.sin(a)
    gf = np.concatenate([
        np.concatenate([co, -si], axis=1),
        np.concatenate([co[:L], -si[:L]], axis=1),
        np.concatenate([si[:L], co[:L]], axis=1)], axis=0)
    gi = np.concatenate([
        np.concatenate([co[:, :L], si[:, :L]], axis=1),
        np.concatenate([-si[:, :L], co[:, :L]], axis=1)], axis=0) / n
    rc = 256

    def uspec(b, p):
        return pl.BlockSpec((1, 1, rc, L), lambda i: (b, p, i, 0))

    outr, outi = pl.pallas_call(
        _hyena_short_kernel,
        grid=(C // rc,),
        in_specs=[uspec(0, 0), uspec(1, 0), uspec(0, 1), uspec(1, 1), uspec(0, 2), uspec(1, 2),
                  uspec(0, 0), uspec(0, 1), uspec(1, 0), uspec(1, 1),
                  pl.BlockSpec((2 * n, 2 * n), lambda i: (0, 0)),
                  pl.BlockSpec((2 * n, n), lambda i: (0, 0))],
        out_specs=[pl.BlockSpec((1, rc, L), lambda i: (0, i, 0)),
                   pl.BlockSpec((1, rc, L), lambda i: (0, i, 0))],
        out_shape=[jax.ShapeDtypeStruct((1, C, L), F32), jax.ShapeDtypeStruct((1, C, L), F32)],
        compiler_params=_params(("arbitrary",)),
        name="hyena_short",
    )(ut, ut, ut, ut, ut, ut, kt, kt, kt, kt, jnp.asarray(gf, BF16), jnp.asarray(gi, BF16))
    return jnp.concatenate([outr, outi], axis=0)


def hyena_mixer(x, col0, conv_w, conv_b, w1, b1, w2, b2, w3, freq, bias):
    B, L, _ = x.shape
    C = bias.shape[-1]
    assert B == 2
    uc = short_conv(x, col0, 3 * C, conv_w, conv_b)
    ut = uc.reshape(B, L, 3, C).transpose(0, 2, 3, 1)
    kt = hyena_filter_taps(L, C, w1, b1, w2, b2, w3, freq, bias)
    if L == FFT_TA * LANES:
        z = hyena_long(ut.reshape(B, 3, C * FFT_TA, LANES), kt.reshape(HYENA_ORDER, 2, C * FFT_TA, LANES))
        z = z.reshape(B, C, L)
    else:
        z = hyena_short(ut, kt)
    return z.transpose(0, 2, 1)


def _topk_kernel(l_ref, idx_ref, gate_ref):
    l = l_ref[...]
    E = l.shape[1]
    lane = lax.broadcasted_iota(jnp.int32, l.shape, 1).astype(F32)
    vals, idxs = [], []
    for _ in range(TOP_K):
        m = jnp.max(l, axis=-1, keepdims=True)
        am = jnp.min(jnp.where(l == m, lane, float(E)), axis=-1, keepdims=True)
        vals.append(m)
        idxs.append(am)
        l = jnp.where(lane == am, -jnp.inf, l)
    v = jnp.concatenate(vals, axis=1)
    e = jnp.exp(v - vals[0])
    gate_ref[...] = e / jnp.sum(e, axis=-1, keepdims=True)
    idx_ref[...] = jnp.concatenate(idxs, axis=1).astype(jnp.int32)


def topk_gates(logits):
    N, E = logits.shape
    tm = 512
    return pl.pallas_call(
        _topk_kernel,
        grid=(N // tm,),
        in_specs=[pl.BlockSpec((tm, E), lambda i: (i, 0))],
        out_specs=[pl.BlockSpec((tm, TOP_K), lambda i: (i, 0)), pl.BlockSpec((tm, TOP_K), lambda i: (i, 0))],
        out_shape=[jax.ShapeDtypeStruct((N, TOP_K), jnp.int32), jax.ShapeDtypeStruct((N, TOP_K), F32)],
        compiler_params=_params(("arbitrary",)),
        name="topk_gates",
    )(logits)


def _rank_kernel(idx_ref, rank_ref, cnt_ref, carry_ref):
    @pl.when(pl.program_id(0) == 0)
    def _():
        carry_ref[...] = jnp.zeros_like(carry_ref)

    idx = idx_ref[...]
    tm = idx.shape[0]
    E = carry_ref.shape[1]
    e_iota = lax.broadcasted_iota(jnp.int32, (tm, E), 1)
    sel = [idx[:, k:k + 1] == e_iota for k in range(TOP_K)]
    m = jnp.zeros((tm, E), F32)
    for s in sel:
        m = m + jnp.where(s, 1.0, 0.0)
    r_i = lax.broadcasted_iota(jnp.int32, (tm, tm), 0)
    c_i = lax.broadcasted_iota(jnp.int32, (tm, tm), 1)
    tri = jnp.where(r_i > c_i, 1.0, 0.0).astype(BF16)
    before = jnp.dot(tri, m.astype(BF16), preferred_element_type=F32) + carry_ref[...]
    ranks = [jnp.sum(jnp.where(s, before, 0.0), axis=-1, keepdims=True) for s in sel]
    rank_ref[...] = jnp.concatenate(ranks, axis=1).astype(jnp.int32)
    carry_ref[...] = carry_ref[...] + jnp.sum(m, axis=0, keepdims=True)
    cnt_ref[...] = carry_ref[...]


def expert_ranks(top_idx):
    N, _ = top_idx.shape
    tm = 512
    return pl.pallas_call(
        _rank_kernel,
        grid=(N // tm,),
        in_specs=[pl.BlockSpec((tm, TOP_K), lambda i: (i, 0))],
        out_specs=[pl.BlockSpec((tm, TOP_K), lambda i: (i, 0)), pl.BlockSpec((1, N_EXPERTS), lambda i: (0, 0))],
        out_shape=[jax.ShapeDtypeStruct((N, TOP_K), jnp.int32), jax.ShapeDtypeStruct((1, N_EXPERTS), F32)],
        scratch_shapes=[pltpu.VMEM((1, N_EXPERTS), F32)],
        compiler_params=_params(("arbitrary",)),
        name="expert_ranks",
    )(top_idx)


def _issue_row_gather(idx_ref, base, n, srcs, src_rows, dst, sem):
    def body(r, carry):
        tok = idx_ref[base + r]
        if len(srcs) == 1:
            pltpu.make_async_copy(srcs[0].at[pl.ds(tok, 1), :], dst.at[pl.ds(r, 1), :], sem).start()
        else:
            lo = 0
            for s_ref, n_rows in zip(srcs, src_rows):
                @pl.when((tok >= lo) & (tok < lo + n_rows))
                def _(s_ref=s_ref, lo=lo):
                    pltpu.make_async_copy(s_ref.at[pl.ds(tok - lo, 1), :], dst.at[pl.ds(r, 1), :], sem).start()
                lo += n_rows
        return carry

    lax.fori_loop(0, n, body, 0)


def _moe_mm_kernel(rowtok_ref, be_ref, nreal_ref, *refs, n_src, src_rows):
    srcs = refs[:n_src]
    w1_ref, b1_ref, w2_ref, b2_ref, o_ref, xbuf, sems = refs[n_src:]
    i = pl.program_id(0)
    n_real = nreal_ref[0]
    tb = xbuf.shape[1]
    de = w2_ref.shape[0]

    def issue(blk):
        slot = blk % 2
        _issue_row_gather(rowtok_ref, blk * tb, tb, srcs, src_rows, xbuf.at[slot], sems.at[slot])

    @pl.when(i == 0)
    def _():
        issue(i)

    @pl.when(i + 1 < n_real)
    def _():
        issue(i + 1)

    @pl.when(i < n_real)
    def _():
        slot = i % 2
        pltpu.make_async_copy(srcs[0].at[pl.ds(0, tb), :], xbuf.at[slot], sems.at[slot]).wait()
        gu = jnp.dot(xbuf[slot].astype(BF16), w1_ref[...], preferred_element_type=F32) + b1_ref[...]
        g = jnp.minimum(gu[:, :de], SWIGLU_LIMIT)
        up = jnp.clip(gu[:, de:], -SWIGLU_LIMIT, SWIGLU_LIMIT)
        act = (up + 1.0) * g * jax.nn.sigmoid(SWIGLU_ALPHA * g)
        o_ref[...] = jnp.dot(act.astype(BF16), w2_ref[...], preferred_element_type=F32) + b2_ref[...]

    @pl.when(i >= n_real)
    def _():
        o_ref[...] = jnp.zeros_like(o_ref)


def moe_grouped_mm(row_tok, block_e, n_real, n_blocks, srcs, layer, w1, b1, w2, b2):
    D = srcs[0].shape[1]
    _, E, _, de2 = w1.shape
    de = de2 // 2
    tb = MOE_ROWS
    kern = functools.partial(_moe_mm_kernel, n_src=len(srcs), src_rows=tuple(a.shape[0] for a in srcs))
    return pl.pallas_call(
        kern,
        grid_spec=pltpu.PrefetchScalarGridSpec(
            num_scalar_prefetch=3,
            grid=(n_blocks,),
            in_specs=[pl.BlockSpec(memory_space=pl.ANY)] * len(srcs) + [
                pl.BlockSpec((None, None, D, de2), lambda i, rt, be, nr: (layer, be[i], 0, 0)),
                pl.BlockSpec((None, None, 1, de2), lambda i, rt, be, nr: (layer, be[i], 0, 0)),
                pl.BlockSpec((None, None, de, D), lambda i, rt, be, nr: (layer, be[i], 0, 0)),
                pl.BlockSpec((None, None, 1, D), lambda i, rt, be, nr: (layer, be[i], 0, 0)),
            ],
            out_specs=pl.BlockSpec((tb, D), lambda i, rt, be, nr: (i, 0)),
            scratch_shapes=[pltpu.VMEM((2, tb, D), F32), pltpu.SemaphoreType.DMA((2,))],
        ),
        out_shape=jax.ShapeDtypeStruct((n_blocks * tb, D), F32),
        compiler_params=_params(("arbitrary",)),
        name="moe_grouped_mm",
    )(row_tok, block_e, n_real, *srcs, w1, b1.reshape(b1.shape[0], E, 1, de2), w2, b2.reshape(b2.shape[0], E, 1, D))


def _combine_kernel(dest_ref, ys_ref, gates_ref, x_ref, gt_ref, o_ref, buf, sems, *, tok0):
    nt = pl.num_programs(1)
    step = pl.program_id(0) * nt + pl.program_id(1)
    n_steps = pl.num_programs(0) * nt
    tm = buf.shape[2]

    def issue(st):
        slot = st % 2
        base = (tok0 + st * tm) * TOP_K

        def body(r, carry):
            for k in range(TOP_K):
                d = dest_ref[base + r * TOP_K + k]
                pltpu.make_async_copy(ys_ref.at[pl.ds(d, 1), :], buf.at[slot, k, pl.ds(r, 1), :], sems.at[slot]).start()
            return carry

        lax.fori_loop(0, tm, body, 0)

    @pl.when(step == 0)
    def _():
        issue(step)

    @pl.when(step + 1 < n_steps)
    def _():
        issue(step + 1)

    slot = step % 2
    for k in range(TOP_K):
        pltpu.make_async_copy(ys_ref.at[pl.ds(0, tm), :], buf.at[slot, k], sems.at[slot]).wait()
    gates = gates_ref[...]
    acc = gates[:, 0:1] * buf[slot, 0]
    for k in range(1, TOP_K):
        acc = acc + gates[:, k:k + 1] * buf[slot, k]
    o_ref[0] = x_ref[0] + gt_ref[0] * acc


def moe_combine(dest, ys, gates, x, gate_vec, tok0):
    B, T, D = x.shape
    tm = 128
    nt = T // tm
    g0 = tok0 // tm
    gmap = (lambda b, i, d: (b, 0, 0)) if gate_vec.shape[0] > 1 else (lambda b, i, d: (0, 0, 0))
    return pl.pallas_call(
        functools.partial(_combine_kernel, tok0=tok0),
        grid_spec=pltpu.PrefetchScalarGridSpec(
            num_scalar_prefetch=1,
            grid=(B, nt),
            in_specs=[
                pl.BlockSpec(memory_space=pl.ANY),
                pl.BlockSpec((tm, TOP_K), lambda b, i, d: (g0 + b * nt + i, 0)),
                pl.BlockSpec((1, tm, D), lambda b, i, d: (b, i, 0)),
                pl.BlockSpec((1, 1, D), gmap),
            ],
            out_specs=pl.BlockSpec((1, tm, D), lambda b, i, d: (b, i, 0)),
            scratch_shapes=[pltpu.VMEM((2, TOP_K, tm, D), F32), pltpu.SemaphoreType.DMA((2,))],
        ),
        out_shape=jax.ShapeDtypeStruct((B, T, D), F32),
        compiler_params=_params(("arbitrary", "arbitrary")),
        name="moe_combine",
    )(dest, ys, gates, x, gate_vec)


def moe_route(logits):
    N = logits.shape[0]
    tb = MOE_ROWS
    top_idx, gates = topk_gates(logits)
    rank, counts = expert_ranks(top_idx)
    counts = counts.reshape(N_EXPERTS).astype(jnp.int32)
    padded = (counts + tb - 1) // tb * tb
    pad_end = jnp.cumsum(padded)
    pad_start = pad_end - padded
    dest = pad_start[top_idx] + rank
    n_blocks = -(-(N * TOP_K) // tb) + N_EXPERTS
    n_real = (pad_end[-1] // tb).astype(jnp.int32).reshape(1)
    starts = jnp.arange(n_blocks, dtype=jnp.int32) * tb
    block_e = jnp.minimum(jnp.sum(pad_end[None, :] <= starts[:, None], axis=1), N_EXPERTS - 1).astype(jnp.int32)
    tok_of = jnp.arange(N * TOP_K, dtype=jnp.int32) // TOP_K
    row_tok = jnp.zeros((n_blocks * tb,), jnp.int32).at[dest.reshape(-1)].set(tok_of)
    return dest.reshape(-1).astype(jnp.int32), gates, row_tok, block_e, n_real, n_blocks


def moe_ffn(hs, logits, layer, w1, b1, w2, b2):
    dest, gates, row_tok, block_e, n_real, n_blocks = moe_route(logits)
    ys = moe_grouped_mm(row_tok, block_e, n_real, n_blocks, hs, layer, w1, b1, w2, b2)
    return dest, gates, ys


def kernel(x, c, ctx, c_ctx, w_ada, b_ada, g_mix, g_ffn, w_in, q_norm, k_norm, hy_conv_w, hy_conv_b, hy_w1, hy_b1, hy_w2, hy_b2, hy_w3, hy_freq, hy_bias, g_out, w_out, w_router, b_router, moe_w1, moe_b1, moe_w2, moe_b2, g_final):
    B, T, D = x.shape
    C = ctx.shape[1]
    depth = w_ada.shape[0]
    wq = N_HEADS * HEAD_DIM
    wkv = N_KV_HEADS * HEAD_DIM
    col_k, col_v, col_u = wq, wq + wkv, wq + 2 * wkv
    cos_t, sin_t = rope_tables(T)
    ones_c = jnp.ones((C, HEAD_DIM), F32)
    scale = HEAD_DIM ** -0.5

    cvecs = jnp.concatenate([c, c_ctx[None, :], jnp.zeros((8 - B - 1, D), F32)], axis=0)
    ada = adaln_all(cvecs, w_ada, b_ada)

    moe_w1_b = moe_w1.astype(BF16)
    moe_w2_b = moe_w2.astype(BF16)

    xc = ctx
    for i in range(depth):
        last = i == depth - 1
        mod = [ada[i, :B, j * D:(j + 1) * D].reshape(B, 1, D) for j in range(6)]
        cmod = [ada[i, B:B + 1, j * D:(j + 1) * D].reshape(1, 1, D) for j in range(6)]
        hp = (hy_conv_w[i], hy_conv_b[i], hy_w1[i], hy_b1[i], hy_w2[i], hy_b2[i], hy_w3[i], hy_freq[i], hy_bias[i])

        h = norm_mod(x, g_mix[i], mod[0], mod[1])
        hc = norm_mod(xc, g_mix[i], cmod[0], cmod[1])
        qkvu = matmul(h, w_in, i)
        q = head_norm(qkvu, 0, N_HEADS, q_norm[i], cos_t, sin_t, True, scale)
        k = head_norm(qkvu, col_k, N_KV_HEADS, k_norm[i], cos_t, sin_t, True, 1.0)
        v = qkvu[:, :, col_v:col_u].astype(BF16)
        if last:
            kvc = matmul(hc, w_in, i, col_k, col_u - col_k)
            kc = head_norm(kvc, 0, N_KV_HEADS, k_norm[i], ones_c, ones_c, False, 1.0)
            vc = kvc[:, :, wkv:].astype(BF16)
        else:
            qkvuc = matmul(hc, w_in, i)
            qc = head_norm(qkvuc, 0, N_HEADS, q_norm[i], ones_c, ones_c, False, scale)
            kc = head_norm(qkvuc, col_k, N_KV_HEADS, k_norm[i], ones_c, ones_c, False, 1.0)
            vc = qkvuc[:, :, col_v:col_u].astype(BF16)
        y_attn = attention(q, jnp.concatenate([kc, k], axis=1), jnp.concatenate([vc, v], axis=1))
        y_hy = hyena_mixer(qkvu, col_u, *hp)
        x = matmul_residual(mixnorm(y_attn, y_hy, g_out[i]), w_out, i, x, mod[2])
        if not last:
            yc_attn = attention(qc, kc, vc)
            yc_hy = hyena_mixer(qkvuc, col_u, *hp)
            xc = matmul_residual(mixnorm(yc_attn, yc_hy, g_out[i]), w_out, i, xc, cmod[2])

        hf, lg = norm_mod_router(x, g_ffn[i], mod[3], mod[4], w_router[i], b_router[i])
        hs = [hf.reshape(B * T, D)]
        lgs = [lg.reshape(B * T, N_EXPERTS)]
        if not last:
            hfc, lgc = norm_mod_router(xc, g_ffn[i], cmod[3], cmod[4], w_router[i], b_router[i])
            hs.append(hfc.reshape(B * C, D))
            lgs.append(lgc.reshape(B * C, N_EXPERTS))
        logits = jnp.concatenate(lgs, axis=0) if len(lgs) > 1 else lgs[0]
        dest, gates, ys = moe_ffn(hs, logits, i, moe_w1_b, moe_b1, moe_w2_b, moe_b2)
        x = moe_combine(dest, ys, gates, x, mod[5], 0)
        if not last:
            xc = moe_combine(dest, ys, gates, xc, cmod[5], B * T)
    return final_norm(x, g_final)
```

```python
import functools
import math

import numpy as np
import jax
import jax.numpy as jnp
from jax import lax
from jax.experimental import pallas as pl
from jax.experimental.pallas import tpu as pltpu

F32 = jnp.float32
BF16 = jnp.bfloat16

GRID_W = 64
HEAD_DIM = 128
N_HEADS = 16
N_KV_HEADS = 4
GROUP = N_HEADS // N_KV_HEADS
HYENA_ORDER = 2
FILTER_BANDS = 16
FILTER_EMB = 1 + 2 * FILTER_BANDS
FILTER_HIDDEN = 64
DECAY_TARGET = 1e-2
FAST_DECAY_PCT = 0.3
SLOW_DECAY_PCT = 1.5
ROPE_THETA = 10000.0
N_EXPERTS = 32
TOP_K = 4
SWIGLU_LIMIT = 7.0
SWIGLU_ALPHA = 1.702
EPS = 1e-6

LANES = 128
V7X_VMEM_BYTES = 64 * 1024 * 1024
VMEM_LIMIT = V7X_VMEM_BYTES - 8 * 1024 * 1024
FFT_TA = 32
MOE_ROWS = 256


def _params(sem):
    return pltpu.CompilerParams(dimension_semantics=sem, vmem_limit_bytes=VMEM_LIMIT)


def _adaln_kernel(c_ref, w_ref, b_ref, o_ref):
    c = c_ref[...]
    s = (c * jax.nn.sigmoid(c)).astype(BF16)
    o_ref[0] = jnp.dot(s, w_ref[0].astype(BF16), preferred_element_type=F32) + b_ref[0]


def adaln_all(cvecs, w_ada, b_ada):
    L, D, N = w_ada.shape
    tn = 512
    return pl.pallas_call(
        _adaln_kernel,
        grid=(L, N // tn),
        in_specs=[
            pl.BlockSpec((8, D), lambda l, j: (0, 0)),
            pl.BlockSpec((1, D, tn), lambda l, j: (l, 0, j)),
            pl.BlockSpec((1, 1, tn), lambda l, j: (l, 0, j)),
        ],
        out_specs=pl.BlockSpec((1, 8, tn), lambda l, j: (l, 0, j)),
        out_shape=jax.ShapeDtypeStruct((L, 8, N), F32),
        compiler_params=_params(("arbitrary", "arbitrary")),
        name="adaln",
    )(cvecs, w_ada, b_ada.reshape(L, 1, N))


def _norm_mod_kernel(x_ref, g_ref, sh_ref, sc_ref, h_ref):
    x = x_ref[0]
    y = x * lax.rsqrt(jnp.mean(x * x, axis=-1, keepdims=True) + EPS) * g_ref[...]
    h_ref[0] = (y * (1.0 + sc_ref[0]) + sh_ref[0]).astype(h_ref.dtype)


def _norm_mod_router_kernel(*refs, n_first):
    xs = refs[:-7]
    g_ref, sh_ref, sc_ref, wr_ref, br_ref, h_ref, l_ref = refs[-7:]

    def run(x_ref):
        x = x_ref[...]
        y = x * lax.rsqrt(jnp.mean(x * x, axis=-1, keepdims=True) + EPS) * g_ref[...]
        h = y * (1.0 + sc_ref[0]) + sh_ref[0]
        h_ref[...] = h
        l_ref[...] = jnp.dot(h, wr_ref[...], preferred_element_type=F32,
                             precision=lax.Precision.HIGHEST) + br_ref[...]

    if len(xs) == 1:
        run(xs[0])
    else:
        i = pl.program_id(0)
        pl.when(i < n_first)(lambda: run(xs[0]))
        pl.when(i >= n_first)(lambda: run(xs[1]))


def _mod_map(bm):
    return (lambda b, i: (b, 0, 0)) if bm > 1 else (lambda b, i: (0, 0, 0))


def norm_mod(x, g, shift, scale, out_dtype=BF16):
    B, T, D = x.shape
    tm = min(T, 256)
    return pl.pallas_call(
        _norm_mod_kernel,
        grid=(B, T // tm),
        in_specs=[
            pl.BlockSpec((1, tm, D), lambda b, i: (b, i, 0)),
            pl.BlockSpec((1, D), lambda b, i: (0, 0)),
            pl.BlockSpec((1, 1, D), _mod_map(shift.shape[0])),
            pl.BlockSpec((1, 1, D), _mod_map(scale.shape[0])),
        ],
        out_specs=pl.BlockSpec((1, tm, D), lambda b, i: (b, i, 0)),
        out_shape=jax.ShapeDtypeStruct((B, T, D), out_dtype),
        compiler_params=_params(("arbitrary", "arbitrary")),
        name="norm_mod",
    )(x, g.reshape(1, D), shift, scale)


def norm_mod_router(xs, g, shifts, scales, w_router, b_router):
    D = xs[0].shape[-1]
    E = w_router.shape[1]
    tm = 256
    sizes = [a.shape[0] * a.shape[1] for a in xs]
    n_first = sizes[0] // tm
    n_tiles = sum(sizes) // tm
    assert all(a.shape[1] % tm == 0 for a in xs) and len(xs) <= 2

    def mod_rows(ms):
        return jnp.concatenate([jnp.broadcast_to(m, (a.shape[0], 1, D)) for m, a in zip(ms, xs)], axis=0)

    per0 = xs[0].shape[1] // tm

    def tile_row(i):
        if len(xs) == 1:
            return i // per0
        per1 = xs[1].shape[1] // tm
        return jnp.where(i < n_first, i // per0, xs[0].shape[0] + (i - n_first) // per1)

    x_specs = [pl.BlockSpec((tm, D), lambda i: (jnp.minimum(i, n_first - 1), 0))]
    if len(xs) == 2:
        x_specs.append(pl.BlockSpec((tm, D), lambda i: (jnp.maximum(i - n_first, 0), 0)))
    return pl.pallas_call(
        functools.partial(_norm_mod_router_kernel, n_first=n_first),
        grid=(n_tiles,),
        in_specs=x_specs + [
            pl.BlockSpec((1, D), lambda i: (0, 0)),
            pl.BlockSpec((1, 1, D), lambda i: (tile_row(i), 0, 0)),
            pl.BlockSpec((1, 1, D), lambda i: (tile_row(i), 0, 0)),
            pl.BlockSpec((D, E), lambda i: (0, 0)),
            pl.BlockSpec((1, E), lambda i: (0, 0)),
        ],
        out_specs=[pl.BlockSpec((tm, D), lambda i: (i, 0)), pl.BlockSpec((tm, E), lambda i: (i, 0))],
        out_shape=[jax.ShapeDtypeStruct((n_tiles * tm, D), F32), jax.ShapeDtypeStruct((n_tiles * tm, E), F32)],
        compiler_params=_params(("arbitrary",)),
        name="norm_mod_router",
    )(*[a.reshape(-1, D) for a in xs], g.reshape(1, D), mod_rows(shifts), mod_rows(scales),
      w_router, b_router.reshape(1, E))


def _mixnorm_kernel(a_ref, hy_ref, g_ref, o_ref):
    wa = a_ref.shape[-1]
    a = a_ref[0].astype(F32)
    b = hy_ref[0]
    g = g_ref[...]
    o_ref[0, :, :wa] = (a * lax.rsqrt(jnp.mean(a * a, axis=-1, keepdims=True) + EPS) * g[:, :wa]).astype(o_ref.dtype)
    o_ref[0, :, wa:] = (b * lax.rsqrt(jnp.mean(b * b, axis=-1, keepdims=True) + EPS) * g[:, wa:]).astype(o_ref.dtype)


def mixnorm(y_attn, y_hy, g):
    B, T, wa = y_attn.shape
    wh = y_hy.shape[-1]
    tm = min(T, 256)
    return pl.pallas_call(
        _mixnorm_kernel,
        grid=(B, T // tm),
        in_specs=[
            pl.BlockSpec((1, tm, wa), lambda b, i: (b, i, 0)),
            pl.BlockSpec((1, tm, wh), lambda b, i: (b, i, 0)),
            pl.BlockSpec((1, wa + wh), lambda b, i: (0, 0)),
        ],
        out_specs=pl.BlockSpec((1, tm, wa + wh), lambda b, i: (b, i, 0)),
        out_shape=jax.ShapeDtypeStruct((B, T, wa + wh), BF16),
        compiler_params=_params(("arbitrary", "arbitrary")),
        name="mixnorm",
    )(y_attn, y_hy, g.reshape(1, wa + wh))


def _mixnorm_cm_kernel(a_ref, hy_ref, g_ref, o_ref):
    wa = a_ref.shape[-1]
    g = g_ref[...]
    a = a_ref[0].astype(F32)
    o_ref[0, :, :wa] = (a * lax.rsqrt(jnp.mean(a * a, axis=-1, keepdims=True) + EPS) * g[:, :wa]).astype(o_ref.dtype)
    for j in range(hy_ref.shape[2]):
        b = hy_ref[0, :, j, :].T
        y = b * lax.rsqrt(jnp.mean(b * b, axis=-1, keepdims=True) + EPS) * g[:, wa:]
        o_ref[0, j * LANES:(j + 1) * LANES, wa:] = y.astype(o_ref.dtype)


def mixnorm_channel_major(y_attn, y_hy, g):
    B, T, wa = y_attn.shape
    wh, nta = y_hy.shape[1], y_hy.shape[2]
    jb = 8
    tm = jb * LANES
    assert nta * LANES == T and nta % jb == 0
    return pl.pallas_call(
        _mixnorm_cm_kernel,
        grid=(B, T // tm),
        in_specs=[
            pl.BlockSpec((1, tm, wa), lambda b, i: (b, i, 0)),
            pl.BlockSpec((1, wh, jb, LANES), lambda b, i: (b, 0, i, 0)),
            pl.BlockSpec((1, wa + wh), lambda b, i: (0, 0)),
        ],
        out_specs=pl.BlockSpec((1, tm, wa + wh), lambda b, i: (b, i, 0)),
        out_shape=jax.ShapeDtypeStruct((B, T, wa + wh), BF16),
        compiler_params=_params(("arbitrary", "arbitrary")),
        name="mixnorm_cm",
    )(y_attn, y_hy, g.reshape(1, wa + wh))


def _cast_weight_once(w_ref, wb_ref):
    @pl.when((pl.program_id(1) == 0) & (pl.program_id(2) == 0))
    def _():
        wb_ref[...] = w_ref[...].astype(BF16)


def _mm_kernel(x_ref, w_ref, o_ref, wb_ref):
    _cast_weight_once(w_ref, wb_ref)
    o_ref[0] = jnp.dot(x_ref[0], wb_ref[...], preferred_element_type=F32).astype(o_ref.dtype)


def _mm_res_kernel(x_ref, w_ref, r_ref, g_ref, o_ref, wb_ref):
    _cast_weight_once(w_ref, wb_ref)
    acc = jnp.dot(x_ref[0], wb_ref[...], preferred_element_type=F32)
    o_ref[0] = r_ref[0] + g_ref[0] * acc


MM_TN = 512


def matmul(x, w, layer, col0=0, n_out=None, out_dtype=F32):
    B, T, K = x.shape
    N = n_out or w.shape[2]
    tm, tn = min(T, 1024), MM_TN
    c0 = col0 // tn
    assert c0 * tn == col0 and N % tn == 0
    return pl.pallas_call(
        _mm_kernel,
        grid=(N // tn, B, T // tm),
        in_specs=[
            pl.BlockSpec((1, tm, K), lambda j, b, i: (b, i, 0)),
            pl.BlockSpec((None, K, tn), lambda j, b, i: (layer, 0, c0 + j)),
        ],
        out_specs=pl.BlockSpec((1, tm, tn), lambda j, b, i: (b, i, j)),
        out_shape=jax.ShapeDtypeStruct((B, T, N), out_dtype),
        scratch_shapes=[pltpu.VMEM((K, tn), BF16)],
        compiler_params=_params(("arbitrary", "arbitrary", "arbitrary")),
        name="matmul",
    )(x, w)


def matmul_residual(x, w, layer, res, gate):
    B, T, K = x.shape
    N = w.shape[2]
    tm, tn = min(T, 1024), MM_TN
    gmap = (lambda j, b, i: (b, 0, j)) if gate.shape[0] > 1 else (lambda j, b, i: (0, 0, j))
    return pl.pallas_call(
        _mm_res_kernel,
        grid=(N // tn, B, T // tm),
        in_specs=[
            pl.BlockSpec((1, tm, K), lambda j, b, i: (b, i, 0)),
            pl.BlockSpec((None, K, tn), lambda j, b, i: (layer, 0, j)),
            pl.BlockSpec((1, tm, tn), lambda j, b, i: (b, i, j)),
            pl.BlockSpec((1, 1, tn), gmap),
        ],
        out_specs=pl.BlockSpec((1, tm, tn), lambda j, b, i: (b, i, j)),
        out_shape=jax.ShapeDtypeStruct((B, T, N), F32),
        scratch_shapes=[pltpu.VMEM((K, tn), BF16)],
        compiler_params=_params(("arbitrary", "arbitrary", "arbitrary")),
        name="matmul_residual",
    )(x, w, res, gate)


def _head_norm_kernel(x_ref, g_ref, cos_ref, sin_ref, o_ref, *, n_heads, rope, scale):
    g = g_ref[...]
    if rope:
        cs = cos_ref[...]
        sn = sin_ref[...]
        lane = lax.broadcasted_iota(jnp.int32, cs.shape, 1)
        first = (lane % 64) < 32
    for h in range(n_heads):
        xh = x_ref[0, :, h * HEAD_DIM:(h + 1) * HEAD_DIM]
        y = xh * lax.rsqrt(jnp.mean(xh * xh, axis=-1, keepdims=True) + EPS) * g
        if rope:
            swapped = jnp.where(first, pltpu.roll(y, 96, 1), pltpu.roll(y, 32, 1))
            y = y * cs + swapped * sn
        o_ref[0, :, h * HEAD_DIM:(h + 1) * HEAD_DIM] = (y * scale).astype(o_ref.dtype)


def head_norm(x, col0, n_heads, g, cos_t, sin_t, rope, scale):
    B, T, _ = x.shape
    w = n_heads * HEAD_DIM
    tm = min(T, 256)
    cb = col0 // w
    assert cb * w == col0
    kern = functools.partial(_head_norm_kernel, n_heads=n_heads, rope=rope, scale=scale)
    return pl.pallas_call(
        kern,
        grid=(B, T // tm),
        in_specs=[
            pl.BlockSpec((1, tm, w), lambda b, i: (b, i, cb)),
            pl.BlockSpec((1, HEAD_DIM), lambda b, i: (0, 0)),
            pl.BlockSpec((tm, HEAD_DIM), lambda b, i: (i, 0)),
            pl.BlockSpec((tm, HEAD_DIM), lambda b, i: (i, 0)),
        ],
        out_specs=pl.BlockSpec((1, tm, w), lambda b, i: (b, i, 0)),
        out_shape=jax.ShapeDtypeStruct((B, T, w), BF16),
        compiler_params=_params(("arbitrary", "arbitrary")),
        name="head_norm",
    )(x, g.reshape(1, HEAD_DIM), cos_t, sin_t)


def rope_tables(T):
    pos = np.arange(T)
    r, col = pos // GRID_W, pos % GRID_W
    n_freq = HEAD_DIM // 4
    inv = ROPE_THETA ** (-np.arange(n_freq, dtype=np.float64) / n_freq)
    ar, ac = r[:, None] * inv, col[:, None] * inv
    cos_t = np.concatenate([np.cos(ar), np.cos(ar), np.cos(ac), np.cos(ac)], axis=1)
    sin_t = np.concatenate([-np.sin(ar), np.sin(ar), -np.sin(ac), np.sin(ac)], axis=1)
    return jnp.asarray(cos_t, F32), jnp.asarray(sin_t, F32)


ATTN_TK = 512
ATTN_SCALE = HEAD_DIM ** -0.5 * math.log2(math.e)


def _attn_kernel(q_ref, k_ref, v_ref, o_ref):
    tq = q_ref.shape[1]
    S = k_ref.shape[1]
    q = q_ref[0]
    qs = jnp.concatenate([q[:, j * HEAD_DIM:(j + 1) * HEAD_DIM] for j in range(GROUP)], axis=0)
    rows = GROUP * tq
    m = jnp.full((rows, 1), -jnp.inf, F32)
    acc = jnp.zeros((rows, 2 * HEAD_DIM), F32)
    for k0 in range(0, S, ATTN_TK):
        ks = slice(k0, min(k0 + ATTN_TK, S))
        s = lax.dot_general(qs, k_ref[0, ks, :], (((1,), (1,)), ((), ())), preferred_element_type=F32)
        m_new = jnp.maximum(m, jnp.max(s, axis=-1, keepdims=True))
        p = jnp.exp2(s - m_new)
        acc = jnp.exp2(m - m_new) * acc + jnp.dot(p.astype(BF16), v_ref[0, ks, :], preferred_element_type=F32)
        m = m_new
    o = acc[:, :HEAD_DIM] / acc[:, HEAD_DIM:HEAD_DIM + 1]
    for j in range(GROUP):
        o_ref[0, :, j * HEAD_DIM:(j + 1) * HEAD_DIM] = o[j * tq:(j + 1) * tq, :].astype(o_ref.dtype)


def values_with_ones(v):
    B, S, _ = v.shape
    v4 = v.reshape(B, S, N_KV_HEADS, HEAD_DIM).astype(BF16)
    pad = jnp.zeros((B, S, N_KV_HEADS, HEAD_DIM), BF16).at[..., 0].set(1.0)
    return jnp.concatenate([v4, pad], axis=-1).reshape(B, S, N_KV_HEADS * 2 * HEAD_DIM)


def attention(q, k, v1):
    B, T, _ = q.shape
    S = k.shape[1]
    assert S % 128 == 0
    tq = min(T, 128)
    gw = GROUP * HEAD_DIM
    return pl.pallas_call(
        _attn_kernel,
        grid=(B, N_KV_HEADS, T // tq),
        in_specs=[
            pl.BlockSpec((1, tq, gw), lambda b, g, i: (b, i, g)),
            pl.BlockSpec((1, S, HEAD_DIM), lambda b, g, i: (b, 0, g)),
            pl.BlockSpec((1, S, 2 * HEAD_DIM), lambda b, g, i: (b, 0, g)),
        ],
        out_specs=pl.BlockSpec((1, tq, gw), lambda b, g, i: (b, i, g)),
        out_shape=jax.ShapeDtypeStruct((B, T, N_HEADS * HEAD_DIM), BF16),
        compiler_params=_params(("arbitrary", "arbitrary", "arbitrary")),
        name="attention",
    )(q, k, v1)


def _short_conv_kernel(u_ref, w_ref, b_ref, o_ref, *, channel_major):
    u = u_ref[0]
    L = u.shape[0]
    row = lax.broadcasted_iota(jnp.int32, u.shape, 0)
    prev = jnp.where(row == 0, 0.0, pltpu.roll(u, 1, 0))
    nxt = jnp.where(row == L - 1, 0.0, pltpu.roll(u, L - 1, 0))
    w = w_ref[...]
    y = prev * w[0:1, :] + u * w[1:2, :] + nxt * w[2:3, :] + b_ref[...]
    if not channel_major:
        o_ref[0] = y
    else:
        nta = L // LANES
        cb = y.shape[1]
        for ta in range(nta):
            o_ref[0, pl.ds(ta, cb, stride=nta), :] = y[ta * LANES:(ta + 1) * LANES, :].T


def short_conv(x, col0, width, w, b, channel_major):
    B, L, _ = x.shape
    cb = 256 if L > 1024 else 512
    assert col0 % cb == 0 and width % cb == 0
    off = col0 // cb
    if channel_major:
        nta = L // LANES
        out_spec = pl.BlockSpec((1, cb * nta, LANES), lambda bi, j: (bi, j, 0))
        out_shape = jax.ShapeDtypeStruct((B, width * nta, LANES), F32)
    else:
        out_spec = pl.BlockSpec((1, L, cb), lambda bi, j: (bi, 0, j))
        out_shape = jax.ShapeDtypeStruct((B, L, width), F32)
    return pl.pallas_call(
        functools.partial(_short_conv_kernel, channel_major=channel_major),
        grid=(B, width // cb),
        in_specs=[
            pl.BlockSpec((1, L, cb), lambda bi, j: (bi, 0, off + j)),
            pl.BlockSpec((3, cb), lambda bi, j: (0, j)),
            pl.BlockSpec((1, cb), lambda bi, j: (0, j)),
        ],
        out_specs=out_spec,
        out_shape=out_shape,
        compiler_params=_params(("arbitrary", "arbitrary")),
        name="short_conv",
    )(x, w, b.reshape(1, width))


def _filter_hidden_kernel(f_ref, w1_ref, b1_ref, w2_ref, b2_ref, fr_ref, o_ref):
    hp = lax.Precision.HIGHEST
    fr = fr_ref[...]
    h1 = jnp.sin(fr * (jnp.dot(w1_ref[...], f_ref[...], preferred_element_type=F32, precision=hp) + b1_ref[...]))
    o_ref[...] = jnp.sin(fr * (jnp.dot(w2_ref[...], h1, preferred_element_type=F32, precision=hp) + b2_ref[...]))


def filter_hidden(feats_t, w1, b1, w2, b2, freq):
    H = FILTER_HIDDEN
    P = feats_t.shape[1]
    fe = feats_t.shape[0]
    w1t = jnp.zeros((H, fe), F32).at[:, :FILTER_EMB].set(w1.T)
    return pl.pallas_call(
        _filter_hidden_kernel,
        out_shape=jax.ShapeDtypeStruct((H, P), F32),
        compiler_params=pltpu.CompilerParams(vmem_limit_bytes=VMEM_LIMIT),
        name="filter_hidden",
    )(feats_t, w1t, b1.reshape(H, 1), w2.T, b2.reshape(H, 1), freq.reshape(H, 1))


def _filter_k_kernel(w3_ref, hid_ref, tpos_ref, mask_ref, e0_ref, dl_ref, bias_ref, o_ref, *, channel_major):
    h = jnp.dot(w3_ref[0], hid_ref[0], preferred_element_type=F32, precision=lax.Precision.HIGHEST)
    decay = jnp.exp(-tpos_ref[0] * dl_ref[...])
    k = mask_ref[0] * decay * h + bias_ref[0] * e0_ref[...]
    if not channel_major:
        o_ref[0] = k
    else:
        rc, L = k.shape
        nta = L // LANES
        for ta in range(nta):
            o_ref[0, pl.ds(ta, rc, stride=nta), :] = k[:, ta * LANES:(ta + 1) * LANES]


def filter_k(w3t, hid, tpos, mask, e0, deltas, bias_aug, channel_major):
    OD, C, H = w3t.shape
    L = hid.shape[-1]
    rc = 256
    if channel_major:
        nta = L // LANES
        out_spec = pl.BlockSpec((1, rc * nta, LANES), lambda od, j: (od, j, 0))
        out_shape = jax.ShapeDtypeStruct((OD, C * nta, LANES), F32)
    else:
        out_spec = pl.BlockSpec((1, rc, L), lambda od, j: (od, j, 0))
        out_shape = jax.ShapeDtypeStruct((OD, C, L), F32)
    return pl.pallas_call(
        functools.partial(_filter_k_kernel, channel_major=channel_major),
        grid=(OD, C // rc),
        in_specs=[
            pl.BlockSpec((1, rc, H), lambda od, j: (od, j, 0)),
            pl.BlockSpec((1, H, L), lambda od, j: (od % 2, 0, 0)),
            pl.BlockSpec((1, 1, L), lambda od, j: (od % 2, 0, 0)),
            pl.BlockSpec((1, 1, L), lambda od, j: (od % 2, 0, 0)),
            pl.BlockSpec((1, L), lambda od, j: (0, 0)),
            pl.BlockSpec((rc, 1), lambda od, j: (j, 0)),
            pl.BlockSpec((1, rc, 1), lambda od, j: (od, j, 0)),
        ],
        out_specs=out_spec,
        out_shape=out_shape,
        compiler_params=_params(("arbitrary", "arbitrary")),
        name="filter_k",
    )(w3t, hid, tpos, mask, e0, deltas, bias_aug)


def hyena_filter_taps(L, C, w1, b1, w2, b2, w3, freq, bias, channel_major):
    pos = np.arange(L, dtype=np.float64)
    posr = np.where(pos == 0, 0.0, L - pos)

    def feats(p):
        t = p / max(L - 1, 1)
        bands = np.linspace(1e-4, FILTER_BANDS - 1, FILTER_BANDS)
        ang = (2 * math.pi / L) * p[:, None] * bands
        return np.concatenate([t[:, None], np.cos(ang), np.sin(ang)], axis=-1), t

    f0, t0 = feats(pos)
    f1, t1 = feats(posr)
    fe = 40
    ft = np.zeros((fe, 2 * L))
    ft[:FILTER_EMB, :L] = f0.T
    ft[:FILTER_EMB, L:] = f1.T
    hid = filter_hidden(jnp.asarray(ft, F32), w1, b1, w2, b2, freq)
    hid = jnp.stack([hid[:, :L], hid[:, L:]], axis=0)
    tpos = jnp.asarray(np.stack([t0, t1])[:, None, :], F32)
    mask = jnp.asarray(np.stack([np.ones(L), (pos > 0).astype(np.float64)])[:, None, :], F32)
    e0 = jnp.asarray((pos == 0).astype(np.float64)[None, :], F32)
    max_decay = math.log(DECAY_TARGET) / FAST_DECAY_PCT
    min_decay = math.log(DECAY_TARGET) / SLOW_DECAY_PCT
    deltas = jnp.asarray(np.abs(np.linspace(min_decay, max_decay, C))[:, None], F32)
    w3t = w3.T.reshape(HYENA_ORDER * 2, C, FILTER_HIDDEN)
    bias_aug = jnp.stack([bias, jnp.zeros_like(bias)], axis=1).reshape(HYENA_ORDER * 2, C, 1)
    k = filter_k(w3t, hid, tpos, mask, e0, deltas, bias_aug, channel_major)
    return k.reshape((HYENA_ORDER, 2) + k.shape[1:])


def _fft_consts():
    def emb(fr, fi):
        return np.block([[fr, fi], [-fi, fr]])

    n2 = LANES
    a = 2 * np.pi * np.outer(np.arange(n2), np.arange(n2)) / n2
    g128f = emb(np.cos(a), -np.sin(a))
    g128i = emb(np.cos(a), np.sin(a)) / (2.0 * FFT_TA * LANES)
    b = 2 * np.pi * np.outer(np.arange(FFT_TA), np.arange(FFT_TA)) / FFT_TA
    eye = np.eye(LANES // FFT_TA)
    g32f = emb(np.kron(eye, np.cos(b)), np.kron(eye, -np.sin(b)))
    g32i = emb(np.kron(eye, np.cos(b)), np.kron(eye, np.sin(b)))
    mats = jnp.asarray(np.stack([g128f, g128i, g32f, g32i]), BF16)
    n = FFT_TA * LANES
    fa = np.tile(np.arange(FFT_TA), LANES // FFT_TA)[:, None]
    tb = np.arange(LANES)[None, :]
    th = 2 * np.pi * fa * tb / n
    ph = 2 * np.pi * (fa * LANES + tb) / (2 * n)
    tw = jnp.asarray(np.stack([np.cos(th), np.sin(th), np.cos(ph), np.sin(ph)]), F32)
    return mats, tw


def _dft_fwd(re, im, g32, g128, twc, tws):
    rb = re.shape[0]
    parts = []
    for g in range(rb // LANES):
        sl = slice(g * LANES, (g + 1) * LANES)
        if im is None:
            parts.append(re[sl, :].T)
        else:
            parts.append(jnp.concatenate([re[sl, :].T, im[sl, :].T], axis=1))
    lhs = jnp.concatenate(parts, axis=0).astype(BF16)
    gm = g32 if im is not None else g32[:LANES, :]
    o1 = jnp.dot(lhs, gm, preferred_element_type=F32)
    parts = []
    for g in range(rb // LANES):
        sl = slice(g * LANES, (g + 1) * LANES)
        r = o1[sl, :LANES].T
        i = o1[sl, LANES:].T
        parts.append(jnp.concatenate([r * twc + i * tws, i * twc - r * tws], axis=1))
    lhs2 = jnp.concatenate(parts, axis=0).astype(BF16)
    o2 = jnp.dot(lhs2, g128, preferred_element_type=F32)
    return o2[:, :LANES], o2[:, LANES:]


def _dft_inv(re, im, g128, g32, twc, tws):
    rb = re.shape[0]
    lhs = jnp.concatenate([re, im], axis=1).astype(BF16)
    o1 = jnp.dot(lhs, g128, preferred_element_type=F32)
    parts = []
    for g in range(rb // LANES):
        sl = slice(g * LANES, (g + 1) * LANES)
        r = o1[sl, :LANES]
        i = o1[sl, LANES:]
        parts.append(jnp.concatenate([(r * twc - i * tws).T, (i * twc + r * tws).T], axis=1))
    lhs2 = jnp.concatenate(parts, axis=0).astype(BF16)
    o2 = jnp.dot(lhs2, g32, preferred_element_type=F32)
    rs, is_ = [], []
    for g in range(rb // LANES):
        sl = slice(g * LANES, (g + 1) * LANES)
        rs.append(o2[sl, :LANES].T)
        is_.append(o2[sl, LANES:].T)
    return jnp.concatenate(rs, axis=0), jnp.concatenate(is_, axis=0)


def _hyena_long_kernel(vr_ref, vi_ref, x1r_ref, x1i_ref, x2r_ref, x2i_ref,
                       k0lo_ref, k0hi_ref, k1lo_ref, k1hi_ref, mats_ref, tw_ref, o_ref):
    g128f, g128i, g32f, g32i = mats_ref[0], mats_ref[1], mats_ref[2], mats_ref[3]
    rb = vr_ref.shape[2]
    rep = rb // LANES

    def tile(t):
        return jnp.concatenate([t] * rep, axis=0)

    twc, tws = tw_ref[0], tw_ref[1]
    pc, ps = tile(tw_ref[2]), tile(tw_ref[3])

    def conv(zr, zi, klo, khi):
        ke = klo + khi
        kd = klo - khi
        ker, kei = _dft_fwd(ke, None, g32f, g128f, twc, tws)
        kor, koi = _dft_fwd(kd * pc, -(kd * ps), g32f, g128f, twc, tws)
        er, ei = _dft_fwd(zr, zi, g32f, g128f, twc, tws)
        o_r, o_i = _dft_fwd(zr * pc + zi * ps, zi * pc - zr * ps, g32f, g128f, twc, tws)
        yer, yei = _dft_inv(er * ker - ei * kei, er * kei + ei * ker, g128i, g32i, twc, tws)
        yor, yoi = _dft_inv(o_r * kor - o_i * koi, o_r * koi + o_i * kor, g128i, g32i, twc, tws)
        return yer + yor * pc - yoi * ps, yei + yoi * pc + yor * ps

    yr, yi = conv(vr_ref[0, 0], vi_ref[0, 0], k0lo_ref[0, 0], k0hi_ref[0, 0])
    z1r = x1r_ref[0, 0] * yr
    z1i = x1i_ref[0, 0] * yi
    yr, yi = conv(z1r, z1i, k1lo_ref[0, 0], k1hi_ref[0, 0])
    o_ref[0] = x2r_ref[0, 0] * yr
    o_ref[1] = x2i_ref[0, 0] * yi


def hyena_long(ut, kt):
    _, _, R, _ = ut.shape
    rb = 32 * FFT_TA
    mats, tw = _fft_consts()

    def uspec(b, p):
        return pl.BlockSpec((1, 1, rb, LANES), lambda i: (b, p, i, 0))

    def kspec(o, d):
        return pl.BlockSpec((1, 1, rb, LANES), lambda i: (o, d, i, 0))

    return pl.pallas_call(
        _hyena_long_kernel,
        grid=(R // rb,),
        in_specs=[uspec(0, 0), uspec(1, 0), uspec(0, 1), uspec(1, 1), uspec(0, 2), uspec(1, 2),
                  kspec(0, 0), kspec(0, 1), kspec(1, 0), kspec(1, 1),
                  pl.BlockSpec((4, 2 * LANES, 2 * LANES), lambda i: (0, 0, 0)),
                  pl.BlockSpec((4, LANES, LANES), lambda i: (0, 0, 0))],
        out_specs=pl.BlockSpec((2, rb, LANES), lambda i: (0, i, 0)),
        out_shape=jax.ShapeDtypeStruct((2, R, LANES), F32),
        compiler_params=_params(("arbitrary",)),
        name="hyena_long",
    )(ut, ut, ut, ut, ut, ut, kt, kt, kt, kt, mats, tw)


def _hyena_short_kernel(vr_ref, vi_ref, x1r_ref, x1i_ref, x2r_ref, x2i_ref,
                        k0lo_ref, k0hi_ref, k1lo_ref, k1hi_ref, gf_ref, gi_ref, or_ref, oi_ref):
    L = vr_ref.shape[-1]
    n = 2 * L

    def conv(zr, zi, klo, khi):
        kk = jnp.concatenate([klo, khi], axis=1).astype(BF16)
        ks = jnp.dot(kk, gf_ref[:n, :], preferred_element_type=F32)
        kr, ki = ks[:, :n], ks[:, n:]
        zz = jnp.concatenate([zr, zi], axis=1).astype(BF16)
        zs = jnp.dot(zz, gf_ref[n:, :], preferred_element_type=F32)
        sr, si = zs[:, :n], zs[:, n:]
        pr = sr * kr - si * ki
        pi = sr * ki + si * kr
        y = jnp.dot(jnp.concatenate([pr, pi], axis=1).astype(BF16), gi_ref[...], preferred_element_type=F32)
        return y[:, :L], y[:, L:]

    yr, yi = conv(vr_ref[0, 0], vi_ref[0, 0], k0lo_ref[0, 0], k0hi_ref[0, 0])
    z1r = x1r_ref[0, 0] * yr
    z1i = x1i_ref[0, 0] * yi
    yr, yi = conv(z1r, z1i, k1lo_ref[0, 0], k1hi_ref[0, 0])
    or_ref[0] = x2r_ref[0, 0] * yr
    oi_ref[0] = x2i_ref[0, 0] * yi


def hyena_short(ut, kt):
    _, _, C, L = ut.shape
    n = 2 * L
    t = np.arange(n)
    a = 2 * np.pi * np.outer(t, t) / n
    co, si = np.cos(a), np.sin(a)
    gf = np.concatenate([
        np.concatenate([co, -si], axis=1),
        np.concatenate([co[:L], -si[:L]], axis=1),
        np.concatenate([si[:L], co[:L]], axis=1)], axis=0)
    gi = np.concatenate([
        np.concatenate([co[:, :L], si[:, :L]], axis=1),
        np.concatenate([-si[:, :L], co[:, :L]], axis=1)], axis=0) / n
    rc = 256

    def uspec(b, p):
        return pl.BlockSpec((1, 1, rc, L), lambda i: (b, p, i, 0))

    outr, outi = pl.pallas_call(
        _hyena_short_kernel,
        grid=(C // rc,),
        in_specs=[uspec(0, 0), uspec(1, 0), uspec(0, 1), uspec(1, 1), uspec(0, 2), uspec(1, 2),
                  uspec(0, 0), uspec(0, 1), uspec(1, 0), uspec(1, 1),
                  pl.BlockSpec((2 * n, 2 * n), lambda i: (0, 0)),
                  pl.BlockSpec((2 * n, n), lambda i: (0, 0))],
        out_specs=[pl.BlockSpec((1, rc, L), lambda i: (0, i, 0)),
                   pl.BlockSpec((1, rc, L), lambda i: (0, i, 0))],
        out_shape=[jax.ShapeDtypeStruct((1, C, L), F32), jax.ShapeDtypeStruct((1, C, L), F32)],
        compiler_params=_params(("arbitrary",)),
        name="hyena_short",
    )(ut, ut, ut, ut, ut, ut, kt, kt, kt, kt, jnp.asarray(gf, BF16), jnp.asarray(gi, BF16))
    return jnp.concatenate([outr, outi], axis=0)


def hyena_mixer(x, col0, conv_w, conv_b, w1, b1, w2, b2, w3, freq, bias):
    B, L, _ = x.shape
    C = bias.shape[-1]
    assert B == 2
    if L == FFT_TA * LANES:
        ut = short_conv(x, col0, 3 * C, conv_w, conv_b, True).reshape(B, 3, C * FFT_TA, LANES)
        kt = hyena_filter_taps(L, C, w1, b1, w2, b2, w3, freq, bias, True)
        return hyena_long(ut, kt).reshape(B, C, FFT_TA, LANES)
    uc = short_conv(x, col0, 3 * C, conv_w, conv_b, False)
    ut = uc.reshape(B, L, 3, C).transpose(0, 2, 3, 1)
    kt = hyena_filter_taps(L, C, w1, b1, w2, b2, w3, freq, bias, False)
    return hyena_short(ut, kt).transpose(0, 2, 1)


def _topk_kernel(l_ref, idx_ref, gate_ref):
    l = l_ref[...]
    E = l.shape[1]
    lane = lax.broadcasted_iota(jnp.int32, l.shape, 1).astype(F32)
    vals, idxs = [], []
    for _ in range(TOP_K):
        m = jnp.max(l, axis=-1, keepdims=True)
        am = jnp.min(jnp.where(l == m, lane, float(E)), axis=-1, keepdims=True)
        vals.append(m)
        idxs.append(am)
        l = jnp.where(lane == am, -jnp.inf, l)
    v = jnp.concatenate(vals, axis=1)
    e = jnp.exp(v - vals[0])
    gate_ref[...] = e / jnp.sum(e, axis=-1, keepdims=True)
    idx_ref[...] = jnp.concatenate(idxs, axis=1).astype(jnp.int32)


def topk_gates(logits):
    N, E = logits.shape
    tm = 512
    return pl.pallas_call(
        _topk_kernel,
        grid=(N // tm,),
        in_specs=[pl.BlockSpec((tm, E), lambda i: (i, 0))],
        out_specs=[pl.BlockSpec((tm, TOP_K), lambda i: (i, 0)), pl.BlockSpec((tm, TOP_K), lambda i: (i, 0))],
        out_shape=[jax.ShapeDtypeStruct((N, TOP_K), jnp.int32), jax.ShapeDtypeStruct((N, TOP_K), F32)],
        compiler_params=_params(("arbitrary",)),
        name="topk_gates",
    )(logits)


def _rank_kernel(idx_ref, rank_ref, cnt_ref, carry_ref):
    @pl.when(pl.program_id(0) == 0)
    def _():
        carry_ref[...] = jnp.zeros_like(carry_ref)

    idx = idx_ref[...]
    tm = idx.shape[0]
    E = carry_ref.shape[1]
    e_iota = lax.broadcasted_iota(jnp.int32, (tm, E), 1)
    sel = [idx[:, k:k + 1] == e_iota for k in range(TOP_K)]
    m = jnp.zeros((tm, E), F32)
    for s in sel:
        m = m + jnp.where(s, 1.0, 0.0)
    r_i = lax.broadcasted_iota(jnp.int32, (tm, tm), 0)
    c_i = lax.broadcasted_iota(jnp.int32, (tm, tm), 1)
    tri = jnp.where(r_i > c_i, 1.0, 0.0).astype(BF16)
    before = jnp.dot(tri, m.astype(BF16), preferred_element_type=F32) + carry_ref[...]
    ranks = [jnp.sum(jnp.where(s, before, 0.0), axis=-1, keepdims=True) for s in sel]
    rank_ref[...] = jnp.concatenate(ranks, axis=1).astype(jnp.int32)
    carry_ref[...] = carry_ref[...] + jnp.sum(m, axis=0, keepdims=True)
    cnt_ref[...] = carry_ref[...]


def expert_ranks(top_idx):
    N, _ = top_idx.shape
    tm = 512
    return pl.pallas_call(
        _rank_kernel,
        grid=(N // tm,),
        in_specs=[pl.BlockSpec((tm, TOP_K), lambda i: (i, 0))],
        out_specs=[pl.BlockSpec((tm, TOP_K), lambda i: (i, 0)), pl.BlockSpec((1, N_EXPERTS), lambda i: (0, 0))],
        out_shape=[jax.ShapeDtypeStruct((N, TOP_K), jnp.int32), jax.ShapeDtypeStruct((1, N_EXPERTS), F32)],
        scratch_shapes=[pltpu.VMEM((1, N_EXPERTS), F32)],
        compiler_params=_params(("arbitrary",)),
        name="expert_ranks",
    )(top_idx)


def _gather_params(sem):
    return pltpu.CompilerParams(dimension_semantics=sem, vmem_limit_bytes=VMEM_LIMIT, disable_bounds_checks=True)


def _moe_mm_kernel(rowtok_ref, be_ref, nreal_ref, h_ref, w1_ref, b1_ref, w2_ref, b2_ref, o_ref, xbuf, sems):
    i = pl.program_id(0)
    n_real = nreal_ref[0]
    tb = xbuf.shape[1]
    de = w2_ref.shape[0]

    def issue(blk, slot):
        for r in range(tb):
            tok = rowtok_ref[blk * tb + r]
            pltpu.make_async_copy(h_ref.at[pl.ds(tok, 1), :], xbuf.at[slot, pl.ds(r, 1), :], sems.at[slot]).start()

    def wait(slot):
        pltpu.make_async_copy(h_ref.at[pl.ds(0, tb), :], xbuf.at[slot], sems.at[slot]).wait()

    @pl.when(i == 0)
    def _():
        issue(i, 0)

    @pl.when(i < n_real)
    def _():
        slot = i % 2
        wait(slot)
        issue(jnp.minimum(i + 1, n_real - 1), 1 - slot)
        gu = jnp.dot(xbuf[slot].astype(BF16), w1_ref[...], preferred_element_type=F32) + b1_ref[...]
        g = jnp.minimum(gu[:, :de], SWIGLU_LIMIT)
        up = jnp.clip(gu[:, de:], -SWIGLU_LIMIT, SWIGLU_LIMIT)
        act = (up + 1.0) * g * jax.nn.sigmoid(SWIGLU_ALPHA * g)
        o_ref[...] = jnp.dot(act.astype(BF16), w2_ref[...], preferred_element_type=F32) + b2_ref[...]

    @pl.when(i == n_real - 1)
    def _():
        wait(1 - i % 2)

    @pl.when(i >= n_real)
    def _():
        o_ref[...] = jnp.zeros_like(o_ref)


def moe_grouped_mm(row_tok, block_e, n_real, n_blocks, h, layer, w1, b1, w2, b2):
    D = h.shape[1]
    _, E, _, de2 = w1.shape
    de = de2 // 2
    tb = MOE_ROWS
    return pl.pallas_call(
        _moe_mm_kernel,
        grid_spec=pltpu.PrefetchScalarGridSpec(
            num_scalar_prefetch=3,
            grid=(n_blocks,),
            in_specs=[
                pl.BlockSpec(memory_space=pl.ANY),
                pl.BlockSpec((None, None, D, de2), lambda i, rt, be, nr: (layer, be[i], 0, 0)),
                pl.BlockSpec((None, None, 1, de2), lambda i, rt, be, nr: (layer, be[i], 0, 0)),
                pl.BlockSpec((None, None, de, D), lambda i, rt, be, nr: (layer, be[i], 0, 0)),
                pl.BlockSpec((None, None, 1, D), lambda i, rt, be, nr: (layer, be[i], 0, 0)),
            ],
            out_specs=pl.BlockSpec((tb, D), lambda i, rt, be, nr: (i, 0)),
            scratch_shapes=[pltpu.VMEM((2, tb, D), F32), pltpu.SemaphoreType.DMA((2,))],
        ),
        out_shape=jax.ShapeDtypeStruct((n_blocks * tb, D), F32),
        compiler_params=_gather_params(("arbitrary",)),
        name="moe_grouped_mm",
    )(row_tok, block_e, n_real, h, w1, b1.reshape(b1.shape[0], E, 1, de2), w2, b2.reshape(b2.shape[0], E, 1, D))


def _combine_kernel(dest_ref, ys_ref, gates_ref, x_ref, gt_ref, *refs, tok0, final_norm):
    fg_ref = refs[0] if final_norm else None
    o_ref, buf, sems = refs[-3:]
    nt = pl.num_programs(1)
    step = pl.program_id(0) * nt + pl.program_id(1)
    n_steps = pl.num_programs(0) * nt
    tm = buf.shape[2]

    def issue(st, slot):
        base = (tok0 + st * tm) * TOP_K
        for r in range(tm):
            for k in range(TOP_K):
                d = dest_ref[base + r * TOP_K + k]
                pltpu.make_async_copy(ys_ref.at[pl.ds(d, 1), :], buf.at[slot, k, pl.ds(r, 1), :], sems.at[slot]).start()

    def wait(slot):
        for k in range(TOP_K):
            pltpu.make_async_copy(ys_ref.at[pl.ds(0, tm), :], buf.at[slot, k], sems.at[slot]).wait()

    @pl.when(step == 0)
    def _():
        issue(step, 0)

    slot = step % 2
    wait(slot)
    issue(jnp.minimum(step + 1, n_steps - 1), 1 - slot)
    gates = gates_ref[...]
    acc = gates[:, 0:1] * buf[slot, 0]
    for k in range(1, TOP_K):
        acc = acc + gates[:, k:k + 1] * buf[slot, k]
    y = x_ref[0] + gt_ref[0] * acc
    if final_norm:
        y = y * lax.rsqrt(jnp.mean(y * y, axis=-1, keepdims=True) + EPS) * fg_ref[...]
    o_ref[0] = y

    @pl.when(step == n_steps - 1)
    def _():
        wait(1 - slot)


def moe_combine(dest, ys, gates, x, gate_vec, tok0, final_g=None):
    B, T, D = x.shape
    tm = 128
    nt = T // tm
    g0 = tok0 // tm
    gmap = (lambda b, i, d: (b, 0, 0)) if gate_vec.shape[0] > 1 else (lambda b, i, d: (0, 0, 0))
    in_specs = [
        pl.BlockSpec(memory_space=pl.ANY),
        pl.BlockSpec((tm, TOP_K), lambda b, i, d: (g0 + b * nt + i, 0)),
        pl.BlockSpec((1, tm, D), lambda b, i, d: (b, i, 0)),
        pl.BlockSpec((1, 1, D), gmap),
    ]
    args = [dest, ys, gates, x, gate_vec]
    if final_g is not None:
        in_specs.append(pl.BlockSpec((1, D), lambda b, i, d: (0, 0)))
        args.append(final_g.reshape(1, D))
    return pl.pallas_call(
        functools.partial(_combine_kernel, tok0=tok0, final_norm=final_g is not None),
        grid_spec=pltpu.PrefetchScalarGridSpec(
            num_scalar_prefetch=1,
            grid=(B, nt),
            in_specs=in_specs,
            out_specs=pl.BlockSpec((1, tm, D), lambda b, i, d: (b, i, 0)),
            scratch_shapes=[pltpu.VMEM((2, TOP_K, tm, D), F32), pltpu.SemaphoreType.DMA((2,))],
        ),
        out_shape=jax.ShapeDtypeStruct((B, T, D), F32),
        compiler_params=_gather_params(("arbitrary", "arbitrary")),
        name="moe_combine",
    )(*args)


def moe_route(logits):
    N = logits.shape[0]
    tb = MOE_ROWS
    top_idx, gates = topk_gates(logits)
    rank, counts = expert_ranks(top_idx)
    counts = counts.reshape(N_EXPERTS).astype(jnp.int32)
    padded = (counts + tb - 1) // tb * tb
    pad_end = jnp.cumsum(padded)
    pad_start = pad_end - padded
    dest = pad_start[top_idx] + rank
    n_blocks = -(-(N * TOP_K) // tb) + N_EXPERTS
    n_real = (pad_end[-1] // tb).astype(jnp.int32).reshape(1)
    starts = jnp.arange(n_blocks, dtype=jnp.int32) * tb
    block_e = jnp.minimum(jnp.sum(pad_end[None, :] <= starts[:, None], axis=1), N_EXPERTS - 1).astype(jnp.int32)
    tok_of = jnp.arange(N * TOP_K, dtype=jnp.int32) // TOP_K
    row_tok = jnp.zeros((n_blocks * tb,), jnp.int32).at[dest.reshape(-1)].set(tok_of)
    return dest.reshape(-1).astype(jnp.int32), gates, row_tok, block_e, n_real, n_blocks


def moe_ffn(h, logits, layer, w1, b1, w2, b2):
    dest, gates, row_tok, block_e, n_real, n_blocks = moe_route(logits)
    ys = moe_grouped_mm(row_tok, block_e, n_real, n_blocks, h, layer, w1, b1, w2, b2)
    return dest, gates, ys


def kernel(x, c, ctx, c_ctx, w_ada, b_ada, g_mix, g_ffn, w_in, q_norm, k_norm, hy_conv_w, hy_conv_b, hy_w1, hy_b1, hy_w2, hy_b2, hy_w3, hy_freq, hy_bias, g_out, w_out, w_router, b_router, moe_w1, moe_b1, moe_w2, moe_b2, g_final):
    B, T, D = x.shape
    C = ctx.shape[1]
    depth = w_ada.shape[0]
    wq = N_HEADS * HEAD_DIM
    wkv = N_KV_HEADS * HEAD_DIM
    col_k, col_v, col_u = wq, wq + wkv, wq + 2 * wkv
    cos_t, sin_t = rope_tables(T)
    ones_c = jnp.ones((C, HEAD_DIM), F32)

    cvecs = jnp.concatenate([c, c_ctx[None, :], jnp.zeros((8 - B - 1, D), F32)], axis=0)
    ada = adaln_all(cvecs, w_ada, b_ada)

    moe_w1_b = moe_w1.astype(BF16)
    moe_w2_b = moe_w2.astype(BF16)

    xc = ctx
    for i in range(depth):
        last = i == depth - 1
        mod = [ada[i, :B, j * D:(j + 1) * D].reshape(B, 1, D) for j in range(6)]
        cmod = [ada[i, B:B + 1, j * D:(j + 1) * D].reshape(1, 1, D) for j in range(6)]
        hp = (hy_conv_w[i], hy_conv_b[i], hy_w1[i], hy_b1[i], hy_w2[i], hy_b2[i], hy_w3[i], hy_freq[i], hy_bias[i])

        h = norm_mod(x, g_mix[i], mod[0], mod[1])
        hc = norm_mod(xc, g_mix[i], cmod[0], cmod[1])
        qkvu = matmul(h, w_in, i)
        q = head_norm(qkvu, 0, N_HEADS, q_norm[i], cos_t, sin_t, True, ATTN_SCALE)
        k = head_norm(qkvu, col_k, N_KV_HEADS, k_norm[i], cos_t, sin_t, True, 1.0)
        v = qkvu[:, :, col_v:col_u]
        if last:
            kvc = matmul(hc, w_in, i, col_k, col_u - col_k)
            kc = head_norm(kvc, 0, N_KV_HEADS, k_norm[i], ones_c, ones_c, False, 1.0)
            vc = kvc[:, :, wkv:]
        else:
            qkvuc = matmul(hc, w_in, i)
            qc = head_norm(qkvuc, 0, N_HEADS, q_norm[i], ones_c, ones_c, False, ATTN_SCALE)
            kc = head_norm(qkvuc, col_k, N_KV_HEADS, k_norm[i], ones_c, ones_c, False, 1.0)
            vc = qkvuc[:, :, col_v:col_u]
        y_attn = attention(q, jnp.concatenate([kc, k], axis=1), values_with_ones(jnp.concatenate([vc, v], axis=1)))
        y_hy = hyena_mixer(qkvu, col_u, *hp)
        x = matmul_residual(mixnorm_channel_major(y_attn, y_hy, g_out[i]), w_out, i, x, mod[2])
        if not last:
            yc_attn = attention(qc, kc, values_with_ones(vc))
            yc_hy = hyena_mixer(qkvuc, col_u, *hp)
            xc = matmul_residual(mixnorm(yc_attn, yc_hy, g_out[i]), w_out, i, xc, cmod[2])

        if last:
            hf, logits = norm_mod_router([x], g_ffn[i], [mod[3]], [mod[4]], w_router[i], b_router[i])
        else:
            hf, logits = norm_mod_router([x, xc], g_ffn[i], [mod[3], cmod[3]], [mod[4], cmod[4]],
                                         w_router[i], b_router[i])
        dest, gates, ys = moe_ffn(hf, logits, i, moe_w1_b, moe_b1, moe_w2_b, moe_b2)
        x = moe_combine(dest, ys, gates, x, mod[5], 0, g_final if last else None)
        if not last:
            xc = moe_combine(dest, ys, gates, xc, cmod[5], B * T)
    return x
```

```python
import functools
import math

import numpy as np
import jax
import jax.numpy as jnp
from jax import lax
from jax.experimental import pallas as pl
from jax.experimental.pallas import tpu as pltpu

F32 = jnp.float32
BF16 = jnp.bfloat16

GRID_W = 64
HEAD_DIM = 128
N_HEADS = 16
N_KV_HEADS = 4
GROUP = N_HEADS // N_KV_HEADS
HYENA_ORDER = 2
FILTER_BANDS = 16
FILTER_EMB = 1 + 2 * FILTER_BANDS
FILTER_HIDDEN = 64
DECAY_TARGET = 1e-2
FAST_DECAY_PCT = 0.3
SLOW_DECAY_PCT = 1.5
ROPE_THETA = 10000.0
N_EXPERTS = 32
TOP_K = 4
SWIGLU_LIMIT = 7.0
SWIGLU_ALPHA = 1.702
EPS = 1e-6

LANES = 128
V7X_VMEM_BYTES = 64 * 1024 * 1024
VMEM_LIMIT = V7X_VMEM_BYTES - 8 * 1024 * 1024
FFT_TA = 32
MOE_ROWS = 256


def _params(sem):
    return pltpu.CompilerParams(dimension_semantics=sem, vmem_limit_bytes=VMEM_LIMIT)


def _pack_bf16_pair(x):
    n = x.shape[1] // 2
    b = lax.bitcast_convert_type(x, jnp.uint32)
    r = b + jnp.uint32(0x7FFF) + ((b >> 16) & jnp.uint32(1))
    return (r[:, :n] >> 16) | (r[:, n:] & jnp.uint32(0xFFFF0000))


def _unpack_bf16_pair(w):
    lo = lax.bitcast_convert_type(w << 16, F32)
    hi = lax.bitcast_convert_type(w & jnp.uint32(0xFFFF0000), F32)
    return lo, hi


def _adaln_kernel(c_ref, w_ref, b_ref, o_ref):
    c = c_ref[...]
    s = (c * jax.nn.sigmoid(c)).astype(BF16)
    o_ref[0] = jnp.dot(s, w_ref[0].astype(BF16), preferred_element_type=F32) + b_ref[0]


def adaln_all(cvecs, w_ada, b_ada):
    L, D, N = w_ada.shape
    tn = 512
    return pl.pallas_call(
        _adaln_kernel,
        grid=(L, N // tn),
        in_specs=[
            pl.BlockSpec((8, D), lambda l, j: (0, 0)),
            pl.BlockSpec((1, D, tn), lambda l, j: (l, 0, j)),
            pl.BlockSpec((1, 1, tn), lambda l, j: (l, 0, j)),
        ],
        out_specs=pl.BlockSpec((1, 8, tn), lambda l, j: (l, 0, j)),
        out_shape=jax.ShapeDtypeStruct((L, 8, N), F32),
        compiler_params=_params(("arbitrary", "arbitrary")),
        name="adaln",
    )(cvecs, w_ada, b_ada.reshape(L, 1, N))


def _norm_mod_kernel(x_ref, g_ref, sh_ref, sc_ref, h_ref):
    x = x_ref[0]
    y = x * lax.rsqrt(jnp.mean(x * x, axis=-1, keepdims=True) + EPS) * g_ref[...]
    h_ref[0] = (y * (1.0 + sc_ref[0]) + sh_ref[0]).astype(h_ref.dtype)


def _norm_mod_router_kernel(*refs, n_first):
    xs = refs[:-7]
    g_ref, sh_ref, sc_ref, wr_ref, br_ref, h_ref, l_ref = refs[-7:]

    def run(x_ref):
        x = x_ref[...]
        y = x * lax.rsqrt(jnp.mean(x * x, axis=-1, keepdims=True) + EPS) * g_ref[...]
        h = y * (1.0 + sc_ref[0]) + sh_ref[0]
        h_ref[...] = _pack_bf16_pair(h)
        e = l_ref.shape[1]
        h_hi = h.astype(BF16)
        h_lo = (h - h_hi.astype(F32)).astype(BF16)
        a = jnp.dot(h_hi, wr_ref[...], preferred_element_type=F32)
        b = jnp.dot(h_lo, wr_ref[:, :e], preferred_element_type=F32)
        l_ref[...] = a[:, :e] + a[:, e:] + b + br_ref[...]

    if len(xs) == 1:
        run(xs[0])
    else:
        i = pl.program_id(0)
        pl.when(i < n_first)(lambda: run(xs[0]))
        pl.when(i >= n_first)(lambda: run(xs[1]))


def _mod_map(bm):
    return (lambda b, i: (b, 0, 0)) if bm > 1 else (lambda b, i: (0, 0, 0))


def norm_mod(x, g, shift, scale, out_dtype=BF16):
    B, T, D = x.shape
    tm = min(T, 256)
    return pl.pallas_call(
        _norm_mod_kernel,
        grid=(B, T // tm),
        in_specs=[
            pl.BlockSpec((1, tm, D), lambda b, i: (b, i, 0)),
            pl.BlockSpec((1, D), lambda b, i: (0, 0)),
            pl.BlockSpec((1, 1, D), _mod_map(shift.shape[0])),
            pl.BlockSpec((1, 1, D), _mod_map(scale.shape[0])),
        ],
        out_specs=pl.BlockSpec((1, tm, D), lambda b, i: (b, i, 0)),
        out_shape=jax.ShapeDtypeStruct((B, T, D), out_dtype),
        compiler_params=_params(("arbitrary", "arbitrary")),
        name="norm_mod",
    )(x, g.reshape(1, D), shift, scale)


def norm_mod_router(xs, g, shifts, scales, w_router, b_router):
    D = xs[0].shape[-1]
    E = w_router.shape[1]
    tm = 256
    sizes = [a.shape[0] * a.shape[1] for a in xs]
    n_first = sizes[0] // tm
    n_tiles = sum(sizes) // tm
    assert all(a.shape[1] % tm == 0 for a in xs) and len(xs) <= 2

    def mod_rows(ms):
        return jnp.concatenate([jnp.broadcast_to(m, (a.shape[0], 1, D)) for m, a in zip(ms, xs)], axis=0)

    per0 = xs[0].shape[1] // tm

    def tile_row(i):
        if len(xs) == 1:
            return i // per0
        per1 = xs[1].shape[1] // tm
        return jnp.where(i < n_first, i // per0, xs[0].shape[0] + (i - n_first) // per1)

    x_specs = [pl.BlockSpec((tm, D), lambda i: (jnp.minimum(i, n_first - 1), 0))]
    if len(xs) == 2:
        x_specs.append(pl.BlockSpec((tm, D), lambda i: (jnp.maximum(i - n_first, 0), 0)))
    w_hi = w_router.astype(BF16)
    w_hilo = jnp.concatenate([w_hi, (w_router - w_hi.astype(F32)).astype(BF16)], axis=1)
    return pl.pallas_call(
        functools.partial(_norm_mod_router_kernel, n_first=n_first),
        grid=(n_tiles,),
        in_specs=x_specs + [
            pl.BlockSpec((1, D), lambda i: (0, 0)),
            pl.BlockSpec((1, 1, D), lambda i: (tile_row(i), 0, 0)),
            pl.BlockSpec((1, 1, D), lambda i: (tile_row(i), 0, 0)),
            pl.BlockSpec((D, 2 * E), lambda i: (0, 0)),
            pl.BlockSpec((1, E), lambda i: (0, 0)),
        ],
        out_specs=[pl.BlockSpec((tm, D // 2), lambda i: (i, 0)), pl.BlockSpec((tm, E), lambda i: (i, 0))],
        out_shape=[jax.ShapeDtypeStruct((n_tiles * tm, D // 2), jnp.uint32),
                   jax.ShapeDtypeStruct((n_tiles * tm, E), F32)],
        compiler_params=_params(("arbitrary",)),
        name="norm_mod_router",
    )(*[a.reshape(-1, D) for a in xs], g.reshape(1, D), mod_rows(shifts), mod_rows(scales),
      w_hilo, b_router.reshape(1, E))


def _mixnorm_kernel(a_ref, hy_ref, g_ref, o_ref):
    wa = a_ref.shape[-1]
    a = a_ref[0].astype(F32)
    b = hy_ref[0]
    g = g_ref[...]
    o_ref[0, :, :wa] = (a * lax.rsqrt(jnp.mean(a * a, axis=-1, keepdims=True) + EPS) * g[:, :wa]).astype(o_ref.dtype)
    o_ref[0, :, wa:] = (b * lax.rsqrt(jnp.mean(b * b, axis=-1, keepdims=True) + EPS) * g[:, wa:]).astype(o_ref.dtype)


def mixnorm(y_attn, y_hy, g):
    B, T, wa = y_attn.shape
    wh = y_hy.shape[-1]
    tm = min(T, 256)
    return pl.pallas_call(
        _mixnorm_kernel,
        grid=(B, T // tm),
        in_specs=[
            pl.BlockSpec((1, tm, wa), lambda b, i: (b, i, 0)),
            pl.BlockSpec((1, tm, wh), lambda b, i: (b, i, 0)),
            pl.BlockSpec((1, wa + wh), lambda b, i: (0, 0)),
        ],
        out_specs=pl.BlockSpec((1, tm, wa + wh), lambda b, i: (b, i, 0)),
        out_shape=jax.ShapeDtypeStruct((B, T, wa + wh), BF16),
        compiler_params=_params(("arbitrary", "arbitrary")),
        name="mixnorm",
    )(y_attn, y_hy, g.reshape(1, wa + wh))


def _mixnorm_cm_kernel(a_ref, hy_ref, g_ref, o_ref):
    wa = a_ref.shape[-1]
    g = g_ref[...]
    a = a_ref[0].astype(F32)
    o_ref[0, :, :wa] = (a * lax.rsqrt(jnp.mean(a * a, axis=-1, keepdims=True) + EPS) * g[:, :wa]).astype(o_ref.dtype)
    for j in range(hy_ref.shape[2]):
        b = hy_ref[0, :, j, :].T
        y = b * lax.rsqrt(jnp.mean(b * b, axis=-1, keepdims=True) + EPS) * g[:, wa:]
        o_ref[0, j * LANES:(j + 1) * LANES, wa:] = y.astype(o_ref.dtype)


def mixnorm_channel_major(y_attn, y_hy, g):
    B, T, wa = y_attn.shape
    wh, nta = y_hy.shape[1], y_hy.shape[2]
    jb = 8
    tm = jb * LANES
    assert nta * LANES == T and nta % jb == 0
    return pl.pallas_call(
        _mixnorm_cm_kernel,
        grid=(B, T // tm),
        in_specs=[
            pl.BlockSpec((1, tm, wa), lambda b, i: (b, i, 0)),
            pl.BlockSpec((1, wh, jb, LANES), lambda b, i: (b, 0, i, 0)),
            pl.BlockSpec((1, wa + wh), lambda b, i: (0, 0)),
        ],
        out_specs=pl.BlockSpec((1, tm, wa + wh), lambda b, i: (b, i, 0)),
        out_shape=jax.ShapeDtypeStruct((B, T, wa + wh), BF16),
        compiler_params=_params(("arbitrary", "arbitrary")),
        name="mixnorm_cm",
    )(y_attn, y_hy, g.reshape(1, wa + wh))


def _cast_weight_once(w_ref, wb_ref):
    @pl.when((pl.program_id(1) == 0) & (pl.program_id(2) == 0))
    def _():
        wb_ref[...] = w_ref[...].astype(BF16)


def _mm_kernel(x_ref, w_ref, o_ref, wb_ref):
    _cast_weight_once(w_ref, wb_ref)
    o_ref[0] = jnp.dot(x_ref[0], wb_ref[...], preferred_element_type=F32).astype(o_ref.dtype)


def _mm_res_kernel(x_ref, w_ref, r_ref, g_ref, o_ref, wb_ref):
    _cast_weight_once(w_ref, wb_ref)
    acc = jnp.dot(x_ref[0], wb_ref[...], preferred_element_type=F32)
    o_ref[0] = r_ref[0] + g_ref[0] * acc


MM_TN = 512


def matmul(x, w, layer, col0=0, n_out=None, out_dtype=F32):
    B, T, K = x.shape
    N = n_out or w.shape[2]
    tm, tn = min(T, 1024), MM_TN
    c0 = col0 // tn
    assert c0 * tn == col0 and N % tn == 0
    return pl.pallas_call(
        _mm_kernel,
        grid=(N // tn, B, T // tm),
        in_specs=[
            pl.BlockSpec((1, tm, K), lambda j, b, i: (b, i, 0)),
            pl.BlockSpec((None, K, tn), lambda j, b, i: (layer, 0, c0 + j)),
        ],
        out_specs=pl.BlockSpec((1, tm, tn), lambda j, b, i: (b, i, j)),
        out_shape=jax.ShapeDtypeStruct((B, T, N), out_dtype),
        scratch_shapes=[pltpu.VMEM((K, tn), BF16)],
        compiler_params=_params(("arbitrary", "arbitrary", "arbitrary")),
        name="matmul",
    )(x, w)


def matmul_residual(x, w, layer, res, gate):
    B, T, K = x.shape
    N = w.shape[2]
    tm, tn = min(T, 1024), MM_TN
    gmap = (lambda j, b, i: (b, 0, j)) if gate.shape[0] > 1 else (lambda j, b, i: (0, 0, j))
    return pl.pallas_call(
        _mm_res_kernel,
        grid=(N // tn, B, T // tm),
        in_specs=[
            pl.BlockSpec((1, tm, K), lambda j, b, i: (b, i, 0)),
            pl.BlockSpec((None, K, tn), lambda j, b, i: (layer, 0, j)),
            pl.BlockSpec((1, tm, tn), lambda j, b, i: (b, i, j)),
            pl.BlockSpec((1, 1, tn), gmap),
        ],
        out_specs=pl.BlockSpec((1, tm, tn), lambda j, b, i: (b, i, j)),
        out_shape=jax.ShapeDtypeStruct((B, T, N), F32),
        scratch_shapes=[pltpu.VMEM((K, tn), BF16)],
        compiler_params=_params(("arbitrary", "arbitrary", "arbitrary")),
        name="matmul_residual",
    )(x, w, res, gate)


def _head_norm_kernel(x_ref, g_ref, cos_ref, sin_ref, o_ref, *, n_heads, rope, scale):
    g = g_ref[...]
    if rope:
        cs = cos_ref[...]
        sn = sin_ref[...]
        lane = lax.broadcasted_iota(jnp.int32, cs.shape, 1)
        first = (lane % 64) < 32
    for h in range(n_heads):
        xh = x_ref[0, :, h * HEAD_DIM:(h + 1) * HEAD_DIM]
        y = xh * lax.rsqrt(jnp.mean(xh * xh, axis=-1, keepdims=True) + EPS) * g
        if rope:
            swapped = jnp.where(first, pltpu.roll(y, 96, 1), pltpu.roll(y, 32, 1))
            y = y * cs + swapped * sn
        o_ref[0, :, h * HEAD_DIM:(h + 1) * HEAD_DIM] = (y * scale).astype(o_ref.dtype)


def head_norm(x, col0, n_heads, g, cos_t, sin_t, rope, scale):
    B, T, _ = x.shape
    w = n_heads * HEAD_DIM
    tm = min(T, 256)
    cb = col0 // w
    assert cb * w == col0
    kern = functools.partial(_head_norm_kernel, n_heads=n_heads, rope=rope, scale=scale)
    return pl.pallas_call(
        kern,
        grid=(B, T // tm),
        in_specs=[
            pl.BlockSpec((1, tm, w), lambda b, i: (b, i, cb)),
            pl.BlockSpec((1, HEAD_DIM), lambda b, i: (0, 0)),
            pl.BlockSpec((tm, HEAD_DIM), lambda b, i: (i, 0)),
            pl.BlockSpec((tm, HEAD_DIM), lambda b, i: (i, 0)),
        ],
        out_specs=pl.BlockSpec((1, tm, w), lambda b, i: (b, i, 0)),
        out_shape=jax.ShapeDtypeStruct((B, T, w), BF16),
        compiler_params=_params(("arbitrary", "arbitrary")),
        name="head_norm",
    )(x, g.reshape(1, HEAD_DIM), cos_t, sin_t)


def rope_tables(T):
    pos = np.arange(T)
    r, col = pos // GRID_W, pos % GRID_W
    n_freq = HEAD_DIM // 4
    inv = ROPE_THETA ** (-np.arange(n_freq, dtype=np.float64) / n_freq)
    ar, ac = r[:, None] * inv, col[:, None] * inv
    cos_t = np.concatenate([np.cos(ar), np.cos(ar), np.cos(ac), np.cos(ac)], axis=1)
    sin_t = np.concatenate([-np.sin(ar), np.sin(ar), -np.sin(ac), np.sin(ac)], axis=1)
    return jnp.asarray(cos_t, F32), jnp.asarray(sin_t, F32)


ATTN_TK = 512
ATTN_SCALE = HEAD_DIM ** -0.5 * math.log2(math.e)


def _attn_kernel(q_ref, k_ref, v_ref, o_ref):
    tq = q_ref.shape[1]
    S = k_ref.shape[1]
    q = q_ref[0]
    qs = jnp.concatenate([q[:, j * HEAD_DIM:(j + 1) * HEAD_DIM] for j in range(GROUP)], axis=0)
    rows = GROUP * tq
    m = jnp.full((rows, 1), -jnp.inf, F32)
    acc = jnp.zeros((rows, 2 * HEAD_DIM), F32)
    for k0 in range(0, S, ATTN_TK):
        ks = slice(k0, min(k0 + ATTN_TK, S))
        s = lax.dot_general(qs, k_ref[0, ks, :], (((1,), (1,)), ((), ())), preferred_element_type=F32)
        m_new = jnp.maximum(m, jnp.max(s, axis=-1, keepdims=True))
        p = jnp.exp2(s - m_new)
        acc = jnp.exp2(m - m_new) * acc + jnp.dot(p.astype(BF16), v_ref[0, ks, :], preferred_element_type=F32)
        m = m_new
    o = acc[:, :HEAD_DIM] / acc[:, HEAD_DIM:HEAD_DIM + 1]
    for j in range(GROUP):
        o_ref[0, :, j * HEAD_DIM:(j + 1) * HEAD_DIM] = o[j * tq:(j + 1) * tq, :].astype(o_ref.dtype)


def values_with_ones(v):
    B, S, _ = v.shape
    v4 = v.reshape(B, S, N_KV_HEADS, HEAD_DIM).astype(BF16)
    pad = jnp.zeros((B, S, N_KV_HEADS, HEAD_DIM), BF16).at[..., 0].set(1.0)
    return jnp.concatenate([v4, pad], axis=-1).reshape(B, S, N_KV_HEADS * 2 * HEAD_DIM)


def attention(q, k, v1):
    B, T, _ = q.shape
    S = k.shape[1]
    assert S % 128 == 0
    tq = min(T, 128)
    gw = GROUP * HEAD_DIM
    return pl.pallas_call(
        _attn_kernel,
        grid=(B, N_KV_HEADS, T // tq),
        in_specs=[
            pl.BlockSpec((1, tq, gw), lambda b, g, i: (b, i, g)),
            pl.BlockSpec((1, S, HEAD_DIM), lambda b, g, i: (b, 0, g)),
            pl.BlockSpec((1, S, 2 * HEAD_DIM), lambda b, g, i: (b, 0, g)),
        ],
        out_specs=pl.BlockSpec((1, tq, gw), lambda b, g, i: (b, i, g)),
        out_shape=jax.ShapeDtypeStruct((B, T, N_HEADS * HEAD_DIM), BF16),
        compiler_params=_params(("arbitrary", "arbitrary", "arbitrary")),
        name="attention",
    )(q, k, v1)


PITCH_PAD = 4


def _store_channel_major(o_ref, chunk, n_chan, nta, stage):
    pitch = n_chan + PITCH_PAD
    for ta in range(nta):
        stage[ta * pitch:ta * pitch + n_chan, :] = chunk(ta)
    for c in range(n_chan):
        o_ref[0, c * nta:(c + 1) * nta, :] = stage[pl.ds(c, nta, stride=pitch), :]


def _short_conv_kernel(u_ref, w_ref, b_ref, o_ref, *scratch, channel_major):
    u = u_ref[0]
    L = u.shape[0]
    row = lax.broadcasted_iota(jnp.int32, u.shape, 0)
    prev = jnp.where(row == 0, 0.0, pltpu.roll(u, 1, 0))
    nxt = jnp.where(row == L - 1, 0.0, pltpu.roll(u, L - 1, 0))
    w = w_ref[...]
    y = prev * w[0:1, :] + u * w[1:2, :] + nxt * w[2:3, :] + b_ref[...]
    if not channel_major:
        o_ref[0] = y
    else:
        _store_channel_major(o_ref, lambda ta: y[ta * LANES:(ta + 1) * LANES, :].T, y.shape[1], L // LANES, scratch[0])


def short_conv(x, col0, width, w, b, channel_major):
    B, L, _ = x.shape
    cb = 256 if L > 1024 else 512
    assert col0 % cb == 0 and width % cb == 0
    off = col0 // cb
    scratch = []
    if channel_major:
        nta = L // LANES
        out_spec = pl.BlockSpec((1, cb * nta, LANES), lambda bi, j: (bi, j, 0))
        out_shape = jax.ShapeDtypeStruct((B, width * nta, LANES), F32)
        scratch = [pltpu.VMEM((nta * (cb + PITCH_PAD), LANES), F32)]
    else:
        out_spec = pl.BlockSpec((1, L, cb), lambda bi, j: (bi, 0, j))
        out_shape = jax.ShapeDtypeStruct((B, L, width), F32)
    return pl.pallas_call(
        functools.partial(_short_conv_kernel, channel_major=channel_major),
        grid=(B, width // cb),
        in_specs=[
            pl.BlockSpec((1, L, cb), lambda bi, j: (bi, 0, off + j)),
            pl.BlockSpec((3, cb), lambda bi, j: (0, j)),
            pl.BlockSpec((1, cb), lambda bi, j: (0, j)),
        ],
        out_specs=out_spec,
        out_shape=out_shape,
        scratch_shapes=scratch,
        compiler_params=_params(("arbitrary", "arbitrary")),
        name="short_conv",
    )(x, w, b.reshape(1, width))


def _filter_hidden_kernel(f_ref, w1_ref, b1_ref, w2_ref, b2_ref, fr_ref, o_ref):
    hp = lax.Precision.HIGHEST
    fr = fr_ref[...]
    h1 = jnp.sin(fr * (jnp.dot(w1_ref[...], f_ref[...], preferred_element_type=F32, precision=hp) + b1_ref[...]))
    o_ref[...] = jnp.sin(fr * (jnp.dot(w2_ref[...], h1, preferred_element_type=F32, precision=hp) + b2_ref[...]))


def filter_hidden(feats_t, w1, b1, w2, b2, freq):
    H = FILTER_HIDDEN
    P = feats_t.shape[1]
    fe = feats_t.shape[0]
    w1t = jnp.zeros((H, fe), F32).at[:, :FILTER_EMB].set(w1.T)
    return pl.pallas_call(
        _filter_hidden_kernel,
        out_shape=jax.ShapeDtypeStruct((H, P), F32),
        compiler_params=pltpu.CompilerParams(vmem_limit_bytes=VMEM_LIMIT),
        name="filter_hidden",
    )(feats_t, w1t, b1.reshape(H, 1), w2.T, b2.reshape(H, 1), freq.reshape(H, 1))


def _filter_k_kernel(w3_ref, hid_ref, tpos_ref, mask_ref, e0_ref, dl_ref, bias_ref, o_ref, *scratch, channel_major):
    h = jnp.dot(w3_ref[0], hid_ref[0], preferred_element_type=F32, precision=lax.Precision.HIGHEST)
    decay = jnp.exp(-tpos_ref[0] * dl_ref[...])
    k = mask_ref[0] * decay * h + bias_ref[0] * e0_ref[...]
    if not channel_major:
        o_ref[0] = k
    else:
        rc, L = k.shape
        _store_channel_major(o_ref, lambda ta: k[:, ta * LANES:(ta + 1) * LANES], rc, L // LANES, scratch[0])


def filter_k(w3t, hid, tpos, mask, e0, deltas, bias_aug, channel_major):
    OD, C, H = w3t.shape
    L = hid.shape[-1]
    rc = 256
    scratch = []
    if channel_major:
        nta = L // LANES
        out_spec = pl.BlockSpec((1, rc * nta, LANES), lambda od, j: (od, j, 0))
        out_shape = jax.ShapeDtypeStruct((OD, C * nta, LANES), F32)
        scratch = [pltpu.VMEM((nta * (rc + PITCH_PAD), LANES), F32)]
    else:
        out_spec = pl.BlockSpec((1, rc, L), lambda od, j: (od, j, 0))
        out_shape = jax.ShapeDtypeStruct((OD, C, L), F32)
    return pl.pallas_call(
        functools.partial(_filter_k_kernel, channel_major=channel_major),
        grid=(OD, C // rc),
        in_specs=[
            pl.BlockSpec((1, rc, H), lambda od, j: (od, j, 0)),
            pl.BlockSpec((1, H, L), lambda od, j: (od % 2, 0, 0)),
            pl.BlockSpec((1, 1, L), lambda od, j: (od % 2, 0, 0)),
            pl.BlockSpec((1, 1, L), lambda od, j: (od % 2, 0, 0)),
            pl.BlockSpec((1, L), lambda od, j: (0, 0)),
            pl.BlockSpec((rc, 1), lambda od, j: (j, 0)),
            pl.BlockSpec((1, rc, 1), lambda od, j: (od, j, 0)),
        ],
        out_specs=out_spec,
        out_shape=out_shape,
        scratch_shapes=scratch,
        compiler_params=_params(("arbitrary", "arbitrary")),
        name="filter_k",
    )(w3t, hid, tpos, mask, e0, deltas, bias_aug)


def hyena_filter_taps(L, C, w1, b1, w2, b2, w3, freq, bias, channel_major):
    pos = np.arange(L, dtype=np.float64)
    posr = np.where(pos == 0, 0.0, L - pos)

    def feats(p):
        t = p / max(L - 1, 1)
        bands = np.linspace(1e-4, FILTER_BANDS - 1, FILTER_BANDS)
        ang = (2 * math.pi / L) * p[:, None] * bands
        return np.concatenate([t[:, None], np.cos(ang), np.sin(ang)], axis=-1), t

    f0, t0 = feats(pos)
    f1, t1 = feats(posr)
    fe = 40
    ft = np.zeros((fe, 2 * L))
    ft[:FILTER_EMB, :L] = f0.T
    ft[:FILTER_EMB, L:] = f1.T
    hid = filter_hidden(jnp.asarray(ft, F32), w1, b1, w2, b2, freq)
    hid = jnp.stack([hid[:, :L], hid[:, L:]], axis=0)
    tpos = jnp.asarray(np.stack([t0, t1])[:, None, :], F32)
    mask = jnp.asarray(np.stack([np.ones(L), (pos > 0).astype(np.float64)])[:, None, :], F32)
    e0 = jnp.asarray((pos == 0).astype(np.float64)[None, :], F32)
    max_decay = math.log(DECAY_TARGET) / FAST_DECAY_PCT
    min_decay = math.log(DECAY_TARGET) / SLOW_DECAY_PCT
    deltas = jnp.asarray(np.abs(np.linspace(min_decay, max_decay, C))[:, None], F32)
    w3t = w3.T.reshape(HYENA_ORDER * 2, C, FILTER_HIDDEN)
    bias_aug = jnp.stack([bias, jnp.zeros_like(bias)], axis=1).reshape(HYENA_ORDER * 2, C, 1)
    k = filter_k(w3t, hid, tpos, mask, e0, deltas, bias_aug, channel_major)
    return k.reshape((HYENA_ORDER, 2) + k.shape[1:])


def _fft_consts():
    def emb(fr, fi):
        return np.block([[fr, fi], [-fi, fr]])

    n2 = LANES
    a = 2 * np.pi * np.outer(np.arange(n2), np.arange(n2)) / n2
    g128f = emb(np.cos(a), -np.sin(a))
    g128i = emb(np.cos(a), np.sin(a)) / (2.0 * FFT_TA * LANES)
    ta = np.arange(FFT_TA)
    eye = np.eye(LANES // FFT_TA)
    be = 2 * np.pi * np.outer(ta, ta) / FFT_TA
    bo_f = be + 2 * np.pi * ta[:, None] / (2 * FFT_TA)
    bo_i = be + 2 * np.pi * ta[None, :] / (2 * FFT_TA)
    g32 = [emb(np.kron(eye, np.cos(be)), np.kron(eye, -np.sin(be))), emb(np.kron(eye, np.cos(be)), np.kron(eye, np.sin(be))),
           emb(np.kron(eye, np.cos(bo_f)), np.kron(eye, -np.sin(bo_f))), emb(np.kron(eye, np.cos(bo_i)), np.kron(eye, np.sin(bo_i)))]
    mats = jnp.asarray(np.stack([g128f, g128i] + g32), BF16)
    n = FFT_TA * LANES
    fa = np.tile(ta, LANES // FFT_TA)[:, None]
    tb = np.arange(LANES)[None, :]
    th_e = 2 * np.pi * fa * tb / n
    th_o = 2 * np.pi * (2 * fa + 1) * tb / (2 * n)
    tw = jnp.asarray(np.stack([np.cos(th_e), np.sin(th_e), np.cos(th_o), np.sin(th_o)]), F32)
    return mats, tw


def _rows_to_lanes(re, im):
    parts = []
    for g in range(re.shape[0] // LANES):
        sl = slice(g * LANES, (g + 1) * LANES)
        parts.append(re[sl, :].T if im is None else jnp.concatenate([re[sl, :].T, im[sl, :].T], axis=1))
    return jnp.concatenate(parts, axis=0).astype(BF16)


def _dft_fwd(lhs, g32, g128, twc, tws):
    gm = g32 if lhs.shape[1] == 2 * LANES else g32[:LANES, :]
    o1 = jnp.dot(lhs, gm, preferred_element_type=F32)
    parts = []
    for g in range(lhs.shape[0] // LANES):
        sl = slice(g * LANES, (g + 1) * LANES)
        r = o1[sl, :LANES].T
        i = o1[sl, LANES:].T
        parts.append(jnp.concatenate([r * twc + i * tws, i * twc - r * tws], axis=1))
    o2 = jnp.dot(jnp.concatenate(parts, axis=0).astype(BF16), g128, preferred_element_type=F32)
    return o2[:, :LANES], o2[:, LANES:]


def _inv_stage1(re, im, g128, twc, tws):
    o1 = jnp.dot(jnp.concatenate([re, im], axis=1).astype(BF16), g128, preferred_element_type=F32)
    parts = []
    for g in range(re.shape[0] // LANES):
        sl = slice(g * LANES, (g + 1) * LANES)
        r = o1[sl, :LANES]
        i = o1[sl, LANES:]
        parts.append(jnp.concatenate([(r * twc - i * tws).T, (i * twc + r * tws).T], axis=1))
    return jnp.concatenate(parts, axis=0).astype(BF16)


def _lanes_to_rows(o2):
    rs, is_ = [], []
    for g in range(o2.shape[0] // LANES):
        sl = slice(g * LANES, (g + 1) * LANES)
        rs.append(o2[sl, :LANES].T)
        is_.append(o2[sl, LANES:].T)
    return jnp.concatenate(rs, axis=0), jnp.concatenate(is_, axis=0)


def _cmul(ar, ai, br, bi):
    return ar * br - ai * bi, ar * bi + ai * br


def _hyena_long_kernel(vr_ref, vi_ref, x1r_ref, x1i_ref, x2r_ref, x2i_ref,
                       k0lo_ref, k0hi_ref, k1lo_ref, k1hi_ref, mats_ref, tw_ref, o_ref):
    g128f, g128i = mats_ref[0], mats_ref[1]
    g32e_f, g32e_i, g32o_f, g32o_i = mats_ref[2], mats_ref[3], mats_ref[4], mats_ref[5]
    tce, tse, tco, tso = tw_ref[0], tw_ref[1], tw_ref[2], tw_ref[3]

    def conv(zr, zi, klo, khi):
        ke = _dft_fwd(_rows_to_lanes(klo + khi, None), g32e_f, g128f, tce, tse)
        ko = _dft_fwd(_rows_to_lanes(klo - khi, None), g32o_f, g128f, tco, tso)
        z_t = _rows_to_lanes(zr, zi)
        ze = _dft_fwd(z_t, g32e_f, g128f, tce, tse)
        zo = _dft_fwd(z_t, g32o_f, g128f, tco, tso)
        ye = _inv_stage1(*_cmul(*ze, *ke), g128i, tce, tse)
        yo = _inv_stage1(*_cmul(*zo, *ko), g128i, tco, tso)
        return _lanes_to_rows(jnp.dot(ye, g32e_i, preferred_element_type=F32)
                              + jnp.dot(yo, g32o_i, preferred_element_type=F32))

    yr, yi = conv(vr_ref[0, 0], vi_ref[0, 0], k0lo_ref[0, 0], k0hi_ref[0, 0])
    z1r = x1r_ref[0, 0] * yr
    z1i = x1i_ref[0, 0] * yi
    yr, yi = conv(z1r, z1i, k1lo_ref[0, 0], k1hi_ref[0, 0])
    o_ref[0] = x2r_ref[0, 0] * yr
    o_ref[1] = x2i_ref[0, 0] * yi


def hyena_long(ut, kt):
    _, _, R, _ = ut.shape
    rb = 32 * FFT_TA
    mats, tw = _fft_consts()

    def uspec(b, p):
        return pl.BlockSpec((1, 1, rb, LANES), lambda i: (b, p, i, 0))

    def kspec(o, d):
        return pl.BlockSpec((1, 1, rb, LANES), lambda i: (o, d, i, 0))

    return pl.pallas_call(
        _hyena_long_kernel,
        grid=(R // rb,),
        in_specs=[uspec(0, 0), uspec(1, 0), uspec(0, 1), uspec(1, 1), uspec(0, 2), uspec(1, 2),
                  kspec(0, 0), kspec(0, 1), kspec(1, 0), kspec(1, 1),
                  pl.BlockSpec(mats.shape, lambda i: (0, 0, 0)),
                  pl.BlockSpec(tw.shape, lambda i: (0, 0, 0))],
        out_specs=pl.BlockSpec((2, rb, LANES), lambda i: (0, i, 0)),
        out_shape=jax.ShapeDtypeStruct((2, R, LANES), F32),
        compiler_params=_params(("arbitrary",)),
        name="hyena_long",
    )(ut, ut, ut, ut, ut, ut, kt, kt, kt, kt, mats, tw)


def _hyena_short_kernel(vr_ref, vi_ref, x1r_ref, x1i_ref, x2r_ref, x2i_ref,
                        k0lo_ref, k0hi_ref, k1lo_ref, k1hi_ref, gf_ref, gi_ref, or_ref, oi_ref):
    L = vr_ref.shape[-1]
    n = 2 * L

    def conv(zr, zi, klo, khi):
        kk = jnp.concatenate([klo, khi], axis=1).astype(BF16)
        ks = jnp.dot(kk, gf_ref[:n, :], preferred_element_type=F32)
        kr, ki = ks[:, :n], ks[:, n:]
        zz = jnp.concatenate([zr, zi], axis=1).astype(BF16)
        zs = jnp.dot(zz, gf_ref[n:, :], preferred_element_type=F32)
        sr, si = zs[:, :n], zs[:, n:]
        pr = sr * kr - si * ki
        pi = sr * ki + si * kr
        y = jnp.dot(jnp.concatenate([pr, pi], axis=1).astype(BF16), gi_ref[...], preferred_element_type=F32)
        return y[:, :L], y[:, L:]

    yr, yi = conv(vr_ref[0, 0], vi_ref[0, 0], k0lo_ref[0, 0], k0hi_ref[0, 0])
    z1r = x1r_ref[0, 0] * yr
    z1i = x1i_ref[0, 0] * yi
    yr, yi = conv(z1r, z1i, k1lo_ref[0, 0], k1hi_ref[0, 0])
    or_ref[0] = x2r_ref[0, 0] * yr
    oi_ref[0] = x2i_ref[0, 0] * yi


def hyena_short(ut, kt):
    _, _, C, L = ut.shape
    n = 2 * L
    t = np.arange(n)
    a = 2 * np.pi * np.outer(t, t) / n
    co, si = np.cos(a), np.sin(a)
    gf = np.concatenate([
        np.concatenate([co, -si], axis=1),
        np.concatenate([co[:L], -si[:L]], axis=1),
        np.concatenate([si[:L], co[:L]], axis=1)], axis=0)
    gi = np.concatenate([
        np.concatenate([co[:, :L], si[:, :L]], axis=1),
        np.concatenate([-si[:, :L], co[:, :L]], axis=1)], axis=0) / n
    rc = 256

    def uspec(b, p):
        return pl.BlockSpec((1, 1, rc, L), lambda i: (b, p, i, 0))

    outr, outi = pl.pallas_call(
        _hyena_short_kernel,
        grid=(C // rc,),
        in_specs=[uspec(0, 0), uspec(1, 0), uspec(0, 1), uspec(1, 1), uspec(0, 2), uspec(1, 2),
                  uspec(0, 0), uspec(0, 1), uspec(1, 0), uspec(1, 1),
                  pl.BlockSpec((2 * n, 2 * n), lambda i: (0, 0)),
                  pl.BlockSpec((2 * n, n), lambda i: (0, 0))],
        out_specs=[pl.BlockSpec((1, rc, L), lambda i: (0, i, 0)),
                   pl.BlockSpec((1, rc, L), lambda i: (0, i, 0))],
        out_shape=[jax.ShapeDtypeStruct((1, C, L), F32), jax.ShapeDtypeStruct((1, C, L), F32)],
        compiler_params=_params(("arbitrary",)),
        name="hyena_short",
    )(ut, ut, ut, ut, ut, ut, kt, kt, kt, kt, jnp.asarray(gf, BF16), jnp.asarray(gi, BF16))
    return jnp.concatenate([outr, outi], axis=0)


def hyena_mixer(x, col0, conv_w, conv_b, w1, b1, w2, b2, w3, freq, bias):
    B, L, _ = x.shape
    C = bias.shape[-1]
    assert B == 2
    if L == FFT_TA * LANES:
        ut = short_conv(x, col0, 3 * C, conv_w, conv_b, True).reshape(B, 3, C * FFT_TA, LANES)
        kt = hyena_filter_taps(L, C, w1, b1, w2, b2, w3, freq, bias, True)
        return hyena_long(ut, kt).reshape(B, C, FFT_TA, LANES)
    uc = short_conv(x, col0, 3 * C, conv_w, conv_b, False)
    ut = uc.reshape(B, L, 3, C).transpose(0, 2, 3, 1)
    kt = hyena_filter_taps(L, C, w1, b1, w2, b2, w3, freq, bias, False)
    return hyena_short(ut, kt).transpose(0, 2, 1)


def _topk_kernel(l_ref, idx_ref, gate_ref):
    l = l_ref[...]
    E = l.shape[1]
    lane = lax.broadcasted_iota(jnp.int32, l.shape, 1).astype(F32)
    vals, idxs = [], []
    for _ in range(TOP_K):
        m = jnp.max(l, axis=-1, keepdims=True)
        am = jnp.min(jnp.where(l == m, lane, float(E)), axis=-1, keepdims=True)
        vals.append(m)
        idxs.append(am)
        l = jnp.where(lane == am, -jnp.inf, l)
    v = jnp.concatenate(vals, axis=1)
    e = jnp.exp(v - vals[0])
    gate_ref[...] = e / jnp.sum(e, axis=-1, keepdims=True)
    idx_ref[...] = jnp.concatenate(idxs, axis=1).astype(jnp.int32)


def topk_gates(logits):
    N, E = logits.shape
    tm = 512
    return pl.pallas_call(
        _topk_kernel,
        grid=(N // tm,),
        in_specs=[pl.BlockSpec((tm, E), lambda i: (i, 0))],
        out_specs=[pl.BlockSpec((tm, TOP_K), lambda i: (i, 0)), pl.BlockSpec((tm, TOP_K), lambda i: (i, 0))],
        out_shape=[jax.ShapeDtypeStruct((N, TOP_K), jnp.int32), jax.ShapeDtypeStruct((N, TOP_K), F32)],
        compiler_params=_params(("arbitrary",)),
        name="topk_gates",
    )(logits)


def _rank_kernel(idx_ref, rank_ref, cnt_ref, carry_ref):
    @pl.when(pl.program_id(0) == 0)
    def _():
        carry_ref[...] = jnp.zeros_like(carry_ref)

    idx = idx_ref[...]
    tm = idx.shape[0]
    E = carry_ref.shape[1]
    e_iota = lax.broadcasted_iota(jnp.int32, (tm, E), 1)
    sel = [idx[:, k:k + 1] == e_iota for k in range(TOP_K)]
    m = jnp.zeros((tm, E), F32)
    for s in sel:
        m = m + jnp.where(s, 1.0, 0.0)
    r_i = lax.broadcasted_iota(jnp.int32, (tm, tm), 0)
    c_i = lax.broadcasted_iota(jnp.int32, (tm, tm), 1)
    tri = jnp.where(r_i > c_i, 1.0, 0.0).astype(BF16)
    before = jnp.dot(tri, m.astype(BF16), preferred_element_type=F32) + carry_ref[...]
    ranks = [jnp.sum(jnp.where(s, before, 0.0), axis=-1, keepdims=True) for s in sel]
    rank_ref[...] = jnp.concatenate(ranks, axis=1).astype(jnp.int32)
    carry_ref[...] = carry_ref[...] + jnp.sum(m, axis=0, keepdims=True)
    cnt_ref[...] = carry_ref[...]


def expert_ranks(top_idx):
    N, _ = top_idx.shape
    tm = 512
    return pl.pallas_call(
        _rank_kernel,
        grid=(N // tm,),
        in_specs=[pl.BlockSpec((tm, TOP_K), lambda i: (i, 0))],
        out_specs=[pl.BlockSpec((tm, TOP_K), lambda i: (i, 0)), pl.BlockSpec((1, N_EXPERTS), lambda i: (0, 0))],
        out_shape=[jax.ShapeDtypeStruct((N, TOP_K), jnp.int32), jax.ShapeDtypeStruct((1, N_EXPERTS), F32)],
        scratch_shapes=[pltpu.VMEM((1, N_EXPERTS), F32)],
        compiler_params=_params(("arbitrary",)),
        name="expert_ranks",
    )(top_idx)


def _gather_params(sem):
    return pltpu.CompilerParams(dimension_semantics=sem, vmem_limit_bytes=VMEM_LIMIT, disable_bounds_checks=True)


def _moe_mm_kernel(rowtok_ref, be_ref, nreal_ref, h_ref, w1_ref, b1_ref, w2_ref, b2_ref, o_ref, xbuf, sems):
    i = pl.program_id(0)
    n_real = nreal_ref[0]
    tb = xbuf.shape[1]
    de = w2_ref.shape[0]

    def issue(blk, slot):
        for r in range(tb):
            tok = rowtok_ref[blk * tb + r]
            pltpu.make_async_copy(h_ref.at[pl.ds(tok, 1), :], xbuf.at[slot, pl.ds(r, 1), :], sems.at[slot]).start()

    def wait(slot):
        pltpu.make_async_copy(h_ref.at[pl.ds(0, tb), :], xbuf.at[slot], sems.at[slot]).wait()

    @pl.when(i == 0)
    def _():
        issue(i, 0)

    @pl.when(i < n_real)
    def _():
        slot = i % 2
        wait(slot)
        issue(jnp.minimum(i + 1, n_real - 1), 1 - slot)
        x = jnp.concatenate(_unpack_bf16_pair(xbuf[slot]), axis=1).astype(BF16)
        gu = jnp.dot(x, w1_ref[...], preferred_element_type=F32) + b1_ref[...]
        g = jnp.minimum(gu[:, :de], SWIGLU_LIMIT)
        up = jnp.clip(gu[:, de:], -SWIGLU_LIMIT, SWIGLU_LIMIT)
        act = (up + 1.0) * g * jax.nn.sigmoid(SWIGLU_ALPHA * g)
        o_ref[...] = _pack_bf16_pair(jnp.dot(act.astype(BF16), w2_ref[...], preferred_element_type=F32) + b2_ref[...])

    @pl.when(i == n_real - 1)
    def _():
        wait(1 - i % 2)

    @pl.when(i >= n_real)
    def _():
        o_ref[...] = jnp.zeros_like(o_ref)


def moe_grouped_mm(row_tok, block_e, n_real, n_blocks, h, layer, w1, b1, w2, b2):
    D = 2 * h.shape[1]
    _, E, _, de2 = w1.shape
    de = de2 // 2
    tb = MOE_ROWS
    return pl.pallas_call(
        _moe_mm_kernel,
        grid_spec=pltpu.PrefetchScalarGridSpec(
            num_scalar_prefetch=3,
            grid=(n_blocks,),
            in_specs=[
                pl.BlockSpec(memory_space=pl.ANY),
                pl.BlockSpec((None, None, D, de2), lambda i, rt, be, nr: (layer, be[i], 0, 0)),
                pl.BlockSpec((None, None, 1, de2), lambda i, rt, be, nr: (layer, be[i], 0, 0)),
                pl.BlockSpec((None, None, de, D), lambda i, rt, be, nr: (layer, be[i], 0, 0)),
                pl.BlockSpec((None, None, 1, D), lambda i, rt, be, nr: (layer, be[i], 0, 0)),
            ],
            out_specs=pl.BlockSpec((tb, D // 2), lambda i, rt, be, nr: (i, 0)),
            scratch_shapes=[pltpu.VMEM((2, tb, D // 2), jnp.uint32), pltpu.SemaphoreType.DMA((2,))],
        ),
        out_shape=jax.ShapeDtypeStruct((n_blocks * tb, D // 2), jnp.uint32),
        compiler_params=_gather_params(("arbitrary",)),
        name="moe_grouped_mm",
    )(row_tok, block_e, n_real, h, w1, b1.reshape(b1.shape[0], E, 1, de2), w2, b2.reshape(b2.shape[0], E, 1, D))


def _combine_kernel(dest_ref, ys_ref, gates_ref, x_ref, gt_ref, *refs, tok0, final_norm):
    fg_ref = refs[0] if final_norm else None
    o_ref, buf, sems = refs[-3:]
    nt = pl.num_programs(1)
    step = pl.program_id(0) * nt + pl.program_id(1)
    n_steps = pl.num_programs(0) * nt
    tm = buf.shape[2]

    def issue(st, slot):
        base = (tok0 + st * tm) * TOP_K
        for r in range(tm):
            for k in range(TOP_K):
                d = dest_ref[base + r * TOP_K + k]
                pltpu.make_async_copy(ys_ref.at[pl.ds(d, 1), :], buf.at[slot, k, pl.ds(r, 1), :], sems.at[slot]).start()

    def wait(slot):
        for k in range(TOP_K):
            pltpu.make_async_copy(ys_ref.at[pl.ds(0, tm), :], buf.at[slot, k], sems.at[slot]).wait()

    @pl.when(step == 0)
    def _():
        issue(step, 0)

    slot = step % 2
    wait(slot)
    issue(jnp.minimum(step + 1, n_steps - 1), 1 - slot)
    gates = gates_ref[...]
    acc_lo, acc_hi = 0.0, 0.0
    for k in range(TOP_K):
        lo, hi = _unpack_bf16_pair(buf[slot, k])
        acc_lo = acc_lo + gates[:, k:k + 1] * lo
        acc_hi = acc_hi + gates[:, k:k + 1] * hi
    y = x_ref[0] + gt_ref[0] * jnp.concatenate([acc_lo, acc_hi], axis=1)
    if final_norm:
        y = y * lax.rsqrt(jnp.mean(y * y, axis=-1, keepdims=True) + EPS) * fg_ref[...]
    o_ref[0] = y

    @pl.when(step == n_steps - 1)
    def _():
        wait(1 - slot)


def moe_combine(dest, ys, gates, x, gate_vec, tok0, final_g=None):
    B, T, D = x.shape
    tm = 128
    nt = T // tm
    g0 = tok0 // tm
    gmap = (lambda b, i, d: (b, 0, 0)) if gate_vec.shape[0] > 1 else (lambda b, i, d: (0, 0, 0))
    in_specs = [
        pl.BlockSpec(memory_space=pl.ANY),
        pl.BlockSpec((tm, TOP_K), lambda b, i, d: (g0 + b * nt + i, 0)),
        pl.BlockSpec((1, tm, D), lambda b, i, d: (b, i, 0)),
        pl.BlockSpec((1, 1, D), gmap),
    ]
    args = [dest, ys, gates, x, gate_vec]
    if final_g is not None:
        in_specs.append(pl.BlockSpec((1, D), lambda b, i, d: (0, 0)))
        args.append(final_g.reshape(1, D))
    return pl.pallas_call(
        functools.partial(_combine_kernel, tok0=tok0, final_norm=final_g is not None),
        grid_spec=pltpu.PrefetchScalarGridSpec(
            num_scalar_prefetch=1,
            grid=(B, nt),
            in_specs=in_specs,
            out_specs=pl.BlockSpec((1, tm, D), lambda b, i, d: (b, i, 0)),
            scratch_shapes=[pltpu.VMEM((2, TOP_K, tm, D // 2), jnp.uint32), pltpu.SemaphoreType.DMA((2,))],
        ),
        out_shape=jax.ShapeDtypeStruct((B, T, D), F32),
        compiler_params=_gather_params(("arbitrary", "arbitrary")),
        name="moe_combine",
    )(*args)


def moe_route(logits):
    N = logits.shape[0]
    tb = MOE_ROWS
    top_idx, gates = topk_gates(logits)
    rank, counts = expert_ranks(top_idx)
    counts = counts.reshape(N_EXPERTS).astype(jnp.int32)
    padded = (counts + tb - 1) // tb * tb
    pad_end = jnp.cumsum(padded)
    pad_start = pad_end - padded
    dest = pad_start[top_idx] + rank
    n_blocks = -(-(N * TOP_K) // tb) + N_EXPERTS
    n_real = (pad_end[-1] // tb).astype(jnp.int32).reshape(1)
    starts = jnp.arange(n_blocks, dtype=jnp.int32) * tb
    block_e = jnp.minimum(jnp.sum(pad_end[None, :] <= starts[:, None], axis=1), N_EXPERTS - 1).astype(jnp.int32)
    tok_of = jnp.arange(N * TOP_K, dtype=jnp.int32) // TOP_K
    row_tok = jnp.zeros((n_blocks * tb,), jnp.int32).at[dest.reshape(-1)].set(tok_of)
    return dest.reshape(-1).astype(jnp.int32), gates, row_tok, block_e, n_real, n_blocks


def moe_ffn(h, logits, layer, w1, b1, w2, b2):
    dest, gates, row_tok, block_e, n_real, n_blocks = moe_route(logits)
    ys = moe_grouped_mm(row_tok, block_e, n_real, n_blocks, h, layer, w1, b1, w2, b2)
    return dest, gates, ys


def kernel(x, c, ctx, c_ctx, w_ada, b_ada, g_mix, g_ffn, w_in, q_norm, k_norm, hy_conv_w, hy_conv_b, hy_w1, hy_b1, hy_w2, hy_b2, hy_w3, hy_freq, hy_bias, g_out, w_out, w_router, b_router, moe_w1, moe_b1, moe_w2, moe_b2, g_final):
    B, T, D = x.shape
    C = ctx.shape[1]
    depth = w_ada.shape[0]
    wq = N_HEADS * HEAD_DIM
    wkv = N_KV_HEADS * HEAD_DIM
    col_k, col_v, col_u = wq, wq + wkv, wq + 2 * wkv
    cos_t, sin_t = rope_tables(T)
    ones_c = jnp.ones((C, HEAD_DIM), F32)

    cvecs = jnp.concatenate([c, c_ctx[None, :], jnp.zeros((8 - B - 1, D), F32)], axis=0)
    ada = adaln_all(cvecs, w_ada, b_ada)

    moe_w1_b = moe_w1.astype(BF16)
    moe_w2_b = moe_w2.astype(BF16)

    xc = ctx
    for i in range(depth):
        last = i == depth - 1
        mod = [ada[i, :B, j * D:(j + 1) * D].reshape(B, 1, D) for j in range(6)]
        cmod = [ada[i, B:B + 1, j * D:(j + 1) * D].reshape(1, 1, D) for j in range(6)]
        hp = (hy_conv_w[i], hy_conv_b[i], hy_w1[i], hy_b1[i], hy_w2[i], hy_b2[i], hy_w3[i], hy_freq[i], hy_bias[i])

        h = norm_mod(x, g_mix[i], mod[0], mod[1])
        hc = norm_mod(xc, g_mix[i], cmod[0], cmod[1])
        qkvu = matmul(h, w_in, i)
        q = head_norm(qkvu, 0, N_HEADS, q_norm[i], cos_t, sin_t, True, ATTN_SCALE)
        k = head_norm(qkvu, col_k, N_KV_HEADS, k_norm[i], cos_t, sin_t, True, 1.0)
        v = qkvu[:, :, col_v:col_u]
        if last:
            kvc = matmul(hc, w_in, i, col_k, col_u - col_k)
            kc = head_norm(kvc, 0, N_KV_HEADS, k_norm[i], ones_c, ones_c, False, 1.0)
            vc = kvc[:, :, wkv:]
        else:
            qkvuc = matmul(hc, w_in, i)
            qc = head_norm(qkvuc, 0, N_HEADS, q_norm[i], ones_c, ones_c, False, ATTN_SCALE)
            kc = head_norm(qkvuc, col_k, N_KV_HEADS, k_norm[i], ones_c, ones_c, False, 1.0)
            vc = qkvuc[:, :, col_v:col_u]
        y_attn = attention(q, jnp.concatenate([kc, k], axis=1), values_with_ones(jnp.concatenate([vc, v], axis=1)))
        y_hy = hyena_mixer(qkvu, col_u, *hp)
        x = matmul_residual(mixnorm_channel_major(y_attn, y_hy, g_out[i]), w_out, i, x, mod[2])
        if not last:
            yc_attn = attention(qc, kc, values_with_ones(vc))
            yc_hy = hyena_mixer(qkvuc, col_u, *hp)
            xc = matmul_residual(mixnorm(yc_attn, yc_hy, g_out[i]), w_out, i, xc, cmod[2])

        if last:
            hf, logits = norm_mod_router([x], g_ffn[i], [mod[3]], [mod[4]], w_router[i], b_router[i])
        else:
            hf, logits = norm_mod_router([x, xc], g_ffn[i], [mod[3], cmod[3]], [mod[4], cmod[4]],
                                         w_router[i], b_router[i])
        dest, gates, ys = moe_ffn(hf, logits, i, moe_w1_b, moe_b1, moe_w2_b, moe_b2)
        x = moe_combine(dest, ys, gates, x, mod[5], 0, g_final if last else None)
        if not last:
            xc = moe_combine(dest, ys, gates, xc, cmod[5], B * T)
    return x
```

```python
import functools
import math

import numpy as np
import jax
import jax.numpy as jnp
from jax import lax
from jax.experimental import pallas as pl
from jax.experimental.pallas import tpu as pltpu

F32 = jnp.float32
BF16 = jnp.bfloat16

GRID_W = 64
HEAD_DIM = 128
N_HEADS = 16
N_KV_HEADS = 4
GROUP = N_HEADS // N_KV_HEADS
HYENA_ORDER = 2
FILTER_BANDS = 16
FILTER_EMB = 1 + 2 * FILTER_BANDS
FILTER_HIDDEN = 64
DECAY_TARGET = 1e-2
FAST_DECAY_PCT = 0.3
SLOW_DECAY_PCT = 1.5
ROPE_THETA = 10000.0
N_EXPERTS = 32
TOP_K = 4
SWIGLU_LIMIT = 7.0
SWIGLU_ALPHA = 1.702
EPS = 1e-6

LANES = 128
V7X_VMEM_BYTES = 64 * 1024 * 1024
VMEM_LIMIT = V7X_VMEM_BYTES - 8 * 1024 * 1024
FFT_TA = 32
MOE_ROWS = 256
MOE_KCHUNK = 1024


def _params(sem):
    return pltpu.CompilerParams(dimension_semantics=sem, vmem_limit_bytes=VMEM_LIMIT)


def _pack_bf16_pair(x):
    n = x.shape[1] // 2
    b = lax.bitcast_convert_type(x, jnp.uint32)
    r = b + jnp.uint32(0x7FFF) + ((b >> 16) & jnp.uint32(1))
    return (r[:, :n] >> 16) | (r[:, n:] & jnp.uint32(0xFFFF0000))


def _unpack_bf16_pair(w):
    lo = lax.bitcast_convert_type(w << 16, F32)
    hi = lax.bitcast_convert_type(w & jnp.uint32(0xFFFF0000), F32)
    return lo, hi


def _adaln_kernel(c_ref, w_ref, b_ref, o_ref):
    c = c_ref[...]
    s = (c * jax.nn.sigmoid(c)).astype(BF16)
    o_ref[0] = jnp.dot(s, w_ref[0].astype(BF16), preferred_element_type=F32) + b_ref[0]


def adaln_all(cvecs, w_ada, b_ada):
    L, D, N = w_ada.shape
    tn = 512
    return pl.pallas_call(
        _adaln_kernel,
        grid=(L, N // tn),
        in_specs=[
            pl.BlockSpec((8, D), lambda l, j: (0, 0)),
            pl.BlockSpec((1, D, tn), lambda l, j: (l, 0, j)),
            pl.BlockSpec((1, 1, tn), lambda l, j: (l, 0, j)),
        ],
        out_specs=pl.BlockSpec((1, 8, tn), lambda l, j: (l, 0, j)),
        out_shape=jax.ShapeDtypeStruct((L, 8, N), F32),
        compiler_params=_params(("arbitrary", "arbitrary")),
        name="adaln",
    )(cvecs, w_ada, b_ada.reshape(L, 1, N))


def _norm_mod_kernel(x_ref, g_ref, sh_ref, sc_ref, h_ref):
    x = x_ref[0]
    y = x * lax.rsqrt(jnp.mean(x * x, axis=-1, keepdims=True) + EPS) * g_ref[...]
    h_ref[0] = (y * (1.0 + sc_ref[0]) + sh_ref[0]).astype(h_ref.dtype)


def _norm_mod_router_kernel(*refs, n_first):
    xs = refs[:-7]
    g_ref, sh_ref, sc_ref, wr_ref, br_ref, h_ref, l_ref = refs[-7:]

    def run(x_ref):
        x = x_ref[...]
        y = x * lax.rsqrt(jnp.mean(x * x, axis=-1, keepdims=True) + EPS) * g_ref[...]
        h = y * (1.0 + sc_ref[0]) + sh_ref[0]
        h_ref[...] = _pack_bf16_pair(h)
        e = l_ref.shape[1]
        h_hi = h.astype(BF16)
        h_lo = (h - h_hi.astype(F32)).astype(BF16)
        a = jnp.dot(h_hi, wr_ref[...], preferred_element_type=F32)
        b = jnp.dot(h_lo, wr_ref[:, :e], preferred_element_type=F32)
        l_ref[...] = a[:, :e] + a[:, e:] + b + br_ref[...]

    if len(xs) == 1:
        run(xs[0])
    else:
        i = pl.program_id(0)
        pl.when(i < n_first)(lambda: run(xs[0]))
        pl.when(i >= n_first)(lambda: run(xs[1]))


def _mod_map(bm):
    return (lambda b, i: (b, 0, 0)) if bm > 1 else (lambda b, i: (0, 0, 0))


def norm_mod(x, g, shift, scale, out_dtype=BF16):
    B, T, D = x.shape
    tm = min(T, 256)
    return pl.pallas_call(
        _norm_mod_kernel,
        grid=(B, T // tm),
        in_specs=[
            pl.BlockSpec((1, tm, D), lambda b, i: (b, i, 0)),
            pl.BlockSpec((1, D), lambda b, i: (0, 0)),
            pl.BlockSpec((1, 1, D), _mod_map(shift.shape[0])),
            pl.BlockSpec((1, 1, D), _mod_map(scale.shape[0])),
        ],
        out_specs=pl.BlockSpec((1, tm, D), lambda b, i: (b, i, 0)),
        out_shape=jax.ShapeDtypeStruct((B, T, D), out_dtype),
        compiler_params=_params(("arbitrary", "arbitrary")),
        name="norm_mod",
    )(x, g.reshape(1, D), shift, scale)


def norm_mod_router(xs, g, shifts, scales, w_router, b_router):
    D = xs[0].shape[-1]
    E = w_router.shape[1]
    tm = 256
    sizes = [a.shape[0] * a.shape[1] for a in xs]
    n_first = sizes[0] // tm
    n_tiles = sum(sizes) // tm
    assert all(a.shape[1] % tm == 0 for a in xs) and len(xs) <= 2

    def mod_rows(ms):
        return jnp.concatenate([jnp.broadcast_to(m, (a.shape[0], 1, D)) for m, a in zip(ms, xs)], axis=0)

    per0 = xs[0].shape[1] // tm

    def tile_row(i):
        if len(xs) == 1:
            return i // per0
        per1 = xs[1].shape[1] // tm
        return jnp.where(i < n_first, i // per0, xs[0].shape[0] + (i - n_first) // per1)

    x_specs = [pl.BlockSpec((tm, D), lambda i: (jnp.minimum(i, n_first - 1), 0))]
    if len(xs) == 2:
        x_specs.append(pl.BlockSpec((tm, D), lambda i: (jnp.maximum(i - n_first, 0), 0)))
    w_hi = w_router.astype(BF16)
    w_hilo = jnp.concatenate([w_hi, (w_router - w_hi.astype(F32)).astype(BF16)], axis=1)
    return pl.pallas_call(
        functools.partial(_norm_mod_router_kernel, n_first=n_first),
        grid=(n_tiles,),
        in_specs=x_specs + [
            pl.BlockSpec((1, D), lambda i: (0, 0)),
            pl.BlockSpec((1, 1, D), lambda i: (tile_row(i), 0, 0)),
            pl.BlockSpec((1, 1, D), lambda i: (tile_row(i), 0, 0)),
            pl.BlockSpec((D, 2 * E), lambda i: (0, 0)),
            pl.BlockSpec((1, E), lambda i: (0, 0)),
        ],
        out_specs=[pl.BlockSpec((tm, D // 2), lambda i: (i, 0)), pl.BlockSpec((tm, E), lambda i: (i, 0))],
        out_shape=[jax.ShapeDtypeStruct((n_tiles * tm, D // 2), jnp.uint32),
                   jax.ShapeDtypeStruct((n_tiles * tm, E), F32)],
        compiler_params=_params(("arbitrary",)),
        name="norm_mod_router",
    )(*[a.reshape(-1, D) for a in xs], g.reshape(1, D), mod_rows(shifts), mod_rows(scales),
      w_hilo, b_router.reshape(1, E))


def _mixnorm_kernel(a_ref, hy_ref, g_ref, o_ref):
    wa = a_ref.shape[-1]
    a = a_ref[0].astype(F32)
    b = hy_ref[0]
    g = g_ref[...]
    o_ref[0, :, :wa] = (a * lax.rsqrt(jnp.mean(a * a, axis=-1, keepdims=True) + EPS) * g[:, :wa]).astype(o_ref.dtype)
    o_ref[0, :, wa:] = (b * lax.rsqrt(jnp.mean(b * b, axis=-1, keepdims=True) + EPS) * g[:, wa:]).astype(o_ref.dtype)


def mixnorm(y_attn, y_hy, g):
    B, T, wa = y_attn.shape
    wh = y_hy.shape[-1]
    tm = min(T, 256)
    return pl.pallas_call(
        _mixnorm_kernel,
        grid=(B, T // tm),
        in_specs=[
            pl.BlockSpec((1, tm, wa), lambda b, i: (b, i, 0)),
            pl.BlockSpec((1, tm, wh), lambda b, i: (b, i, 0)),
            pl.BlockSpec((1, wa + wh), lambda b, i: (0, 0)),
        ],
        out_specs=pl.BlockSpec((1, tm, wa + wh), lambda b, i: (b, i, 0)),
        out_shape=jax.ShapeDtypeStruct((B, T, wa + wh), BF16),
        compiler_params=_params(("arbitrary", "arbitrary")),
        name="mixnorm",
    )(y_attn, y_hy, g.reshape(1, wa + wh))


def _mixnorm_cm_kernel(a_ref, hy_ref, g_ref, o_ref):
    wa = a_ref.shape[-1]
    g = g_ref[...]
    a = a_ref[0].astype(F32)
    o_ref[0, :, :wa] = (a * lax.rsqrt(jnp.mean(a * a, axis=-1, keepdims=True) + EPS) * g[:, :wa]).astype(o_ref.dtype)
    for j in range(hy_ref.shape[2]):
        b = hy_ref[0, :, j, :].T
        y = b * lax.rsqrt(jnp.mean(b * b, axis=-1, keepdims=True) + EPS) * g[:, wa:]
        o_ref[0, j * LANES:(j + 1) * LANES, wa:] = y.astype(o_ref.dtype)


def mixnorm_channel_major(y_attn, y_hy, g):
    B, T, wa = y_attn.shape
    wh, nta = y_hy.shape[1], y_hy.shape[2]
    jb = 8
    tm = jb * LANES
    assert nta * LANES == T and nta % jb == 0
    return pl.pallas_call(
        _mixnorm_cm_kernel,
        grid=(B, T // tm),
        in_specs=[
            pl.BlockSpec((1, tm, wa), lambda b, i: (b, i, 0)),
            pl.BlockSpec((1, wh, jb, LANES), lambda b, i: (b, 0, i, 0)),
            pl.BlockSpec((1, wa + wh), lambda b, i: (0, 0)),
        ],
        out_specs=pl.BlockSpec((1, tm, wa + wh), lambda b, i: (b, i, 0)),
        out_shape=jax.ShapeDtypeStruct((B, T, wa + wh), BF16),
        compiler_params=_params(("arbitrary", "arbitrary")),
        name="mixnorm_cm",
    )(y_attn, y_hy, g.reshape(1, wa + wh))


def _cast_weight_once(w_ref, wb_ref):
    @pl.when((pl.program_id(1) == 0) & (pl.program_id(2) == 0))
    def _():
        wb_ref[...] = w_ref[...].astype(BF16)


def _mm_kernel(x_ref, w_ref, o_ref, wb_ref):
    _cast_weight_once(w_ref, wb_ref)
    o_ref[0] = jnp.dot(x_ref[0], wb_ref[...], preferred_element_type=F32).astype(o_ref.dtype)


def _mm_res_kernel(x_ref, w_ref, r_ref, g_ref, o_ref, wb_ref):
    _cast_weight_once(w_ref, wb_ref)
    acc = jnp.dot(x_ref[0], wb_ref[...], preferred_element_type=F32)
    o_ref[0] = r_ref[0] + g_ref[0] * acc


MM_TN = 512


def matmul(x, w, layer, col0=0, n_out=None, out_dtype=F32):
    B, T, K = x.shape
    N = n_out or w.shape[2]
    tm, tn = min(T, 1024), MM_TN
    c0 = col0 // tn
    assert c0 * tn == col0 and N % tn == 0
    return pl.pallas_call(
        _mm_kernel,
        grid=(N // tn, B, T // tm),
        in_specs=[
            pl.BlockSpec((1, tm, K), lambda j, b, i: (b, i, 0)),
            pl.BlockSpec((None, K, tn), lambda j, b, i: (layer, 0, c0 + j)),
        ],
        out_specs=pl.BlockSpec((1, tm, tn), lambda j, b, i: (b, i, j)),
        out_shape=jax.ShapeDtypeStruct((B, T, N), out_dtype),
        scratch_shapes=[pltpu.VMEM((K, tn), BF16)],
        compiler_params=_params(("arbitrary", "arbitrary", "arbitrary")),
        name="matmul",
    )(x, w)


def matmul_residual(x, w, layer, res, gate):
    B, T, K = x.shape
    N = w.shape[2]
    tm, tn = min(T, 1024), MM_TN
    gmap = (lambda j, b, i: (b, 0, j)) if gate.shape[0] > 1 else (lambda j, b, i: (0, 0, j))
    return pl.pallas_call(
        _mm_res_kernel,
        grid=(N // tn, B, T // tm),
        in_specs=[
            pl.BlockSpec((1, tm, K), lambda j, b, i: (b, i, 0)),
            pl.BlockSpec((None, K, tn), lambda j, b, i: (layer, 0, j)),
            pl.BlockSpec((1, tm, tn), lambda j, b, i: (b, i, j)),
            pl.BlockSpec((1, 1, tn), gmap),
        ],
        out_specs=pl.BlockSpec((1, tm, tn), lambda j, b, i: (b, i, j)),
        out_shape=jax.ShapeDtypeStruct((B, T, N), F32),
        scratch_shapes=[pltpu.VMEM((K, tn), BF16)],
        compiler_params=_params(("arbitrary", "arbitrary", "arbitrary")),
        name="matmul_residual",
    )(x, w, res, gate)


def _head_norm_kernel(x_ref, g_ref, cos_ref, sin_ref, o_ref, *, n_heads, rope, scale):
    g = g_ref[...]
    if rope:
        cs = cos_ref[...]
        sn = sin_ref[...]
        lane = lax.broadcasted_iota(jnp.int32, cs.shape, 1)
        first = (lane % 64) < 32
    for h in range(n_heads):
        xh = x_ref[0, :, h * HEAD_DIM:(h + 1) * HEAD_DIM]
        y = xh * lax.rsqrt(jnp.mean(xh * xh, axis=-1, keepdims=True) + EPS) * g
        if rope:
            swapped = jnp.where(first, pltpu.roll(y, 96, 1), pltpu.roll(y, 32, 1))
            y = y * cs + swapped * sn
        o_ref[0, :, h * HEAD_DIM:(h + 1) * HEAD_DIM] = (y * scale).astype(o_ref.dtype)


def head_norm(x, col0, n_heads, g, cos_t, sin_t, rope, scale):
    B, T, _ = x.shape
    w = n_heads * HEAD_DIM
    tm = min(T, 256)
    cb = col0 // w
    assert cb * w == col0
    kern = functools.partial(_head_norm_kernel, n_heads=n_heads, rope=rope, scale=scale)
    return pl.pallas_call(
        kern,
        grid=(B, T // tm),
        in_specs=[
            pl.BlockSpec((1, tm, w), lambda b, i: (b, i, cb)),
            pl.BlockSpec((1, HEAD_DIM), lambda b, i: (0, 0)),
            pl.BlockSpec((tm, HEAD_DIM), lambda b, i: (i, 0)),
            pl.BlockSpec((tm, HEAD_DIM), lambda b, i: (i, 0)),
        ],
        out_specs=pl.BlockSpec((1, tm, w), lambda b, i: (b, i, 0)),
        out_shape=jax.ShapeDtypeStruct((B, T, w), BF16),
        compiler_params=_params(("arbitrary", "arbitrary")),
        name="head_norm",
    )(x, g.reshape(1, HEAD_DIM), cos_t, sin_t)


def rope_tables(T):
    pos = np.arange(T)
    r, col = pos // GRID_W, pos % GRID_W
    n_freq = HEAD_DIM // 4
    inv = ROPE_THETA ** (-np.arange(n_freq, dtype=np.float64) / n_freq)
    ar, ac = r[:, None] * inv, col[:, None] * inv
    cos_t = np.concatenate([np.cos(ar), np.cos(ar), np.cos(ac), np.cos(ac)], axis=1)
    sin_t = np.concatenate([-np.sin(ar), np.sin(ar), -np.sin(ac), np.sin(ac)], axis=1)
    return jnp.asarray(cos_t, F32), jnp.asarray(sin_t, F32)


ATTN_TK = 512
ATTN_SCALE = HEAD_DIM ** -0.5 * math.log2(math.e)


def _attn_kernel(q_ref, k_ref, v_ref, o_ref):
    tq = q_ref.shape[1]
    S = k_ref.shape[1]
    q = q_ref[0]
    qs = jnp.concatenate([q[:, j * HEAD_DIM:(j + 1) * HEAD_DIM] for j in range(GROUP)], axis=0)
    rows = GROUP * tq
    m = jnp.full((rows, 1), -jnp.inf, F32)
    acc = jnp.zeros((rows, 2 * HEAD_DIM), F32)
    for k0 in range(0, S, ATTN_TK):
        ks = slice(k0, min(k0 + ATTN_TK, S))
        s = lax.dot_general(qs, k_ref[0, ks, :], (((1,), (1,)), ((), ())), preferred_element_type=F32)
        m_new = jnp.maximum(m, jnp.max(s, axis=-1, keepdims=True))
        p = jnp.exp2(s - m_new)
        acc = jnp.exp2(m - m_new) * acc + jnp.dot(p.astype(BF16), v_ref[0, ks, :], preferred_element_type=F32)
        m = m_new
    o = acc[:, :HEAD_DIM] / acc[:, HEAD_DIM:HEAD_DIM + 1]
    for j in range(GROUP):
        o_ref[0, :, j * HEAD_DIM:(j + 1) * HEAD_DIM] = o[j * tq:(j + 1) * tq, :].astype(o_ref.dtype)


def values_with_ones(v):
    B, S, _ = v.shape
    v4 = v.reshape(B, S, N_KV_HEADS, HEAD_DIM).astype(BF16)
    pad = jnp.zeros((B, S, N_KV_HEADS, HEAD_DIM), BF16).at[..., 0].set(1.0)
    return jnp.concatenate([v4, pad], axis=-1).reshape(B, S, N_KV_HEADS * 2 * HEAD_DIM)


def attention(q, k, v1):
    B, T, _ = q.shape
    S = k.shape[1]
    assert S % 128 == 0
    tq = min(T, 128)
    gw = GROUP * HEAD_DIM
    return pl.pallas_call(
        _attn_kernel,
        grid=(B, N_KV_HEADS, T // tq),
        in_specs=[
            pl.BlockSpec((1, tq, gw), lambda b, g, i: (b, i, g)),
            pl.BlockSpec((1, S, HEAD_DIM), lambda b, g, i: (b, 0, g)),
            pl.BlockSpec((1, S, 2 * HEAD_DIM), lambda b, g, i: (b, 0, g)),
        ],
        out_specs=pl.BlockSpec((1, tq, gw), lambda b, g, i: (b, i, g)),
        out_shape=jax.ShapeDtypeStruct((B, T, N_HEADS * HEAD_DIM), BF16),
        compiler_params=_params(("arbitrary", "arbitrary", "arbitrary")),
        name="attention",
    )(q, k, v1)


PITCH_PAD = 4


def _store_channel_major(o_ref, chunk, n_chan, nta, stage):
    pitch = n_chan + PITCH_PAD
    for ta in range(nta):
        stage[ta * pitch:ta * pitch + n_chan, :] = chunk(ta)
    for c in range(n_chan):
        o_ref[0, c * nta:(c + 1) * nta, :] = stage[pl.ds(c, nta, stride=pitch), :]


def _short_conv_kernel(u_ref, w_ref, b_ref, o_ref, *scratch, channel_major):
    u = u_ref[0]
    L = u.shape[0]
    row = lax.broadcasted_iota(jnp.int32, u.shape, 0)
    prev = jnp.where(row == 0, 0.0, pltpu.roll(u, 1, 0))
    nxt = jnp.where(row == L - 1, 0.0, pltpu.roll(u, L - 1, 0))
    w = w_ref[...]
    y = prev * w[0:1, :] + u * w[1:2, :] + nxt * w[2:3, :] + b_ref[...]
    if not channel_major:
        o_ref[0] = y
    else:
        _store_channel_major(o_ref, lambda ta: y[ta * LANES:(ta + 1) * LANES, :].T, y.shape[1], L // LANES, scratch[0])


def short_conv(x, col0, width, w, b, channel_major):
    B, L, _ = x.shape
    cb = 256 if L > 1024 else 512
    assert col0 % cb == 0 and width % cb == 0
    off = col0 // cb
    scratch = []
    if channel_major:
        nta = L // LANES
        out_spec = pl.BlockSpec((1, cb * nta, LANES), lambda bi, j: (bi, j, 0))
        out_shape = jax.ShapeDtypeStruct((B, width * nta, LANES), F32)
        scratch = [pltpu.VMEM((nta * (cb + PITCH_PAD), LANES), F32)]
    else:
        out_spec = pl.BlockSpec((1, L, cb), lambda bi, j: (bi, 0, j))
        out_shape = jax.ShapeDtypeStruct((B, L, width), F32)
    return pl.pallas_call(
        functools.partial(_short_conv_kernel, channel_major=channel_major),
        grid=(B, width // cb),
        in_specs=[
            pl.BlockSpec((1, L, cb), lambda bi, j: (bi, 0, off + j)),
            pl.BlockSpec((3, cb), lambda bi, j: (0, j)),
            pl.BlockSpec((1, cb), lambda bi, j: (0, j)),
        ],
        out_specs=out_spec,
        out_shape=out_shape,
        scratch_shapes=scratch,
        compiler_params=_params(("arbitrary", "arbitrary")),
        name="short_conv",
    )(x, w, b.reshape(1, width))


def _filter_hidden_kernel(f_ref, w1_ref, b1_ref, w2_ref, b2_ref, fr_ref, o_ref):
    hp = lax.Precision.HIGHEST
    fr = fr_ref[...]
    h1 = jnp.sin(fr * (jnp.dot(w1_ref[...], f_ref[...], preferred_element_type=F32, precision=hp) + b1_ref[...]))
    o_ref[...] = jnp.sin(fr * (jnp.dot(w2_ref[...], h1, preferred_element_type=F32, precision=hp) + b2_ref[...]))


def filter_hidden(feats_t, w1, b1, w2, b2, freq):
    H = FILTER_HIDDEN
    P = feats_t.shape[1]
    fe = feats_t.shape[0]
    w1t = jnp.zeros((H, fe), F32).at[:, :FILTER_EMB].set(w1.T)
    return pl.pallas_call(
        _filter_hidden_kernel,
        out_shape=jax.ShapeDtypeStruct((H, P), F32),
        compiler_params=pltpu.CompilerParams(vmem_limit_bytes=VMEM_LIMIT),
        name="filter_hidden",
    )(feats_t, w1t, b1.reshape(H, 1), w2.T, b2.reshape(H, 1), freq.reshape(H, 1))


def _filter_k_kernel(w3_ref, hid_ref, tpos_ref, mask_ref, e0_ref, dl_ref, bias_ref, o_ref, *scratch, channel_major):
    w = w3_ref[0]
    w_hi = w.astype(BF16)
    w_lo = (w - w_hi.astype(F32)).astype(BF16)
    hd = hid_ref[0]
    h_hi = hd.astype(BF16)
    h_lo = (hd - h_hi.astype(F32)).astype(BF16)
    h = jnp.dot(jnp.concatenate([w_hi, w_lo, w_hi], axis=1), jnp.concatenate([h_hi, h_hi, h_lo], axis=0),
                preferred_element_type=F32)
    decay = jnp.exp(-tpos_ref[0] * dl_ref[...])
    k = mask_ref[0] * decay * h + bias_ref[0] * e0_ref[...]
    if not channel_major:
        o_ref[0] = k
    else:
        rc, L = k.shape
        _store_channel_major(o_ref, lambda ta: k[:, ta * LANES:(ta + 1) * LANES], rc, L // LANES, scratch[0])


def filter_k(w3t, hid, tpos, mask, e0, deltas, bias_aug, channel_major):
    OD, C, H = w3t.shape
    L = hid.shape[-1]
    rc = 256
    scratch = []
    if channel_major:
        nta = L // LANES
        out_spec = pl.BlockSpec((1, rc * nta, LANES), lambda od, j: (od, j, 0))
        out_shape = jax.ShapeDtypeStruct((OD, C * nta, LANES), F32)
        scratch = [pltpu.VMEM((nta * (rc + PITCH_PAD), LANES), F32)]
    else:
        out_spec = pl.BlockSpec((1, rc, L), lambda od, j: (od, j, 0))
        out_shape = jax.ShapeDtypeStruct((OD, C, L), F32)
    return pl.pallas_call(
        functools.partial(_filter_k_kernel, channel_major=channel_major),
        grid=(OD, C // rc),
        in_specs=[
            pl.BlockSpec((1, rc, H), lambda od, j: (od, j, 0)),
            pl.BlockSpec((1, H, L), lambda od, j: (od % 2, 0, 0)),
            pl.BlockSpec((1, 1, L), lambda od, j: (od % 2, 0, 0)),
            pl.BlockSpec((1, 1, L), lambda od, j: (od % 2, 0, 0)),
            pl.BlockSpec((1, L), lambda od, j: (0, 0)),
            pl.BlockSpec((rc, 1), lambda od, j: (j, 0)),
            pl.BlockSpec((1, rc, 1), lambda od, j: (od, j, 0)),
        ],
        out_specs=out_spec,
        out_shape=out_shape,
        scratch_shapes=scratch,
        compiler_params=_params(("arbitrary", "arbitrary")),
        name="filter_k",
    )(w3t, hid, tpos, mask, e0, deltas, bias_aug)


def hyena_filter_taps(L, C, w1, b1, w2, b2, w3, freq, bias, channel_major):
    pos = np.arange(L, dtype=np.float64)
    posr = np.where(pos == 0, 0.0, L - pos)

    def feats(p):
        t = p / max(L - 1, 1)
        bands = np.linspace(1e-4, FILTER_BANDS - 1, FILTER_BANDS)
        ang = (2 * math.pi / L) * p[:, None] * bands
        return np.concatenate([t[:, None], np.cos(ang), np.sin(ang)], axis=-1), t

    f0, t0 = feats(pos)
    f1, t1 = feats(posr)
    fe = 40
    ft = np.zeros((fe, 2 * L))
    ft[:FILTER_EMB, :L] = f0.T
    ft[:FILTER_EMB, L:] = f1.T
    hid = filter_hidden(jnp.asarray(ft, F32), w1, b1, w2, b2, freq)
    hid = jnp.stack([hid[:, :L], hid[:, L:]], axis=0)
    tpos = jnp.asarray(np.stack([t0, t1])[:, None, :], F32)
    mask = jnp.asarray(np.stack([np.ones(L), (pos > 0).astype(np.float64)])[:, None, :], F32)
    e0 = jnp.asarray((pos == 0).astype(np.float64)[None, :], F32)
    max_decay = math.log(DECAY_TARGET) / FAST_DECAY_PCT
    min_decay = math.log(DECAY_TARGET) / SLOW_DECAY_PCT
    deltas = jnp.asarray(np.abs(np.linspace(min_decay, max_decay, C))[:, None], F32)
    w3t = w3.T.reshape(HYENA_ORDER * 2, C, FILTER_HIDDEN)
    bias_aug = jnp.stack([bias, jnp.zeros_like(bias)], axis=1).reshape(HYENA_ORDER * 2, C, 1)
    k = filter_k(w3t, hid, tpos, mask, e0, deltas, bias_aug, channel_major)
    return k.reshape((HYENA_ORDER, 2) + k.shape[1:])


def _fft_consts():
    def emb(fr, fi):
        return np.block([[fr, fi], [-fi, fr]])

    n2 = LANES
    a = 2 * np.pi * np.outer(np.arange(n2), np.arange(n2)) / n2
    g128f = emb(np.cos(a), -np.sin(a))
    g128i = emb(np.cos(a), np.sin(a)) / (2.0 * FFT_TA * LANES)
    ta = np.arange(FFT_TA)
    eye = np.eye(LANES // FFT_TA)
    be = 2 * np.pi * np.outer(ta, ta) / FFT_TA
    bo_f = be + 2 * np.pi * ta[:, None] / (2 * FFT_TA)
    bo_i = be + 2 * np.pi * ta[None, :] / (2 * FFT_TA)
    g32 = [emb(np.kron(eye, np.cos(be)), np.kron(eye, -np.sin(be))), emb(np.kron(eye, np.cos(be)), np.kron(eye, np.sin(be))),
           emb(np.kron(eye, np.cos(bo_f)), np.kron(eye, -np.sin(bo_f))), emb(np.kron(eye, np.cos(bo_i)), np.kron(eye, np.sin(bo_i)))]
    mats = jnp.asarray(np.stack([g128f, g128i] + g32), BF16)
    n = FFT_TA * LANES
    fa = np.tile(ta, LANES // FFT_TA)[:, None]
    tb = np.arange(LANES)[None, :]
    th_e = 2 * np.pi * fa * tb / n
    th_o = 2 * np.pi * (2 * fa + 1) * tb / (2 * n)
    tw = jnp.asarray(np.stack([np.cos(th_e), np.sin(th_e), np.cos(th_o), np.sin(th_o)]), F32)
    return mats, tw


def _rows_to_lanes(re, im):
    parts = []
    for g in range(re.shape[0] // LANES):
        sl = slice(g * LANES, (g + 1) * LANES)
        parts.append(re[sl, :].T if im is None else jnp.concatenate([re[sl, :].T, im[sl, :].T], axis=1))
    return jnp.concatenate(parts, axis=0).astype(BF16)


def _dft_fwd(lhs, g32, g128, twc, tws):
    gm = g32 if lhs.shape[1] == 2 * LANES else g32[:LANES, :]
    o1 = jnp.dot(lhs, gm, preferred_element_type=F32)
    parts = []
    for g in range(lhs.shape[0] // LANES):
        sl = slice(g * LANES, (g + 1) * LANES)
        r = o1[sl, :LANES].T
        i = o1[sl, LANES:].T
        parts.append(jnp.concatenate([r * twc + i * tws, i * twc - r * tws], axis=1))
    o2 = jnp.dot(jnp.concatenate(parts, axis=0).astype(BF16), g128, preferred_element_type=F32)
    return o2[:, :LANES], o2[:, LANES:]


def _inv_stage1(re, im, g128, twc, tws):
    o1 = jnp.dot(jnp.concatenate([re, im], axis=1).astype(BF16), g128, preferred_element_type=F32)
    parts = []
    for g in range(re.shape[0] // LANES):
        sl = slice(g * LANES, (g + 1) * LANES)
        r = o1[sl, :LANES]
        i = o1[sl, LANES:]
        parts.append(jnp.concatenate([(r * twc - i * tws).T, (i * twc + r * tws).T], axis=1))
    return jnp.concatenate(parts, axis=0).astype(BF16)


def _lanes_to_rows(o2):
    rs, is_ = [], []
    for g in range(o2.shape[0] // LANES):
        sl = slice(g * LANES, (g + 1) * LANES)
        rs.append(o2[sl, :LANES].T)
        is_.append(o2[sl, LANES:].T)
    return jnp.concatenate(rs, axis=0), jnp.concatenate(is_, axis=0)


def _cmul(ar, ai, br, bi):
    return ar * br - ai * bi, ar * bi + ai * br


def _hyena_long_kernel(vr_ref, vi_ref, x1r_ref, x1i_ref, x2r_ref, x2i_ref,
                       k0lo_ref, k0hi_ref, k1lo_ref, k1hi_ref, mats_ref, tw_ref, o_ref):
    g128f, g128i = mats_ref[0], mats_ref[1]
    g32e_f, g32e_i, g32o_f, g32o_i = mats_ref[2], mats_ref[3], mats_ref[4], mats_ref[5]
    tce, tse, tco, tso = tw_ref[0], tw_ref[1], tw_ref[2], tw_ref[3]

    def conv(zr, zi, klo, khi):
        ke = _dft_fwd(_rows_to_lanes(klo + khi, None), g32e_f, g128f, tce, tse)
        ko = _dft_fwd(_rows_to_lanes(klo - khi, None), g32o_f, g128f, tco, tso)
        z_t = _rows_to_lanes(zr, zi)
        ze = _dft_fwd(z_t, g32e_f, g128f, tce, tse)
        zo = _dft_fwd(z_t, g32o_f, g128f, tco, tso)
        ye = _inv_stage1(*_cmul(*ze, *ke), g128i, tce, tse)
        yo = _inv_stage1(*_cmul(*zo, *ko), g128i, tco, tso)
        return _lanes_to_rows(jnp.dot(ye, g32e_i, preferred_element_type=F32)
                              + jnp.dot(yo, g32o_i, preferred_element_type=F32))

    yr, yi = conv(vr_ref[0, 0], vi_ref[0, 0], k0lo_ref[0, 0], k0hi_ref[0, 0])
    z1r = x1r_ref[0, 0] * yr
    z1i = x1i_ref[0, 0] * yi
    yr, yi = conv(z1r, z1i, k1lo_ref[0, 0], k1hi_ref[0, 0])
    o_ref[0] = x2r_ref[0, 0] * yr
    o_ref[1] = x2i_ref[0, 0] * yi


def hyena_long(ut, kt):
    _, _, R, _ = ut.shape
    rb = 32 * FFT_TA
    mats, tw = _fft_consts()

    def uspec(b, p):
        return pl.BlockSpec((1, 1, rb, LANES), lambda i: (b, p, i, 0))

    def kspec(o, d):
        return pl.BlockSpec((1, 1, rb, LANES), lambda i: (o, d, i, 0))

    return pl.pallas_call(
        _hyena_long_kernel,
        grid=(R // rb,),
        in_specs=[uspec(0, 0), uspec(1, 0), uspec(0, 1), uspec(1, 1), uspec(0, 2), uspec(1, 2),
                  kspec(0, 0), kspec(0, 1), kspec(1, 0), kspec(1, 1),
                  pl.BlockSpec(mats.shape, lambda i: (0, 0, 0)),
                  pl.BlockSpec(tw.shape, lambda i: (0, 0, 0))],
        out_specs=pl.BlockSpec((2, rb, LANES), lambda i: (0, i, 0)),
        out_shape=jax.ShapeDtypeStruct((2, R, LANES), F32),
        compiler_params=_params(("arbitrary",)),
        name="hyena_long",
    )(ut, ut, ut, ut, ut, ut, kt, kt, kt, kt, mats, tw)


def _hyena_short_kernel(vr_ref, vi_ref, x1r_ref, x1i_ref, x2r_ref, x2i_ref,
                        k0lo_ref, k0hi_ref, k1lo_ref, k1hi_ref, gf_ref, gi_ref, or_ref, oi_ref):
    L = vr_ref.shape[-1]
    n = 2 * L

    def conv(zr, zi, klo, khi):
        kk = jnp.concatenate([klo, khi], axis=1).astype(BF16)
        ks = jnp.dot(kk, gf_ref[:n, :], preferred_element_type=F32)
        kr, ki = ks[:, :n], ks[:, n:]
        zz = jnp.concatenate([zr, zi], axis=1).astype(BF16)
        zs = jnp.dot(zz, gf_ref[n:, :], preferred_element_type=F32)
        sr, si = zs[:, :n], zs[:, n:]
        pr = sr * kr - si * ki
        pi = sr * ki + si * kr
        y = jnp.dot(jnp.concatenate([pr, pi], axis=1).astype(BF16), gi_ref[...], preferred_element_type=F32)
        return y[:, :L], y[:, L:]

    yr, yi = conv(vr_ref[0, 0], vi_ref[0, 0], k0lo_ref[0, 0], k0hi_ref[0, 0])
    z1r = x1r_ref[0, 0] * yr
    z1i = x1i_ref[0, 0] * yi
    yr, yi = conv(z1r, z1i, k1lo_ref[0, 0], k1hi_ref[0, 0])
    or_ref[0] = x2r_ref[0, 0] * yr
    oi_ref[0] = x2i_ref[0, 0] * yi


def hyena_short(ut, kt):
    _, _, C, L = ut.shape
    n = 2 * L
    t = np.arange(n)
    a = 2 * np.pi * np.outer(t, t) / n
    co, si = np.cos(a), np.sin(a)
    gf = np.concatenate([
        np.concatenate([co, -si], axis=1),
        np.concatenate([co[:L], -si[:L]], axis=1),
        np.concatenate([si[:L], co[:L]], axis=1)], axis=0)
    gi = np.concatenate([
        np.concatenate([co[:, :L], si[:, :L]], axis=1),
        np.concatenate([-si[:, :L], co[:, :L]], axis=1)], axis=0) / n
    rc = 256

    def uspec(b, p):
        return pl.BlockSpec((1, 1, rc, L), lambda i: (b, p, i, 0))

    outr, outi = pl.pallas_call(
        _hyena_short_kernel,
        grid=(C // rc,),
        in_specs=[uspec(0, 0), uspec(1, 0), uspec(0, 1), uspec(1, 1), uspec(0, 2), uspec(1, 2),
                  uspec(0, 0), uspec(0, 1), uspec(1, 0), uspec(1, 1),
                  pl.BlockSpec((2 * n, 2 * n), lambda i: (0, 0)),
                  pl.BlockSpec((2 * n, n), lambda i: (0, 0))],
        out_specs=[pl.BlockSpec((1, rc, L), lambda i: (0, i, 0)),
                   pl.BlockSpec((1, rc, L), lambda i: (0, i, 0))],
        out_shape=[jax.ShapeDtypeStruct((1, C, L), F32), jax.ShapeDtypeStruct((1, C, L), F32)],
        compiler_params=_params(("arbitrary",)),
        name="hyena_short",
    )(ut, ut, ut, ut, ut, ut, kt, kt, kt, kt, jnp.asarray(gf, BF16), jnp.asarray(gi, BF16))
    return jnp.concatenate([outr, outi], axis=0)


def hyena_mixer(x, col0, conv_w, conv_b, w1, b1, w2, b2, w3, freq, bias):
    B, L, _ = x.shape
    C = bias.shape[-1]
    assert B == 2
    if L == FFT_TA * LANES:
        ut = short_conv(x, col0, 3 * C, conv_w, conv_b, True).reshape(B, 3, C * FFT_TA, LANES)
        kt = hyena_filter_taps(L, C, w1, b1, w2, b2, w3, freq, bias, True)
        return hyena_long(ut, kt).reshape(B, C, FFT_TA, LANES)
    uc = short_conv(x, col0, 3 * C, conv_w, conv_b, False)
    ut = uc.reshape(B, L, 3, C).transpose(0, 2, 3, 1)
    kt = hyena_filter_taps(L, C, w1, b1, w2, b2, w3, freq, bias, False)
    return hyena_short(ut, kt).transpose(0, 2, 1)


def _topk_kernel(l_ref, idx_ref, gate_ref):
    l = l_ref[...]
    E = l.shape[1]
    lane = lax.broadcasted_iota(jnp.int32, l.shape, 1).astype(F32)
    vals, idxs = [], []
    for _ in range(TOP_K):
        m = jnp.max(l, axis=-1, keepdims=True)
        am = jnp.min(jnp.where(l == m, lane, float(E)), axis=-1, keepdims=True)
        vals.append(m)
        idxs.append(am)
        l = jnp.where(lane == am, -jnp.inf, l)
    v = jnp.concatenate(vals, axis=1)
    e = jnp.exp(v - vals[0])
    gate_ref[...] = e / jnp.sum(e, axis=-1, keepdims=True)
    idx_ref[...] = jnp.concatenate(idxs, axis=1).astype(jnp.int32)


def topk_gates(logits):
    N, E = logits.shape
    tm = 512
    return pl.pallas_call(
        _topk_kernel,
        grid=(N // tm,),
        in_specs=[pl.BlockSpec((tm, E), lambda i: (i, 0))],
        out_specs=[pl.BlockSpec((tm, TOP_K), lambda i: (i, 0)), pl.BlockSpec((tm, TOP_K), lambda i: (i, 0))],
        out_shape=[jax.ShapeDtypeStruct((N, TOP_K), jnp.int32), jax.ShapeDtypeStruct((N, TOP_K), F32)],
        compiler_params=_params(("arbitrary",)),
        name="topk_gates",
    )(logits)


def _rank_kernel(idx_ref, rank_ref, cnt_ref, carry_ref):
    @pl.when(pl.program_id(0) == 0)
    def _():
        carry_ref[...] = jnp.zeros_like(carry_ref)

    idx = idx_ref[...]
    tm = idx.shape[0]
    E = carry_ref.shape[1]
    e_iota = lax.broadcasted_iota(jnp.int32, (tm, E), 1)
    sel = [idx[:, k:k + 1] == e_iota for k in range(TOP_K)]
    m = jnp.zeros((tm, E), F32)
    for s in sel:
        m = m + jnp.where(s, 1.0, 0.0)
    r_i = lax.broadcasted_iota(jnp.int32, (tm, tm), 0)
    c_i = lax.broadcasted_iota(jnp.int32, (tm, tm), 1)
    tri = jnp.where(r_i > c_i, 1.0, 0.0).astype(BF16)
    before = jnp.dot(tri, m.astype(BF16), preferred_element_type=F32) + carry_ref[...]
    ranks = [jnp.sum(jnp.where(s, before, 0.0), axis=-1, keepdims=True) for s in sel]
    rank_ref[...] = jnp.concatenate(ranks, axis=1).astype(jnp.int32)
    carry_ref[...] = carry_ref[...] + jnp.sum(m, axis=0, keepdims=True)
    cnt_ref[...] = carry_ref[...]


def expert_ranks(top_idx):
    N, _ = top_idx.shape
    tm = 512
    return pl.pallas_call(
        _rank_kernel,
        grid=(N // tm,),
        in_specs=[pl.BlockSpec((tm, TOP_K), lambda i: (i, 0))],
        out_specs=[pl.BlockSpec((tm, TOP_K), lambda i: (i, 0)), pl.BlockSpec((1, N_EXPERTS), lambda i: (0, 0))],
        out_shape=[jax.ShapeDtypeStruct((N, TOP_K), jnp.int32), jax.ShapeDtypeStruct((1, N_EXPERTS), F32)],
        scratch_shapes=[pltpu.VMEM((1, N_EXPERTS), F32)],
        compiler_params=_params(("arbitrary",)),
        name="expert_ranks",
    )(top_idx)


def _gather_params(sem):
    return pltpu.CompilerParams(dimension_semantics=sem, vmem_limit_bytes=VMEM_LIMIT, disable_bounds_checks=True)


def _moe_mm_kernel(rowtok_ref, be_ref, nreal_ref, h_ref, w1_ref, b1_ref, w2_ref, b2_ref, o_ref, xbuf, sems):
    i = pl.program_id(0)
    n_real = nreal_ref[0]
    tb = xbuf.shape[1]
    de = w2_ref.shape[0]

    def issue(blk, slot):
        for r in range(tb):
            tok = rowtok_ref[blk * tb + r]
            pltpu.make_async_copy(h_ref.at[pl.ds(tok, 1), :], xbuf.at[slot, pl.ds(r, 1), :], sems.at[slot]).start()

    def wait(slot):
        pltpu.make_async_copy(h_ref.at[pl.ds(0, tb), :], xbuf.at[slot], sems.at[slot]).wait()

    @pl.when(i == 0)
    def _():
        issue(i, 0)

    @pl.when(i < n_real)
    def _():
        slot = i % 2
        wait(slot)
        issue(jnp.minimum(i + 1, n_real - 1), 1 - slot)
        x = jnp.concatenate(_unpack_bf16_pair(xbuf[slot]), axis=1).astype(BF16)
        gu = b1_ref[...]
        for k0 in range(0, x.shape[1], MOE_KCHUNK):
            ks = slice(k0, k0 + MOE_KCHUNK)
            gu = gu + jnp.dot(x[:, ks], w1_ref[ks, :].astype(BF16), preferred_element_type=F32)
        g = jnp.minimum(gu[:, :de], SWIGLU_LIMIT)
        up = jnp.clip(gu[:, de:], -SWIGLU_LIMIT, SWIGLU_LIMIT)
        act = (up + 1.0) * g * jax.nn.sigmoid(SWIGLU_ALPHA * g)
        o_ref[...] = _pack_bf16_pair(jnp.dot(act.astype(BF16), w2_ref[...], preferred_element_type=F32) + b2_ref[...])

    @pl.when(i == n_real - 1)
    def _():
        wait(1 - i % 2)

    @pl.when(i >= n_real)
    def _():
        o_ref[...] = jnp.zeros_like(o_ref)


def moe_grouped_mm(row_tok, block_e, n_real, n_blocks, h, layer, w1, b1, w2, b2):
    D = 2 * h.shape[1]
    _, E, _, de2 = w1.shape
    de = de2 // 2
    tb = MOE_ROWS
    return pl.pallas_call(
        _moe_mm_kernel,
        grid_spec=pltpu.PrefetchScalarGridSpec(
            num_scalar_prefetch=3,
            grid=(n_blocks,),
            in_specs=[
                pl.BlockSpec(memory_space=pl.ANY),
                pl.BlockSpec((None, None, D, de2), lambda i, rt, be, nr: (layer, be[i], 0, 0)),
                pl.BlockSpec((None, None, 1, de2), lambda i, rt, be, nr: (layer, be[i], 0, 0)),
                pl.BlockSpec((None, None, de, D), lambda i, rt, be, nr: (layer, be[i], 0, 0)),
                pl.BlockSpec((None, None, 1, D), lambda i, rt, be, nr: (layer, be[i], 0, 0)),
            ],
            out_specs=pl.BlockSpec((tb, D // 2), lambda i, rt, be, nr: (i, 0)),
            scratch_shapes=[pltpu.VMEM((2, tb, D // 2), jnp.uint32), pltpu.SemaphoreType.DMA((2,))],
        ),
        out_shape=jax.ShapeDtypeStruct((n_blocks * tb, D // 2), jnp.uint32),
        compiler_params=_gather_params(("arbitrary",)),
        name="moe_grouped_mm",
    )(row_tok, block_e, n_real, h, w1, b1.reshape(b1.shape[0], E, 1, de2), w2, b2.reshape(b2.shape[0], E, 1, D))


def _combine_kernel(dest_ref, ys_ref, gates_ref, x_ref, gt_ref, *refs, tok0, final_norm):
    fg_ref = refs[0] if final_norm else None
    o_ref, buf, sems = refs[-3:]
    nt = pl.num_programs(1)
    step = pl.program_id(0) * nt + pl.program_id(1)
    n_steps = pl.num_programs(0) * nt
    tm = buf.shape[2]

    def issue(st, slot):
        base = (tok0 + st * tm) * TOP_K
        for r in range(tm):
            for k in range(TOP_K):
                d = dest_ref[base + r * TOP_K + k]
                pltpu.make_async_copy(ys_ref.at[pl.ds(d, 1), :], buf.at[slot, k, pl.ds(r, 1), :], sems.at[slot]).start()

    def wait(slot):
        for k in range(TOP_K):
            pltpu.make_async_copy(ys_ref.at[pl.ds(0, tm), :], buf.at[slot, k], sems.at[slot]).wait()

    @pl.when(step == 0)
    def _():
        issue(step, 0)

    slot = step % 2
    wait(slot)
    issue(jnp.minimum(step + 1, n_steps - 1), 1 - slot)
    gates = gates_ref[...]
    acc_lo, acc_hi = 0.0, 0.0
    for k in range(TOP_K):
        lo, hi = _unpack_bf16_pair(buf[slot, k])
        acc_lo = acc_lo + gates[:, k:k + 1] * lo
        acc_hi = acc_hi + gates[:, k:k + 1] * hi
    y = x_ref[0] + gt_ref[0] * jnp.concatenate([acc_lo, acc_hi], axis=1)
    if final_norm:
        y = y * lax.rsqrt(jnp.mean(y * y, axis=-1, keepdims=True) + EPS) * fg_ref[...]
    o_ref[0] = y

    @pl.when(step == n_steps - 1)
    def _():
        wait(1 - slot)


def moe_combine(dest, ys, gates, x, gate_vec, tok0, final_g=None):
    B, T, D = x.shape
    tm = 128
    nt = T // tm
    g0 = tok0 // tm
    gmap = (lambda b, i, d: (b, 0, 0)) if gate_vec.shape[0] > 1 else (lambda b, i, d: (0, 0, 0))
    in_specs = [
        pl.BlockSpec(memory_space=pl.ANY),
        pl.BlockSpec((tm, TOP_K), lambda b, i, d: (g0 + b * nt + i, 0)),
        pl.BlockSpec((1, tm, D), lambda b, i, d: (b, i, 0)),
        pl.BlockSpec((1, 1, D), gmap),
    ]
    args = [dest, ys, gates, x, gate_vec]
    if final_g is not None:
        in_specs.append(pl.BlockSpec((1, D), lambda b, i, d: (0, 0)))
        args.append(final_g.reshape(1, D))
    return pl.pallas_call(
        functools.partial(_combine_kernel, tok0=tok0, final_norm=final_g is not None),
        grid_spec=pltpu.PrefetchScalarGridSpec(
            num_scalar_prefetch=1,
            grid=(B, nt),
            in_specs=in_specs,
            out_specs=pl.BlockSpec((1, tm, D), lambda b, i, d: (b, i, 0)),
            scratch_shapes=[pltpu.VMEM((2, TOP_K, tm, D // 2), jnp.uint32), pltpu.SemaphoreType.DMA((2,))],
        ),
        out_shape=jax.ShapeDtypeStruct((B, T, D), F32),
        compiler_params=_gather_params(("arbitrary", "arbitrary")),
        name="moe_combine",
    )(*args)


def moe_route(logits):
    N = logits.shape[0]
    tb = MOE_ROWS
    top_idx, gates = topk_gates(logits)
    rank, counts = expert_ranks(top_idx)
    counts = counts.reshape(N_EXPERTS).astype(jnp.int32)
    padded = (counts + tb - 1) // tb * tb
    pad_end = jnp.cumsum(padded)
    pad_start = pad_end - padded
    dest = pad_start[top_idx] + rank
    n_blocks = -(-(N * TOP_K) // tb) + N_EXPERTS
    n_real = (pad_end[-1] // tb).astype(jnp.int32).reshape(1)
    starts = jnp.arange(n_blocks, dtype=jnp.int32) * tb
    block_e = jnp.minimum(jnp.sum(pad_end[None, :] <= starts[:, None], axis=1), N_EXPERTS - 1).astype(jnp.int32)
    tok_of = jnp.arange(N * TOP_K, dtype=jnp.int32) // TOP_K
    row_tok = jnp.zeros((n_blocks * tb,), jnp.int32).at[dest.reshape(-1)].set(tok_of)
    return dest.reshape(-1).astype(jnp.int32), gates, row_tok, block_e, n_real, n_blocks


def moe_ffn(h, logits, layer, w1, b1, w2, b2):
    dest, gates, row_tok, block_e, n_real, n_blocks = moe_route(logits)
    ys = moe_grouped_mm(row_tok, block_e, n_real, n_blocks, h, layer, w1, b1, w2, b2)
    return dest, gates, ys


def kernel(x, c, ctx, c_ctx, w_ada, b_ada, g_mix, g_ffn, w_in, q_norm, k_norm, hy_conv_w, hy_conv_b, hy_w1, hy_b1, hy_w2, hy_b2, hy_w3, hy_freq, hy_bias, g_out, w_out, w_router, b_router, moe_w1, moe_b1, moe_w2, moe_b2, g_final):
    B, T, D = x.shape
    C = ctx.shape[1]
    depth = w_ada.shape[0]
    wq = N_HEADS * HEAD_DIM
    wkv = N_KV_HEADS * HEAD_DIM
    col_k, col_v, col_u = wq, wq + wkv, wq + 2 * wkv
    cos_t, sin_t = rope_tables(T)
    ones_c = jnp.ones((C, HEAD_DIM), F32)

    cvecs = jnp.concatenate([c, c_ctx[None, :], jnp.zeros((8 - B - 1, D), F32)], axis=0)
    ada = adaln_all(cvecs, w_ada, b_ada)

    moe_w2_b = moe_w2.astype(BF16)

    xc = ctx
    for i in range(depth):
        last = i == depth - 1
        mod = [ada[i, :B, j * D:(j + 1) * D].reshape(B, 1, D) for j in range(6)]
        cmod = [ada[i, B:B + 1, j * D:(j + 1) * D].reshape(1, 1, D) for j in range(6)]
        hp = (hy_conv_w[i], hy_conv_b[i], hy_w1[i], hy_b1[i], hy_w2[i], hy_b2[i], hy_w3[i], hy_freq[i], hy_bias[i])

        h = norm_mod(x, g_mix[i], mod[0], mod[1])
        hc = norm_mod(xc, g_mix[i], cmod[0], cmod[1])
        qkvu = matmul(h, w_in, i)
        q = head_norm(qkvu, 0, N_HEADS, q_norm[i], cos_t, sin_t, True, ATTN_SCALE)
        k = head_norm(qkvu, col_k, N_KV_HEADS, k_norm[i], cos_t, sin_t, True, 1.0)
        v = qkvu[:, :, col_v:col_u]
        if last:
            kvc = matmul(hc, w_in, i, col_k, col_u - col_k)
            kc = head_norm(kvc, 0, N_KV_HEADS, k_norm[i], ones_c, ones_c, False, 1.0)
            vc = kvc[:, :, wkv:]
        else:
            qkvuc = matmul(hc, w_in, i)
            qc = head_norm(qkvuc, 0, N_HEADS, q_norm[i], ones_c, ones_c, False, ATTN_SCALE)
            kc = head_norm(qkvuc, col_k, N_KV_HEADS, k_norm[i], ones_c, ones_c, False, 1.0)
            vc = qkvuc[:, :, col_v:col_u]
        y_attn = attention(q, jnp.concatenate([kc, k], axis=1), values_with_ones(jnp.concatenate([vc, v], axis=1)))
        y_hy = hyena_mixer(qkvu, col_u, *hp)
        x = matmul_residual(mixnorm_channel_major(y_attn, y_hy, g_out[i]), w_out, i, x, mod[2])
        if not last:
            yc_attn = attention(qc, kc, values_with_ones(vc))
            yc_hy = hyena_mixer(qkvuc, col_u, *hp)
            xc = matmul_residual(mixnorm(yc_attn, yc_hy, g_out[i]), w_out, i, xc, cmod[2])

        if last:
            hf, logits = norm_mod_router([x], g_ffn[i], [mod[3]], [mod[4]], w_router[i], b_router[i])
        else:
            hf, logits = norm_mod_router([x, xc], g_ffn[i], [mod[3], cmod[3]], [mod[4], cmod[4]],
                                         w_router[i], b_router[i])
        dest, gates, ys = moe_ffn(hf, logits, i, moe_w1, moe_b1, moe_w2_b, moe_b2)
        x = moe_combine(dest, ys, gates, x, mod[5], 0, g_final if last else None)
        if not last:
            xc = moe_combine(dest, ys, gates, xc, cmod[5], B * T)
    return x
```

```python
import functools
import math

import numpy as np
import jax
import jax.numpy as jnp
from jax import lax
from jax.experimental import pallas as pl
from jax.experimental.pallas import tpu as pltpu

F32 = jnp.float32
BF16 = jnp.bfloat16

GRID_W = 64
HEAD_DIM = 128
N_HEADS = 16
N_KV_HEADS = 4
GROUP = N_HEADS // N_KV_HEADS
HYENA_ORDER = 2
FILTER_BANDS = 16
FILTER_EMB = 1 + 2 * FILTER_BANDS
FILTER_HIDDEN = 64
DECAY_TARGET = 1e-2
FAST_DECAY_PCT = 0.3
SLOW_DECAY_PCT = 1.5
ROPE_THETA = 10000.0
N_EXPERTS = 32
TOP_K = 4
SWIGLU_LIMIT = 7.0
SWIGLU_ALPHA = 1.702
EPS = 1e-6

LANES = 128
V7X_VMEM_BYTES = 64 * 1024 * 1024
VMEM_LIMIT = V7X_VMEM_BYTES - 8 * 1024 * 1024
FFT_TA = 32
MOE_ROWS = 512
MOE_KCHUNK = 1024
MOE_NCHUNK = 512
MOE_VMEM_LIMIT = V7X_VMEM_BYTES - 4 * 1024 * 1024


def _params(sem):
    return pltpu.CompilerParams(dimension_semantics=sem, vmem_limit_bytes=VMEM_LIMIT)


def _pack_bf16_pair(x):
    n = x.shape[1] // 2
    return _pack_bf16_words(x[:, :n], x[:, n:])


def _pack_bf16_words(lo, hi):
    def rne(x):
        b = lax.bitcast_convert_type(x, jnp.uint32)
        return b + jnp.uint32(0x7FFF) + ((b >> 16) & jnp.uint32(1))

    return (rne(lo) >> 16) | (rne(hi) & jnp.uint32(0xFFFF0000))


def _unpack_bf16_pair(w):
    lo = lax.bitcast_convert_type(w << 16, F32)
    hi = lax.bitcast_convert_type(w & jnp.uint32(0xFFFF0000), F32)
    return lo, hi


def _adaln_kernel(c_ref, w_ref, b_ref, o_ref):
    c = c_ref[...]
    s = (c * jax.nn.sigmoid(c)).astype(BF16)
    o_ref[0] = jnp.dot(s, w_ref[0].astype(BF16), preferred_element_type=F32) + b_ref[0]


def adaln_all(cvecs, w_ada, b_ada):
    L, D, N = w_ada.shape
    tn = 512
    return pl.pallas_call(
        _adaln_kernel,
        grid=(L, N // tn),
        in_specs=[
            pl.BlockSpec((8, D), lambda l, j: (0, 0)),
            pl.BlockSpec((1, D, tn), lambda l, j: (l, 0, j)),
            pl.BlockSpec((1, 1, tn), lambda l, j: (l, 0, j)),
        ],
        out_specs=pl.BlockSpec((1, 8, tn), lambda l, j: (l, 0, j)),
        out_shape=jax.ShapeDtypeStruct((L, 8, N), F32),
        compiler_params=_params(("arbitrary", "arbitrary")),
        name="adaln",
    )(cvecs, w_ada, b_ada.reshape(L, 1, N))


def _norm_mod_kernel(x_ref, g_ref, sh_ref, sc_ref, h_ref):
    x = x_ref[0]
    y = x * lax.rsqrt(jnp.mean(x * x, axis=-1, keepdims=True) + EPS) * g_ref[...]
    h_ref[0] = (y * (1.0 + sc_ref[0]) + sh_ref[0]).astype(h_ref.dtype)


def _norm_mod_router_kernel(*refs, n_first):
    xs = refs[:-7]
    g_ref, sh_ref, sc_ref, wr_ref, br_ref, h_ref, l_ref = refs[-7:]

    def run(x_ref):
        x = x_ref[...]
        y = x * lax.rsqrt(jnp.mean(x * x, axis=-1, keepdims=True) + EPS) * g_ref[...]
        h = y * (1.0 + sc_ref[0]) + sh_ref[0]
        h_ref[...] = _pack_bf16_pair(h)
        e = l_ref.shape[1]
        h_hi = h.astype(BF16)
        h_lo = (h - h_hi.astype(F32)).astype(BF16)
        a = jnp.dot(h_hi, wr_ref[...], preferred_element_type=F32)
        b = jnp.dot(h_lo, wr_ref[:, :e], preferred_element_type=F32)
        l_ref[...] = a[:, :e] + a[:, e:] + b + br_ref[...]

    if len(xs) == 1:
        run(xs[0])
    else:
        i = pl.program_id(0)
        pl.when(i < n_first)(lambda: run(xs[0]))
        pl.when(i >= n_first)(lambda: run(xs[1]))


def _mod_map(bm):
    return (lambda b, i: (b, 0, 0)) if bm > 1 else (lambda b, i: (0, 0, 0))


def norm_mod(x, g, shift, scale, out_dtype=BF16):
    B, T, D = x.shape
    tm = min(T, 256)
    return pl.pallas_call(
        _norm_mod_kernel,
        grid=(B, T // tm),
        in_specs=[
            pl.BlockSpec((1, tm, D), lambda b, i: (b, i, 0)),
            pl.BlockSpec((1, D), lambda b, i: (0, 0)),
            pl.BlockSpec((1, 1, D), _mod_map(shift.shape[0])),
            pl.BlockSpec((1, 1, D), _mod_map(scale.shape[0])),
        ],
        out_specs=pl.BlockSpec((1, tm, D), lambda b, i: (b, i, 0)),
        out_shape=jax.ShapeDtypeStruct((B, T, D), out_dtype),
        compiler_params=_params(("arbitrary", "arbitrary")),
        name="norm_mod",
    )(x, g.reshape(1, D), shift, scale)


def norm_mod_router(xs, g, shifts, scales, w_router, b_router):
    D = xs[0].shape[-1]
    E = w_router.shape[1]
    tm = 256
    sizes = [a.shape[0] * a.shape[1] for a in xs]
    n_first = sizes[0] // tm
    n_tiles = sum(sizes) // tm
    assert all(a.shape[1] % tm == 0 for a in xs) and len(xs) <= 2

    def mod_rows(ms):
        return jnp.concatenate([jnp.broadcast_to(m, (a.shape[0], 1, D)) for m, a in zip(ms, xs)], axis=0)

    per0 = xs[0].shape[1] // tm

    def tile_row(i):
        if len(xs) == 1:
            return i // per0
        per1 = xs[1].shape[1] // tm
        return jnp.where(i < n_first, i // per0, xs[0].shape[0] + (i - n_first) // per1)

    x_specs = [pl.BlockSpec((tm, D), lambda i: (jnp.minimum(i, n_first - 1), 0))]
    if len(xs) == 2:
        x_specs.append(pl.BlockSpec((tm, D), lambda i: (jnp.maximum(i - n_first, 0), 0)))
    w_hi = w_router.astype(BF16)
    w_hilo = jnp.concatenate([w_hi, (w_router - w_hi.astype(F32)).astype(BF16)], axis=1)
    return pl.pallas_call(
        functools.partial(_norm_mod_router_kernel, n_first=n_first),
        grid=(n_tiles,),
        in_specs=x_specs + [
            pl.BlockSpec((1, D), lambda i: (0, 0)),
            pl.BlockSpec((1, 1, D), lambda i: (tile_row(i), 0, 0)),
            pl.BlockSpec((1, 1, D), lambda i: (tile_row(i), 0, 0)),
            pl.BlockSpec((D, 2 * E), lambda i: (0, 0)),
            pl.BlockSpec((1, E), lambda i: (0, 0)),
        ],
        out_specs=[pl.BlockSpec((tm, D // 2), lambda i: (i, 0)), pl.BlockSpec((tm, E), lambda i: (i, 0))],
        out_shape=[jax.ShapeDtypeStruct((n_tiles * tm, D // 2), jnp.uint32),
                   jax.ShapeDtypeStruct((n_tiles * tm, E), F32)],
        compiler_params=_params(("arbitrary",)),
        name="norm_mod_router",
    )(*[a.reshape(-1, D) for a in xs], g.reshape(1, D), mod_rows(shifts), mod_rows(scales),
      w_hilo, b_router.reshape(1, E))


def _mixnorm_kernel(a_ref, hy_ref, g_ref, o_ref):
    wa = a_ref.shape[-1]
    a = a_ref[0].astype(F32)
    b = hy_ref[0]
    g = g_ref[...]
    o_ref[0, :, :wa] = (a * lax.rsqrt(jnp.mean(a * a, axis=-1, keepdims=True) + EPS) * g[:, :wa]).astype(o_ref.dtype)
    o_ref[0, :, wa:] = (b * lax.rsqrt(jnp.mean(b * b, axis=-1, keepdims=True) + EPS) * g[:, wa:]).astype(o_ref.dtype)


def mixnorm(y_attn, y_hy, g):
    B, T, wa = y_attn.shape
    wh = y_hy.shape[-1]
    tm = min(T, 256)
    return pl.pallas_call(
        _mixnorm_kernel,
        grid=(B, T // tm),
        in_specs=[
            pl.BlockSpec((1, tm, wa), lambda b, i: (b, i, 0)),
            pl.BlockSpec((1, tm, wh), lambda b, i: (b, i, 0)),
            pl.BlockSpec((1, wa + wh), lambda b, i: (0, 0)),
        ],
        out_specs=pl.BlockSpec((1, tm, wa + wh), lambda b, i: (b, i, 0)),
        out_shape=jax.ShapeDtypeStruct((B, T, wa + wh), BF16),
        compiler_params=_params(("arbitrary", "arbitrary")),
        name="mixnorm",
    )(y_attn, y_hy, g.reshape(1, wa + wh))


def _mixnorm_cm_kernel(a_ref, hy_ref, g_ref, o_ref):
    wa = a_ref.shape[-1]
    g = g_ref[...]
    a = a_ref[0].astype(F32)
    o_ref[0, :, :wa] = (a * lax.rsqrt(jnp.mean(a * a, axis=-1, keepdims=True) + EPS) * g[:, :wa]).astype(o_ref.dtype)
    for j in range(hy_ref.shape[2]):
        b = hy_ref[0, :, j, :].T
        y = b * lax.rsqrt(jnp.mean(b * b, axis=-1, keepdims=True) + EPS) * g[:, wa:]
        o_ref[0, j * LANES:(j + 1) * LANES, wa:] = y.astype(o_ref.dtype)


def mixnorm_channel_major(y_attn, y_hy, g):
    B, T, wa = y_attn.shape
    wh, nta = y_hy.shape[1], y_hy.shape[2]
    jb = 8
    tm = jb * LANES
    assert nta * LANES == T and nta % jb == 0
    return pl.pallas_call(
        _mixnorm_cm_kernel,
        grid=(B, T // tm),
        in_specs=[
            pl.BlockSpec((1, tm, wa), lambda b, i: (b, i, 0)),
            pl.BlockSpec((1, wh, jb, LANES), lambda b, i: (b, 0, i, 0)),
            pl.BlockSpec((1, wa + wh), lambda b, i: (0, 0)),
        ],
        out_specs=pl.BlockSpec((1, tm, wa + wh), lambda b, i: (b, i, 0)),
        out_shape=jax.ShapeDtypeStruct((B, T, wa + wh), BF16),
        compiler_params=_params(("arbitrary", "arbitrary")),
        name="mixnorm_cm",
    )(y_attn, y_hy, g.reshape(1, wa + wh))


def _cast_weight_once(w_ref, wb_ref):
    @pl.when((pl.program_id(1) == 0) & (pl.program_id(2) == 0))
    def _():
        wb_ref[...] = w_ref[...].astype(BF16)


def _mm_kernel(x_ref, w_ref, o_ref, wb_ref):
    _cast_weight_once(w_ref, wb_ref)
    o_ref[0] = jnp.dot(x_ref[0], wb_ref[...], preferred_element_type=F32).astype(o_ref.dtype)


def _mm_res_kernel(x_ref, w_ref, r_ref, g_ref, o_ref, wb_ref):
    _cast_weight_once(w_ref, wb_ref)
    acc = jnp.dot(x_ref[0], wb_ref[...], preferred_element_type=F32)
    o_ref[0] = r_ref[0] + g_ref[0] * acc


MM_TN = 512


def matmul(x, w, layer, col0=0, n_out=None, out_dtype=F32):
    B, T, K = x.shape
    N = n_out or w.shape[2]
    tm, tn = min(T, 1024), MM_TN
    c0 = col0 // tn
    assert c0 * tn == col0 and N % tn == 0
    return pl.pallas_call(
        _mm_kernel,
        grid=(N // tn, B, T // tm),
        in_specs=[
            pl.BlockSpec((1, tm, K), lambda j, b, i: (b, i, 0)),
            pl.BlockSpec((None, K, tn), lambda j, b, i: (layer, 0, c0 + j)),
        ],
        out_specs=pl.BlockSpec((1, tm, tn), lambda j, b, i: (b, i, j)),
        out_shape=jax.ShapeDtypeStruct((B, T, N), out_dtype),
        scratch_shapes=[pltpu.VMEM((K, tn), BF16)],
        compiler_params=_params(("arbitrary", "arbitrary", "arbitrary")),
        name="matmul",
    )(x, w)


def matmul_residual(x, w, layer, res, gate):
    B, T, K = x.shape
    N = w.shape[2]
    tm, tn = min(T, 1024), MM_TN
    gmap = (lambda j, b, i: (b, 0, j)) if gate.shape[0] > 1 else (lambda j, b, i: (0, 0, j))
    return pl.pallas_call(
        _mm_res_kernel,
        grid=(N // tn, B, T // tm),
        in_specs=[
            pl.BlockSpec((1, tm, K), lambda j, b, i: (b, i, 0)),
            pl.BlockSpec((None, K, tn), lambda j, b, i: (layer, 0, j)),
            pl.BlockSpec((1, tm, tn), lambda j, b, i: (b, i, j)),
            pl.BlockSpec((1, 1, tn), gmap),
        ],
        out_specs=pl.BlockSpec((1, tm, tn), lambda j, b, i: (b, i, j)),
        out_shape=jax.ShapeDtypeStruct((B, T, N), F32),
        scratch_shapes=[pltpu.VMEM((K, tn), BF16)],
        compiler_params=_params(("arbitrary", "arbitrary", "arbitrary")),
        name="matmul_residual",
    )(x, w, res, gate)


def _head_norm_kernel(x_ref, g_ref, cos_ref, sin_ref, o_ref, *, n_heads, rope, scale):
    g = g_ref[...]
    if rope:
        cs = cos_ref[...]
        sn = sin_ref[...]
        lane = lax.broadcasted_iota(jnp.int32, cs.shape, 1)
        first = (lane % 64) < 32
    for h in range(n_heads):
        xh = x_ref[0, :, h * HEAD_DIM:(h + 1) * HEAD_DIM]
        y = xh * lax.rsqrt(jnp.mean(xh * xh, axis=-1, keepdims=True) + EPS) * g
        if rope:
            swapped = jnp.where(first, pltpu.roll(y, 96, 1), pltpu.roll(y, 32, 1))
            y = y * cs + swapped * sn
        o_ref[0, :, h * HEAD_DIM:(h + 1) * HEAD_DIM] = (y * scale).astype(o_ref.dtype)


def head_norm(x, col0, n_heads, g, cos_t, sin_t, rope, scale):
    B, T, _ = x.shape
    w = n_heads * HEAD_DIM
    tm = min(T, 256)
    cb = col0 // w
    assert cb * w == col0
    kern = functools.partial(_head_norm_kernel, n_heads=n_heads, rope=rope, scale=scale)
    return pl.pallas_call(
        kern,
        grid=(B, T // tm),
        in_specs=[
            pl.BlockSpec((1, tm, w), lambda b, i: (b, i, cb)),
            pl.BlockSpec((1, HEAD_DIM), lambda b, i: (0, 0)),
            pl.BlockSpec((tm, HEAD_DIM), lambda b, i: (i, 0)),
            pl.BlockSpec((tm, HEAD_DIM), lambda b, i: (i, 0)),
        ],
        out_specs=pl.BlockSpec((1, tm, w), lambda b, i: (b, i, 0)),
        out_shape=jax.ShapeDtypeStruct((B, T, w), BF16),
        compiler_params=_params(("arbitrary", "arbitrary")),
        name="head_norm",
    )(x, g.reshape(1, HEAD_DIM), cos_t, sin_t)


def rope_tables(T):
    pos = np.arange(T)
    r, col = pos // GRID_W, pos % GRID_W
    n_freq = HEAD_DIM // 4
    inv = ROPE_THETA ** (-np.arange(n_freq, dtype=np.float64) / n_freq)
    ar, ac = r[:, None] * inv, col[:, None] * inv
    cos_t = np.concatenate([np.cos(ar), np.cos(ar), np.cos(ac), np.cos(ac)], axis=1)
    sin_t = np.concatenate([-np.sin(ar), np.sin(ar), -np.sin(ac), np.sin(ac)], axis=1)
    return jnp.asarray(cos_t, F32), jnp.asarray(sin_t, F32)


ATTN_TK = 512
ATTN_SCALE = HEAD_DIM ** -0.5 * math.log2(math.e)


def _attn_kernel(q_ref, k_ref, v_ref, o_ref):
    tq = q_ref.shape[1]
    S = k_ref.shape[1]
    q = q_ref[0]
    qs = jnp.concatenate([q[:, j * HEAD_DIM:(j + 1) * HEAD_DIM] for j in range(GROUP)], axis=0)
    rows = GROUP * tq
    m = jnp.full((rows, 1), -jnp.inf, F32)
    acc = jnp.zeros((rows, 2 * HEAD_DIM), F32)
    for k0 in range(0, S, ATTN_TK):
        ks = slice(k0, min(k0 + ATTN_TK, S))
        s = lax.dot_general(qs, k_ref[0, ks, :], (((1,), (1,)), ((), ())), preferred_element_type=F32)
        m_new = jnp.maximum(m, jnp.max(s, axis=-1, keepdims=True))
        p = jnp.exp2(s - m_new)
        acc = jnp.exp2(m - m_new) * acc + jnp.dot(p.astype(BF16), v_ref[0, ks, :], preferred_element_type=F32)
        m = m_new
    o = acc[:, :HEAD_DIM] / acc[:, HEAD_DIM:HEAD_DIM + 1]
    for j in range(GROUP):
        o_ref[0, :, j * HEAD_DIM:(j + 1) * HEAD_DIM] = o[j * tq:(j + 1) * tq, :].astype(o_ref.dtype)


def values_with_ones(v):
    B, S, _ = v.shape
    v4 = v.reshape(B, S, N_KV_HEADS, HEAD_DIM).astype(BF16)
    pad = jnp.zeros((B, S, N_KV_HEADS, HEAD_DIM), BF16).at[..., 0].set(1.0)
    return jnp.concatenate([v4, pad], axis=-1).reshape(B, S, N_KV_HEADS * 2 * HEAD_DIM)


def attention(q, k, v1):
    B, T, _ = q.shape
    S = k.shape[1]
    assert S % 128 == 0
    tq = min(T, 128)
    gw = GROUP * HEAD_DIM
    return pl.pallas_call(
        _attn_kernel,
        grid=(B, N_KV_HEADS, T // tq),
        in_specs=[
            pl.BlockSpec((1, tq, gw), lambda b, g, i: (b, i, g)),
            pl.BlockSpec((1, S, HEAD_DIM), lambda b, g, i: (b, 0, g)),
            pl.BlockSpec((1, S, 2 * HEAD_DIM), lambda b, g, i: (b, 0, g)),
        ],
        out_specs=pl.BlockSpec((1, tq, gw), lambda b, g, i: (b, i, g)),
        out_shape=jax.ShapeDtypeStruct((B, T, N_HEADS * HEAD_DIM), BF16),
        compiler_params=_params(("arbitrary", "arbitrary", "arbitrary")),
        name="attention",
    )(q, k, v1)


PITCH_PAD = 4


def _store_channel_major(o_ref, chunk, n_chan, nta, stage):
    pitch = n_chan + PITCH_PAD
    for ta in range(nta):
        stage[ta * pitch:ta * pitch + n_chan, :] = chunk(ta)
    for c in range(n_chan):
        o_ref[0, c * nta:(c + 1) * nta, :] = stage[pl.ds(c, nta, stride=pitch), :]


def _short_conv_kernel(u_ref, w_ref, b_ref, o_ref, *scratch, channel_major):
    u = u_ref[0]
    L = u.shape[0]
    row = lax.broadcasted_iota(jnp.int32, u.shape, 0)
    prev = jnp.where(row == 0, 0.0, pltpu.roll(u, 1, 0))
    nxt = jnp.where(row == L - 1, 0.0, pltpu.roll(u, L - 1, 0))
    w = w_ref[...]
    y = prev * w[0:1, :] + u * w[1:2, :] + nxt * w[2:3, :] + b_ref[...]
    if not channel_major:
        o_ref[0] = y
    else:
        _store_channel_major(o_ref, lambda ta: y[ta * LANES:(ta + 1) * LANES, :].T, y.shape[1], L // LANES, scratch[0])


def short_conv(x, col0, width, w, b, channel_major):
    B, L, _ = x.shape
    cb = 256 if L > 1024 else 512
    assert col0 % cb == 0 and width % cb == 0
    off = col0 // cb
    scratch = []
    if channel_major:
        nta = L // LANES
        out_spec = pl.BlockSpec((1, cb * nta, LANES), lambda bi, j: (bi, j, 0))
        out_shape = jax.ShapeDtypeStruct((B, width * nta, LANES), F32)
        scratch = [pltpu.VMEM((nta * (cb + PITCH_PAD), LANES), F32)]
    else:
        out_spec = pl.BlockSpec((1, L, cb), lambda bi, j: (bi, 0, j))
        out_shape = jax.ShapeDtypeStruct((B, L, width), F32)
    return pl.pallas_call(
        functools.partial(_short_conv_kernel, channel_major=channel_major),
        grid=(B, width // cb),
        in_specs=[
            pl.BlockSpec((1, L, cb), lambda bi, j: (bi, 0, off + j)),
            pl.BlockSpec((3, cb), lambda bi, j: (0, j)),
            pl.BlockSpec((1, cb), lambda bi, j: (0, j)),
        ],
        out_specs=out_spec,
        out_shape=out_shape,
        scratch_shapes=scratch,
        compiler_params=_params(("arbitrary", "arbitrary")),
        name="short_conv",
    )(x, w, b.reshape(1, width))


def _filter_hidden_kernel(f_ref, w1_ref, b1_ref, w2_ref, b2_ref, fr_ref, o_ref):
    hp = lax.Precision.HIGHEST
    fr = fr_ref[...]
    h1 = jnp.sin(fr * (jnp.dot(w1_ref[...], f_ref[...], preferred_element_type=F32, precision=hp) + b1_ref[...]))
    o_ref[...] = jnp.sin(fr * (jnp.dot(w2_ref[...], h1, preferred_element_type=F32, precision=hp) + b2_ref[...]))


def filter_hidden(feats_t, w1, b1, w2, b2, freq):
    H = FILTER_HIDDEN
    P = feats_t.shape[1]
    fe = feats_t.shape[0]
    w1t = jnp.zeros((H, fe), F32).at[:, :FILTER_EMB].set(w1.T)
    return pl.pallas_call(
        _filter_hidden_kernel,
        out_shape=jax.ShapeDtypeStruct((H, P), F32),
        compiler_params=pltpu.CompilerParams(vmem_limit_bytes=VMEM_LIMIT),
        name="filter_hidden",
    )(feats_t, w1t, b1.reshape(H, 1), w2.T, b2.reshape(H, 1), freq.reshape(H, 1))


def _filter_k_kernel(w3_ref, hid_ref, tpos_ref, mask_ref, e0_ref, dl_ref, bias_ref, o_ref, *scratch, channel_major):
    w = w3_ref[0]
    w_hi = w.astype(BF16)
    w_lo = (w - w_hi.astype(F32)).astype(BF16)
    hd = hid_ref[0]
    h_hi = hd.astype(BF16)
    h_lo = (hd - h_hi.astype(F32)).astype(BF16)
    h = jnp.dot(jnp.concatenate([w_hi, w_lo, w_hi], axis=1), jnp.concatenate([h_hi, h_hi, h_lo], axis=0),
                preferred_element_type=F32)
    decay = jnp.exp(-tpos_ref[0] * dl_ref[...])
    k = mask_ref[0] * decay * h + bias_ref[0] * e0_ref[...]
    if not channel_major:
        o_ref[0] = k
    else:
        rc, L = k.shape
        _store_channel_major(o_ref, lambda ta: k[:, ta * LANES:(ta + 1) * LANES], rc, L // LANES, scratch[0])


def filter_k(w3t, hid, tpos, mask, e0, deltas, bias_aug, channel_major):
    OD, C, H = w3t.shape
    L = hid.shape[-1]
    rc = 256
    scratch = []
    if channel_major:
        nta = L // LANES
        out_spec = pl.BlockSpec((1, rc * nta, LANES), lambda od, j: (od, j, 0))
        out_shape = jax.ShapeDtypeStruct((OD, C * nta, LANES), F32)
        scratch = [pltpu.VMEM((nta * (rc + PITCH_PAD), LANES), F32)]
    else:
        out_spec = pl.BlockSpec((1, rc, L), lambda od, j: (od, j, 0))
        out_shape = jax.ShapeDtypeStruct((OD, C, L), F32)
    return pl.pallas_call(
        functools.partial(_filter_k_kernel, channel_major=channel_major),
        grid=(OD, C // rc),
        in_specs=[
            pl.BlockSpec((1, rc, H), lambda od, j: (od, j, 0)),
            pl.BlockSpec((1, H, L), lambda od, j: (od % 2, 0, 0)),
            pl.BlockSpec((1, 1, L), lambda od, j: (od % 2, 0, 0)),
            pl.BlockSpec((1, 1, L), lambda od, j: (od % 2, 0, 0)),
            pl.BlockSpec((1, L), lambda od, j: (0, 0)),
            pl.BlockSpec((rc, 1), lambda od, j: (j, 0)),
            pl.BlockSpec((1, rc, 1), lambda od, j: (od, j, 0)),
        ],
        out_specs=out_spec,
        out_shape=out_shape,
        scratch_shapes=scratch,
        compiler_params=_params(("arbitrary", "arbitrary")),
        name="filter_k",
    )(w3t, hid, tpos, mask, e0, deltas, bias_aug)


def hyena_filter_taps(L, C, w1, b1, w2, b2, w3, freq, bias, channel_major):
    pos = np.arange(L, dtype=np.float64)
    posr = np.where(pos == 0, 0.0, L - pos)

    def feats(p):
        t = p / max(L - 1, 1)
        bands = np.linspace(1e-4, FILTER_BANDS - 1, FILTER_BANDS)
        ang = (2 * math.pi / L) * p[:, None] * bands
        return np.concatenate([t[:, None], np.cos(ang), np.sin(ang)], axis=-1), t

    f0, t0 = feats(pos)
    f1, t1 = feats(posr)
    fe = 40
    ft = np.zeros((fe, 2 * L))
    ft[:FILTER_EMB, :L] = f0.T
    ft[:FILTER_EMB, L:] = f1.T
    hid = filter_hidden(jnp.asarray(ft, F32), w1, b1, w2, b2, freq)
    hid = jnp.stack([hid[:, :L], hid[:, L:]], axis=0)
    tpos = jnp.asarray(np.stack([t0, t1])[:, None, :], F32)
    mask = jnp.asarray(np.stack([np.ones(L), (pos > 0).astype(np.float64)])[:, None, :], F32)
    e0 = jnp.asarray((pos == 0).astype(np.float64)[None, :], F32)
    max_decay = math.log(DECAY_TARGET) / FAST_DECAY_PCT
    min_decay = math.log(DECAY_TARGET) / SLOW_DECAY_PCT
    deltas = jnp.asarray(np.abs(np.linspace(min_decay, max_decay, C))[:, None], F32)
    w3t = w3.T.reshape(HYENA_ORDER * 2, C, FILTER_HIDDEN)
    bias_aug = jnp.stack([bias, jnp.zeros_like(bias)], axis=1).reshape(HYENA_ORDER * 2, C, 1)
    k = filter_k(w3t, hid, tpos, mask, e0, deltas, bias_aug, channel_major)
    return k.reshape((HYENA_ORDER, 2) + k.shape[1:])


def _fft_consts():
    def emb(fr, fi):
        return np.block([[fr, fi], [-fi, fr]])

    n2 = LANES
    a = 2 * np.pi * np.outer(np.arange(n2), np.arange(n2)) / n2
    g128f = emb(np.cos(a), -np.sin(a))
    g128i = emb(np.cos(a), np.sin(a)) / (2.0 * FFT_TA * LANES)
    ta = np.arange(FFT_TA)
    eye = np.eye(LANES // FFT_TA)
    be = 2 * np.pi * np.outer(ta, ta) / FFT_TA
    bo_f = be + 2 * np.pi * ta[:, None] / (2 * FFT_TA)
    bo_i = be + 2 * np.pi * ta[None, :] / (2 * FFT_TA)
    g32 = [emb(np.kron(eye, np.cos(be)), np.kron(eye, -np.sin(be))), emb(np.kron(eye, np.cos(be)), np.kron(eye, np.sin(be))),
           emb(np.kron(eye, np.cos(bo_f)), np.kron(eye, -np.sin(bo_f))), emb(np.kron(eye, np.cos(bo_i)), np.kron(eye, np.sin(bo_i)))]
    mats = jnp.asarray(np.stack([g128f, g128i] + g32), BF16)
    n = FFT_TA * LANES
    fa = np.tile(ta, LANES // FFT_TA)[:, None]
    tb = np.arange(LANES)[None, :]
    th_e = 2 * np.pi * fa * tb / n
    th_o = 2 * np.pi * (2 * fa + 1) * tb / (2 * n)
    tw = jnp.asarray(np.stack([np.cos(th_e), np.sin(th_e), np.cos(th_o), np.sin(th_o)]), F32)
    return mats, tw


def _rows_to_lanes(re, im):
    parts = []
    for g in range(re.shape[0] // LANES):
        sl = slice(g * LANES, (g + 1) * LANES)
        parts.append(re[sl, :].T if im is None else jnp.concatenate([re[sl, :].T, im[sl, :].T], axis=1))
    return jnp.concatenate(parts, axis=0).astype(BF16)


def _dft_fwd(lhs, g32, g128, twc, tws):
    gm = g32 if lhs.shape[1] == 2 * LANES else g32[:LANES, :]
    o1 = jnp.dot(lhs, gm, preferred_element_type=F32)
    parts = []
    for g in range(lhs.shape[0] // LANES):
        sl = slice(g * LANES, (g + 1) * LANES)
        r = o1[sl, :LANES].T
        i = o1[sl, LANES:].T
        parts.append(jnp.concatenate([r * twc + i * tws, i * twc - r * tws], axis=1))
    o2 = jnp.dot(jnp.concatenate(parts, axis=0).astype(BF16), g128, preferred_element_type=F32)
    return o2[:, :LANES], o2[:, LANES:]


def _inv_stage1(re, im, g128, twc, tws):
    o1 = jnp.dot(jnp.concatenate([re, im], axis=1).astype(BF16), g128, preferred_element_type=F32)
    parts = []
    for g in range(re.shape[0] // LANES):
        sl = slice(g * LANES, (g + 1) * LANES)
        r = o1[sl, :LANES]
        i = o1[sl, LANES:]
        parts.append(jnp.concatenate([(r * twc - i * tws).T, (i * twc + r * tws).T], axis=1))
    return jnp.concatenate(parts, axis=0).astype(BF16)


def _lanes_to_rows(o2):
    rs, is_ = [], []
    for g in range(o2.shape[0] // LANES):
        sl = slice(g * LANES, (g + 1) * LANES)
        rs.append(o2[sl, :LANES].T)
        is_.append(o2[sl, LANES:].T)
    return jnp.concatenate(rs, axis=0), jnp.concatenate(is_, axis=0)


def _cmul(ar, ai, br, bi):
    return ar * br - ai * bi, ar * bi + ai * br


def _hyena_long_kernel(vr_ref, vi_ref, x1r_ref, x1i_ref, x2r_ref, x2i_ref,
                       k0lo_ref, k0hi_ref, k1lo_ref, k1hi_ref, mats_ref, tw_ref, o_ref):
    g128f, g128i = mats_ref[0], mats_ref[1]
    g32e_f, g32e_i, g32o_f, g32o_i = mats_ref[2], mats_ref[3], mats_ref[4], mats_ref[5]
    tce, tse, tco, tso = tw_ref[0], tw_ref[1], tw_ref[2], tw_ref[3]

    def conv(zr, zi, klo, khi):
        ke = _dft_fwd(_rows_to_lanes(klo + khi, None), g32e_f, g128f, tce, tse)
        ko = _dft_fwd(_rows_to_lanes(klo - khi, None), g32o_f, g128f, tco, tso)
        z_t = _rows_to_lanes(zr, zi)
        ze = _dft_fwd(z_t, g32e_f, g128f, tce, tse)
        zo = _dft_fwd(z_t, g32o_f, g128f, tco, tso)
        ye = _inv_stage1(*_cmul(*ze, *ke), g128i, tce, tse)
        yo = _inv_stage1(*_cmul(*zo, *ko), g128i, tco, tso)
        return _lanes_to_rows(jnp.dot(ye, g32e_i, preferred_element_type=F32)
                              + jnp.dot(yo, g32o_i, preferred_element_type=F32))

    yr, yi = conv(vr_ref[0, 0], vi_ref[0, 0], k0lo_ref[0, 0], k0hi_ref[0, 0])
    z1r = x1r_ref[0, 0] * yr
    z1i = x1i_ref[0, 0] * yi
    yr, yi = conv(z1r, z1i, k1lo_ref[0, 0], k1hi_ref[0, 0])
    o_ref[0] = x2r_ref[0, 0] * yr
    o_ref[1] = x2i_ref[0, 0] * yi


def hyena_long(ut, kt):
    _, _, R, _ = ut.shape
    rb = 32 * FFT_TA
    mats, tw = _fft_consts()

    def uspec(b, p):
        return pl.BlockSpec((1, 1, rb, LANES), lambda i: (b, p, i, 0))

    def kspec(o, d):
        return pl.BlockSpec((1, 1, rb, LANES), lambda i: (o, d, i, 0))

    return pl.pallas_call(
        _hyena_long_kernel,
        grid=(R // rb,),
        in_specs=[uspec(0, 0), uspec(1, 0), uspec(0, 1), uspec(1, 1), uspec(0, 2), uspec(1, 2),
                  kspec(0, 0), kspec(0, 1), kspec(1, 0), kspec(1, 1),
                  pl.BlockSpec(mats.shape, lambda i: (0, 0, 0)),
                  pl.BlockSpec(tw.shape, lambda i: (0, 0, 0))],
        out_specs=pl.BlockSpec((2, rb, LANES), lambda i: (0, i, 0)),
        out_shape=jax.ShapeDtypeStruct((2, R, LANES), F32),
        compiler_params=_params(("arbitrary",)),
        name="hyena_long",
    )(ut, ut, ut, ut, ut, ut, kt, kt, kt, kt, mats, tw)


def _hyena_short_kernel(vr_ref, vi_ref, x1r_ref, x1i_ref, x2r_ref, x2i_ref,
                        k0lo_ref, k0hi_ref, k1lo_ref, k1hi_ref, gf_ref, gi_ref, or_ref, oi_ref):
    L = vr_ref.shape[-1]
    n = 2 * L

    def conv(zr, zi, klo, khi):
        kk = jnp.concatenate([klo, khi], axis=1).astype(BF16)
        ks = jnp.dot(kk, gf_ref[:n, :], preferred_element_type=F32)
        kr, ki = ks[:, :n], ks[:, n:]
        zz = jnp.concatenate([zr, zi], axis=1).astype(BF16)
        zs = jnp.dot(zz, gf_ref[n:, :], preferred_element_type=F32)
        sr, si = zs[:, :n], zs[:, n:]
        pr = sr * kr - si * ki
        pi = sr * ki + si * kr
        y = jnp.dot(jnp.concatenate([pr, pi], axis=1).astype(BF16), gi_ref[...], preferred_element_type=F32)
        return y[:, :L], y[:, L:]

    yr, yi = conv(vr_ref[0, 0], vi_ref[0, 0], k0lo_ref[0, 0], k0hi_ref[0, 0])
    z1r = x1r_ref[0, 0] * yr
    z1i = x1i_ref[0, 0] * yi
    yr, yi = conv(z1r, z1i, k1lo_ref[0, 0], k1hi_ref[0, 0])
    or_ref[0] = x2r_ref[0, 0] * yr
    oi_ref[0] = x2i_ref[0, 0] * yi


def hyena_short(ut, kt):
    _, _, C, L = ut.shape
    n = 2 * L
    t = np.arange(n)
    a = 2 * np.pi * np.outer(t, t) / n
    co, si = np.cos(a), np.sin(a)
    gf = np.concatenate([
        np.concatenate([co, -si], axis=1),
        np.concatenate([co[:L], -si[:L]], axis=1),
        np.concatenate([si[:L], co[:L]], axis=1)], axis=0)
    gi = np.concatenate([
        np.concatenate([co[:, :L], si[:, :L]], axis=1),
        np.concatenate([-si[:, :L], co[:, :L]], axis=1)], axis=0) / n
    rc = 256

    def uspec(b, p):
        return pl.BlockSpec((1, 1, rc, L), lambda i: (b, p, i, 0))

    outr, outi = pl.pallas_call(
        _hyena_short_kernel,
        grid=(C // rc,),
        in_specs=[uspec(0, 0), uspec(1, 0), uspec(0, 1), uspec(1, 1), uspec(0, 2), uspec(1, 2),
                  uspec(0, 0), uspec(0, 1), uspec(1, 0), uspec(1, 1),
                  pl.BlockSpec((2 * n, 2 * n), lambda i: (0, 0)),
                  pl.BlockSpec((2 * n, n), lambda i: (0, 0))],
        out_specs=[pl.BlockSpec((1, rc, L), lambda i: (0, i, 0)),
                   pl.BlockSpec((1, rc, L), lambda i: (0, i, 0))],
        out_shape=[jax.ShapeDtypeStruct((1, C, L), F32), jax.ShapeDtypeStruct((1, C, L), F32)],
        compiler_params=_params(("arbitrary",)),
        name="hyena_short",
    )(ut, ut, ut, ut, ut, ut, kt, kt, kt, kt, jnp.asarray(gf, BF16), jnp.asarray(gi, BF16))
    return jnp.concatenate([outr, outi], axis=0)


def hyena_mixer(x, col0, conv_w, conv_b, w1, b1, w2, b2, w3, freq, bias):
    B, L, _ = x.shape
    C = bias.shape[-1]
    assert B == 2
    if L == FFT_TA * LANES:
        ut = short_conv(x, col0, 3 * C, conv_w, conv_b, True).reshape(B, 3, C * FFT_TA, LANES)
        kt = hyena_filter_taps(L, C, w1, b1, w2, b2, w3, freq, bias, True)
        return hyena_long(ut, kt).reshape(B, C, FFT_TA, LANES)
    uc = short_conv(x, col0, 3 * C, conv_w, conv_b, False)
    ut = uc.reshape(B, L, 3, C).transpose(0, 2, 3, 1)
    kt = hyena_filter_taps(L, C, w1, b1, w2, b2, w3, freq, bias, False)
    return hyena_short(ut, kt).transpose(0, 2, 1)


def _topk_kernel(l_ref, idx_ref, gate_ref):
    l = l_ref[...]
    E = l.shape[1]
    lane = lax.broadcasted_iota(jnp.int32, l.shape, 1).astype(F32)
    vals, idxs = [], []
    for _ in range(TOP_K):
        m = jnp.max(l, axis=-1, keepdims=True)
        am = jnp.min(jnp.where(l == m, lane, float(E)), axis=-1, keepdims=True)
        vals.append(m)
        idxs.append(am)
        l = jnp.where(lane == am, -jnp.inf, l)
    v = jnp.concatenate(vals, axis=1)
    e = jnp.exp(v - vals[0])
    gate_ref[...] = e / jnp.sum(e, axis=-1, keepdims=True)
    idx_ref[...] = jnp.concatenate(idxs, axis=1).astype(jnp.int32)


def topk_gates(logits):
    N, E = logits.shape
    tm = 512
    return pl.pallas_call(
        _topk_kernel,
        grid=(N // tm,),
        in_specs=[pl.BlockSpec((tm, E), lambda i: (i, 0))],
        out_specs=[pl.BlockSpec((tm, TOP_K), lambda i: (i, 0)), pl.BlockSpec((tm, TOP_K), lambda i: (i, 0))],
        out_shape=[jax.ShapeDtypeStruct((N, TOP_K), jnp.int32), jax.ShapeDtypeStruct((N, TOP_K), F32)],
        compiler_params=_params(("arbitrary",)),
        name="topk_gates",
    )(logits)


def _rank_kernel(idx_ref, rank_ref, cnt_ref, carry_ref):
    @pl.when(pl.program_id(0) == 0)
    def _():
        carry_ref[...] = jnp.zeros_like(carry_ref)

    idx = idx_ref[...]
    tm = idx.shape[0]
    E = carry_ref.shape[1]
    e_iota = lax.broadcasted_iota(jnp.int32, (tm, E), 1)
    sel = [idx[:, k:k + 1] == e_iota for k in range(TOP_K)]
    m = jnp.zeros((tm, E), F32)
    for s in sel:
        m = m + jnp.where(s, 1.0, 0.0)
    r_i = lax.broadcasted_iota(jnp.int32, (tm, tm), 0)
    c_i = lax.broadcasted_iota(jnp.int32, (tm, tm), 1)
    tri = jnp.where(r_i > c_i, 1.0, 0.0).astype(BF16)
    before = jnp.dot(tri, m.astype(BF16), preferred_element_type=F32) + carry_ref[...]
    ranks = [jnp.sum(jnp.where(s, before, 0.0), axis=-1, keepdims=True) for s in sel]
    rank_ref[...] = jnp.concatenate(ranks, axis=1).astype(jnp.int32)
    carry_ref[...] = carry_ref[...] + jnp.sum(m, axis=0, keepdims=True)
    cnt_ref[...] = carry_ref[...]


def expert_ranks(top_idx):
    N, _ = top_idx.shape
    tm = 512
    return pl.pallas_call(
        _rank_kernel,
        grid=(N // tm,),
        in_specs=[pl.BlockSpec((tm, TOP_K), lambda i: (i, 0))],
        out_specs=[pl.BlockSpec((tm, TOP_K), lambda i: (i, 0)), pl.BlockSpec((1, N_EXPERTS), lambda i: (0, 0))],
        out_shape=[jax.ShapeDtypeStruct((N, TOP_K), jnp.int32), jax.ShapeDtypeStruct((1, N_EXPERTS), F32)],
        scratch_shapes=[pltpu.VMEM((1, N_EXPERTS), F32)],
        compiler_params=_params(("arbitrary",)),
        name="expert_ranks",
    )(top_idx)


def _gather_params(sem, vmem_limit=VMEM_LIMIT):
    return pltpu.CompilerParams(dimension_semantics=sem, vmem_limit_bytes=vmem_limit, disable_bounds_checks=True)


def _moe_mm_kernel(rowtok_ref, be_ref, nreal_ref, h_ref, w1_ref, b1_ref, w2_ref, b2_ref, o_ref, xbuf, sems):
    i = pl.program_id(0)
    n_real = nreal_ref[0]
    tb = xbuf.shape[1]
    de = w2_ref.shape[0]

    def issue(blk, slot):
        for r in range(tb):
            tok = rowtok_ref[blk * tb + r]
            pltpu.make_async_copy(h_ref.at[pl.ds(tok, 1), :], xbuf.at[slot, pl.ds(r, 1), :], sems.at[slot]).start()

    def wait(slot):
        pltpu.make_async_copy(h_ref.at[pl.ds(0, tb), :], xbuf.at[slot], sems.at[slot]).wait()

    @pl.when(i == 0)
    def _():
        issue(i, 0)

    @pl.when(i < n_real)
    def _():
        slot = i % 2
        wait(slot)
        issue(jnp.minimum(i + 1, n_real - 1), 1 - slot)
        x = jnp.concatenate(_unpack_bf16_pair(xbuf[slot]), axis=1).astype(BF16)
        gu = b1_ref[...]
        for k0 in range(0, x.shape[1], MOE_KCHUNK):
            ks = slice(k0, k0 + MOE_KCHUNK)
            gu = gu + jnp.dot(x[:, ks], w1_ref[ks, :].astype(BF16), preferred_element_type=F32)
        g = jnp.minimum(gu[:, :de], SWIGLU_LIMIT)
        up = jnp.clip(gu[:, de:], -SWIGLU_LIMIT, SWIGLU_LIMIT)
        act = ((up + 1.0) * g * jax.nn.sigmoid(SWIGLU_ALPHA * g)).astype(BF16)
        half = o_ref.shape[1]
        for n0 in range(0, half, MOE_NCHUNK):
            lo = slice(n0, n0 + MOE_NCHUNK)
            hi = slice(half + n0, half + n0 + MOE_NCHUNK)
            y_lo = jnp.dot(act, w2_ref[:, lo], preferred_element_type=F32) + b2_ref[:, lo]
            y_hi = jnp.dot(act, w2_ref[:, hi], preferred_element_type=F32) + b2_ref[:, hi]
            o_ref[:, lo] = _pack_bf16_words(y_lo, y_hi)

    @pl.when(i == n_real - 1)
    def _():
        wait(1 - i % 2)

    @pl.when(i >= n_real)
    def _():
        o_ref[...] = jnp.zeros_like(o_ref)


def moe_grouped_mm(row_tok, block_e, n_real, n_blocks, h, layer, w1, b1, w2, b2):
    D = 2 * h.shape[1]
    _, E, _, de2 = w1.shape
    de = de2 // 2
    tb = MOE_ROWS
    return pl.pallas_call(
        _moe_mm_kernel,
        grid_spec=pltpu.PrefetchScalarGridSpec(
            num_scalar_prefetch=3,
            grid=(n_blocks,),
            in_specs=[
                pl.BlockSpec(memory_space=pl.ANY),
                pl.BlockSpec((None, None, D, de2), lambda i, rt, be, nr: (layer, be[i], 0, 0)),
                pl.BlockSpec((None, None, 1, de2), lambda i, rt, be, nr: (layer, be[i], 0, 0)),
                pl.BlockSpec((None, None, de, D), lambda i, rt, be, nr: (layer, be[i], 0, 0)),
                pl.BlockSpec((None, None, 1, D), lambda i, rt, be, nr: (layer, be[i], 0, 0)),
            ],
            out_specs=pl.BlockSpec((tb, D // 2), lambda i, rt, be, nr: (i, 0)),
            scratch_shapes=[pltpu.VMEM((2, tb, D // 2), jnp.uint32), pltpu.SemaphoreType.DMA((2,))],
        ),
        out_shape=jax.ShapeDtypeStruct((n_blocks * tb, D // 2), jnp.uint32),
        compiler_params=_gather_params(("arbitrary",), MOE_VMEM_LIMIT),
        name="moe_grouped_mm",
    )(row_tok, block_e, n_real, h, w1, b1.reshape(b1.shape[0], E, 1, de2), w2, b2.reshape(b2.shape[0], E, 1, D))


def _combine_kernel(dest_ref, ys_ref, gates_ref, x_ref, gt_ref, *refs, tok0, final_norm):
    fg_ref = refs[0] if final_norm else None
    o_ref, buf, sems = refs[-3:]
    nt = pl.num_programs(1)
    step = pl.program_id(0) * nt + pl.program_id(1)
    n_steps = pl.num_programs(0) * nt
    tm = buf.shape[2]

    def issue(st, slot):
        base = (tok0 + st * tm) * TOP_K
        for r in range(tm):
            for k in range(TOP_K):
                d = dest_ref[base + r * TOP_K + k]
                pltpu.make_async_copy(ys_ref.at[pl.ds(d, 1), :], buf.at[slot, k, pl.ds(r, 1), :], sems.at[slot]).start()

    def wait(slot):
        for k in range(TOP_K):
            pltpu.make_async_copy(ys_ref.at[pl.ds(0, tm), :], buf.at[slot, k], sems.at[slot]).wait()

    @pl.when(step == 0)
    def _():
        issue(step, 0)

    slot = step % 2
    wait(slot)
    issue(jnp.minimum(step + 1, n_steps - 1), 1 - slot)
    gates = gates_ref[...]
    acc_lo, acc_hi = 0.0, 0.0
    for k in range(TOP_K):
        lo, hi = _unpack_bf16_pair(buf[slot, k])
        acc_lo = acc_lo + gates[:, k:k + 1] * lo
        acc_hi = acc_hi + gates[:, k:k + 1] * hi
    y = x_ref[0] + gt_ref[0] * jnp.concatenate([acc_lo, acc_hi], axis=1)
    if final_norm:
        y = y * lax.rsqrt(jnp.mean(y * y, axis=-1, keepdims=True) + EPS) * fg_ref[...]
    o_ref[0] = y

    @pl.when(step == n_steps - 1)
    def _():
        wait(1 - slot)


def moe_combine(dest, ys, gates, x, gate_vec, tok0, final_g=None):
    B, T, D = x.shape
    tm = 128
    nt = T // tm
    g0 = tok0 // tm
    gmap = (lambda b, i, d: (b, 0, 0)) if gate_vec.shape[0] > 1 else (lambda b, i, d: (0, 0, 0))
    in_specs = [
        pl.BlockSpec(memory_space=pl.ANY),
        pl.BlockSpec((tm, TOP_K), lambda b, i, d: (g0 + b * nt + i, 0)),
        pl.BlockSpec((1, tm, D), lambda b, i, d: (b, i, 0)),
        pl.BlockSpec((1, 1, D), gmap),
    ]
    args = [dest, ys, gates, x, gate_vec]
    if final_g is not None:
        in_specs.append(pl.BlockSpec((1, D), lambda b, i, d: (0, 0)))
        args.append(final_g.reshape(1, D))
    return pl.pallas_call(
        functools.partial(_combine_kernel, tok0=tok0, final_norm=final_g is not None),
        grid_spec=pltpu.PrefetchScalarGridSpec(
            num_scalar_prefetch=1,
            grid=(B, nt),
            in_specs=in_specs,
            out_specs=pl.BlockSpec((1, tm, D), lambda b, i, d: (b, i, 0)),
            scratch_shapes=[pltpu.VMEM((2, TOP_K, tm, D // 2), jnp.uint32), pltpu.SemaphoreType.DMA((2,))],
        ),
        out_shape=jax.ShapeDtypeStruct((B, T, D), F32),
        compiler_params=_gather_params(("arbitrary", "arbitrary")),
        name="moe_combine",
    )(*args)


def moe_route(logits):
    N = logits.shape[0]
    tb = MOE_ROWS
    top_idx, gates = topk_gates(logits)
    rank, counts = expert_ranks(top_idx)
    counts = counts.reshape(N_EXPERTS).astype(jnp.int32)
    padded = (counts + tb - 1) // tb * tb
    pad_end = jnp.cumsum(padded)
    pad_start = pad_end - padded
    dest = pad_start[top_idx] + rank
    n_blocks = -(-(N * TOP_K) // tb) + N_EXPERTS
    n_real = (pad_end[-1] // tb).astype(jnp.int32).reshape(1)
    starts = jnp.arange(n_blocks, dtype=jnp.int32) * tb
    block_e = jnp.minimum(jnp.sum(pad_end[None, :] <= starts[:, None], axis=1), N_EXPERTS - 1).astype(jnp.int32)
    tok_of = jnp.arange(N * TOP_K, dtype=jnp.int32) // TOP_K
    row_tok = jnp.zeros((n_blocks * tb,), jnp.int32).at[dest.reshape(-1)].set(tok_of)
    return dest.reshape(-1).astype(jnp.int32), gates, row_tok, block_e, n_real, n_blocks


def moe_ffn(h, logits, layer, w1, b1, w2, b2):
    dest, gates, row_tok, block_e, n_real, n_blocks = moe_route(logits)
    ys = moe_grouped_mm(row_tok, block_e, n_real, n_blocks, h, layer, w1, b1, w2, b2)
    return dest, gates, ys


def kernel(x, c, ctx, c_ctx, w_ada, b_ada, g_mix, g_ffn, w_in, q_norm, k_norm, hy_conv_w, hy_conv_b, hy_w1, hy_b1, hy_w2, hy_b2, hy_w3, hy_freq, hy_bias, g_out, w_out, w_router, b_router, moe_w1, moe_b1, moe_w2, moe_b2, g_final):
    B, T, D = x.shape
    C = ctx.shape[1]
    depth = w_ada.shape[0]
    wq = N_HEADS * HEAD_DIM
    wkv = N_KV_HEADS * HEAD_DIM
    col_k, col_v, col_u = wq, wq + wkv, wq + 2 * wkv
    cos_t, sin_t = rope_tables(T)
    ones_c = jnp.ones((C, HEAD_DIM), F32)

    cvecs = jnp.concatenate([c, c_ctx[None, :], jnp.zeros((8 - B - 1, D), F32)], axis=0)
    ada = adaln_all(cvecs, w_ada, b_ada)

    moe_w2_b = moe_w2.astype(BF16)

    xc = ctx
    for i in range(depth):
        last = i == depth - 1
        mod = [ada[i, :B, j * D:(j + 1) * D].reshape(B, 1, D) for j in range(6)]
        cmod = [ada[i, B:B + 1, j * D:(j + 1) * D].reshape(1, 1, D) for j in range(6)]
        hp = (hy_conv_w[i], hy_conv_b[i], hy_w1[i], hy_b1[i], hy_w2[i], hy_b2[i], hy_w3[i], hy_freq[i], hy_bias[i])

        h = norm_mod(x, g_mix[i], mod[0], mod[1])
        hc = norm_mod(xc, g_mix[i], cmod[0], cmod[1])
        qkvu = matmul(h, w_in, i)
        q = head_norm(qkvu, 0, N_HEADS, q_norm[i], cos_t, sin_t, True, ATTN_SCALE)
        k = head_norm(qkvu, col_k, N_KV_HEADS, k_norm[i], cos_t, sin_t, True, 1.0)
        v = qkvu[:, :, col_v:col_u]
        hc1 = hc.reshape(1, B * C, D)
        if last:
            kvc = matmul(hc1, w_in, i, col_k, col_u - col_k).reshape(B, C, col_u - col_k)
            kc = head_norm(kvc, 0, N_KV_HEADS, k_norm[i], ones_c, ones_c, False, 1.0)
            vc = kvc[:, :, wkv:]
        else:
            qkvuc = matmul(hc1, w_in, i).reshape(B, C, -1)
            qc = head_norm(qkvuc, 0, N_HEADS, q_norm[i], ones_c, ones_c, False, ATTN_SCALE)
            kc = head_norm(qkvuc, col_k, N_KV_HEADS, k_norm[i], ones_c, ones_c, False, 1.0)
            vc = qkvuc[:, :, col_v:col_u]
        y_attn = attention(q, jnp.concatenate([kc, k], axis=1), values_with_ones(jnp.concatenate([vc, v], axis=1)))
        y_hy = hyena_mixer(qkvu, col_u, *hp)
        x = matmul_residual(mixnorm_channel_major(y_attn, y_hy, g_out[i]), w_out, i, x, mod[2])
        if not last:
            yc_attn = attention(qc, kc, values_with_ones(vc))
            yc_hy = hyena_mixer(qkvuc, col_u, *hp)
            xc = matmul_residual(mixnorm(yc_attn, yc_hy, g_out[i]).reshape(1, B * C, D), w_out, i,
                                 xc.reshape(1, B * C, D), cmod[2]).reshape(B, C, D)

        if last:
            hf, logits = norm_mod_router([x], g_ffn[i], [mod[3]], [mod[4]], w_router[i], b_router[i])
        else:
            hf, logits = norm_mod_router([x, xc], g_ffn[i], [mod[3], cmod[3]], [mod[4], cmod[4]],
                                         w_router[i], b_router[i])
        dest, gates, ys = moe_ffn(hf, logits, i, moe_w1, moe_b1, moe_w2_b, moe_b2)
        x = moe_combine(dest, ys, gates, x, mod[5], 0, g_final if last else None)
        if not last:
            xc = moe_combine(dest, ys, gates, xc, cmod[5], B * T)
    return x
```

```python
import functools
import math

import numpy as np
import jax
import jax.numpy as jnp
from jax import lax
from jax.experimental import pallas as pl
from jax.experimental.pallas import tpu as pltpu

F32 = jnp.float32
BF16 = jnp.bfloat16

GRID_W = 64
HEAD_DIM = 128
N_HEADS = 16
N_KV_HEADS = 4
GROUP = N_HEADS // N_KV_HEADS
HYENA_ORDER = 2
FILTER_BANDS = 16
FILTER_EMB = 1 + 2 * FILTER_BANDS
FILTER_HIDDEN = 64
DECAY_TARGET = 1e-2
FAST_DECAY_PCT = 0.3
SLOW_DECAY_PCT = 1.5
ROPE_THETA = 10000.0
N_EXPERTS = 32
TOP_K = 4
SWIGLU_LIMIT = 7.0
SWIGLU_ALPHA = 1.702
EPS = 1e-6

LANES = 128
V7X_VMEM_BYTES = 64 * 1024 * 1024
VMEM_LIMIT = V7X_VMEM_BYTES - 8 * 1024 * 1024
FFT_TA = 32
MOE_ROWS = 256
MOE_KCHUNK = 1024
MOE_AHEAD = 2


def _params(sem):
    return pltpu.CompilerParams(dimension_semantics=sem, vmem_limit_bytes=VMEM_LIMIT)


def _pack_bf16_pair(x):
    n = x.shape[1] // 2
    return _pack_bf16_words(x[:, :n], x[:, n:])


def _pack_bf16_words(lo, hi):
    def rne(x):
        b = lax.bitcast_convert_type(x, jnp.uint32)
        return b + jnp.uint32(0x7FFF) + ((b >> 16) & jnp.uint32(1))

    return (rne(lo) >> 16) | (rne(hi) & jnp.uint32(0xFFFF0000))


def _unpack_bf16_pair(w):
    lo = lax.bitcast_convert_type(w << 16, F32)
    hi = lax.bitcast_convert_type(w & jnp.uint32(0xFFFF0000), F32)
    return lo, hi


def _adaln_kernel(c_ref, w_ref, b_ref, o_ref):
    c = c_ref[...]
    s = (c * jax.nn.sigmoid(c)).astype(BF16)
    o_ref[0] = jnp.dot(s, w_ref[0].astype(BF16), preferred_element_type=F32) + b_ref[0]


def adaln_all(cvecs, w_ada, b_ada):
    L, D, N = w_ada.shape
    tn = 512
    return pl.pallas_call(
        _adaln_kernel,
        grid=(L, N // tn),
        in_specs=[
            pl.BlockSpec((8, D), lambda l, j: (0, 0)),
            pl.BlockSpec((1, D, tn), lambda l, j: (l, 0, j)),
            pl.BlockSpec((1, 1, tn), lambda l, j: (l, 0, j)),
        ],
        out_specs=pl.BlockSpec((1, 8, tn), lambda l, j: (l, 0, j)),
        out_shape=jax.ShapeDtypeStruct((L, 8, N), F32),
        compiler_params=_params(("arbitrary", "arbitrary")),
        name="adaln",
    )(cvecs, w_ada, b_ada.reshape(L, 1, N))


def _norm_mod_kernel(x_ref, g_ref, sh_ref, sc_ref, h_ref):
    x = x_ref[0]
    y = x * lax.rsqrt(jnp.mean(x * x, axis=-1, keepdims=True) + EPS) * g_ref[...]
    h_ref[0] = (y * (1.0 + sc_ref[0]) + sh_ref[0]).astype(h_ref.dtype)


def _norm_mod_router_kernel(*refs, n_first):
    xs = refs[:-7]
    g_ref, sh_ref, sc_ref, wr_ref, br_ref, h_ref, l_ref = refs[-7:]

    def run(x_ref):
        x = x_ref[...]
        y = x * lax.rsqrt(jnp.mean(x * x, axis=-1, keepdims=True) + EPS) * g_ref[...]
        h = y * (1.0 + sc_ref[0]) + sh_ref[0]
        h_ref[...] = _pack_bf16_pair(h)
        e = l_ref.shape[1]
        h_hi = h.astype(BF16)
        h_lo = (h - h_hi.astype(F32)).astype(BF16)
        a = jnp.dot(h_hi, wr_ref[...], preferred_element_type=F32)
        b = jnp.dot(h_lo, wr_ref[:, :e], preferred_element_type=F32)
        l_ref[...] = a[:, :e] + a[:, e:] + b + br_ref[...]

    if len(xs) == 1:
        run(xs[0])
    else:
        i = pl.program_id(0)
        pl.when(i < n_first)(lambda: run(xs[0]))
        pl.when(i >= n_first)(lambda: run(xs[1]))


def _mod_map(bm):
    return (lambda b, i: (b, 0, 0)) if bm > 1 else (lambda b, i: (0, 0, 0))


def norm_mod(x, g, shift, scale, out_dtype=BF16):
    B, T, D = x.shape
    tm = min(T, 256)
    return pl.pallas_call(
        _norm_mod_kernel,
        grid=(B, T // tm),
        in_specs=[
            pl.BlockSpec((1, tm, D), lambda b, i: (b, i, 0)),
            pl.BlockSpec((1, D), lambda b, i: (0, 0)),
            pl.BlockSpec((1, 1, D), _mod_map(shift.shape[0])),
            pl.BlockSpec((1, 1, D), _mod_map(scale.shape[0])),
        ],
        out_specs=pl.BlockSpec((1, tm, D), lambda b, i: (b, i, 0)),
        out_shape=jax.ShapeDtypeStruct((B, T, D), out_dtype),
        compiler_params=_params(("arbitrary", "arbitrary")),
        name="norm_mod",
    )(x, g.reshape(1, D), shift, scale)


def norm_mod_router(xs, g, shifts, scales, w_router, b_router):
    D = xs[0].shape[-1]
    E = w_router.shape[1]
    tm = 256
    sizes = [a.shape[0] * a.shape[1] for a in xs]
    n_first = sizes[0] // tm
    n_tiles = sum(sizes) // tm
    assert all(a.shape[1] % tm == 0 for a in xs) and len(xs) <= 2

    def mod_rows(ms):
        return jnp.concatenate([jnp.broadcast_to(m, (a.shape[0], 1, D)) for m, a in zip(ms, xs)], axis=0)

    per0 = xs[0].shape[1] // tm

    def tile_row(i):
        if len(xs) == 1:
            return i // per0
        per1 = xs[1].shape[1] // tm
        return jnp.where(i < n_first, i // per0, xs[0].shape[0] + (i - n_first) // per1)

    x_specs = [pl.BlockSpec((tm, D), lambda i: (jnp.minimum(i, n_first - 1), 0))]
    if len(xs) == 2:
        x_specs.append(pl.BlockSpec((tm, D), lambda i: (jnp.maximum(i - n_first, 0), 0)))
    w_hi = w_router.astype(BF16)
    w_hilo = jnp.concatenate([w_hi, (w_router - w_hi.astype(F32)).astype(BF16)], axis=1)
    return pl.pallas_call(
        functools.partial(_norm_mod_router_kernel, n_first=n_first),
        grid=(n_tiles,),
        in_specs=x_specs + [
            pl.BlockSpec((1, D), lambda i: (0, 0)),
            pl.BlockSpec((1, 1, D), lambda i: (tile_row(i), 0, 0)),
            pl.BlockSpec((1, 1, D), lambda i: (tile_row(i), 0, 0)),
            pl.BlockSpec((D, 2 * E), lambda i: (0, 0)),
            pl.BlockSpec((1, E), lambda i: (0, 0)),
        ],
        out_specs=[pl.BlockSpec((tm, D // 2), lambda i: (i, 0)), pl.BlockSpec((tm, E), lambda i: (i, 0))],
        out_shape=[jax.ShapeDtypeStruct((n_tiles * tm, D // 2), jnp.uint32),
                   jax.ShapeDtypeStruct((n_tiles * tm, E), F32)],
        compiler_params=_params(("arbitrary",)),
        name="norm_mod_router",
    )(*[a.reshape(-1, D) for a in xs], g.reshape(1, D), mod_rows(shifts), mod_rows(scales),
      w_hilo, b_router.reshape(1, E))


def _mixnorm_kernel(a_ref, hy_ref, g_ref, o_ref):
    wa = a_ref.shape[-1]
    a = a_ref[0].astype(F32)
    b = hy_ref[0]
    g = g_ref[...]
    o_ref[0, :, :wa] = (a * lax.rsqrt(jnp.mean(a * a, axis=-1, keepdims=True) + EPS) * g[:, :wa]).astype(o_ref.dtype)
    o_ref[0, :, wa:] = (b * lax.rsqrt(jnp.mean(b * b, axis=-1, keepdims=True) + EPS) * g[:, wa:]).astype(o_ref.dtype)


def mixnorm(y_attn, y_hy, g):
    B, T, wa = y_attn.shape
    wh = y_hy.shape[-1]
    tm = min(T, 256)
    return pl.pallas_call(
        _mixnorm_kernel,
        grid=(B, T // tm),
        in_specs=[
            pl.BlockSpec((1, tm, wa), lambda b, i: (b, i, 0)),
            pl.BlockSpec((1, tm, wh), lambda b, i: (b, i, 0)),
            pl.BlockSpec((1, wa + wh), lambda b, i: (0, 0)),
        ],
        out_specs=pl.BlockSpec((1, tm, wa + wh), lambda b, i: (b, i, 0)),
        out_shape=jax.ShapeDtypeStruct((B, T, wa + wh), BF16),
        compiler_params=_params(("arbitrary", "arbitrary")),
        name="mixnorm",
    )(y_attn, y_hy, g.reshape(1, wa + wh))


def _mixnorm_cm_kernel(a_ref, hy_ref, g_ref, o_ref):
    wa = a_ref.shape[-1]
    g = g_ref[...]
    a = a_ref[0].astype(F32)
    o_ref[0, :, :wa] = (a * lax.rsqrt(jnp.mean(a * a, axis=-1, keepdims=True) + EPS) * g[:, :wa]).astype(o_ref.dtype)
    for j in range(hy_ref.shape[2]):
        b = hy_ref[0, :, j, :].T
        y = b * lax.rsqrt(jnp.mean(b * b, axis=-1, keepdims=True) + EPS) * g[:, wa:]
        o_ref[0, j * LANES:(j + 1) * LANES, wa:] = y.astype(o_ref.dtype)


def mixnorm_channel_major(y_attn, y_hy, g):
    B, T, wa = y_attn.shape
    wh, nta = y_hy.shape[1], y_hy.shape[2]
    jb = 8
    tm = jb * LANES
    assert nta * LANES == T and nta % jb == 0
    return pl.pallas_call(
        _mixnorm_cm_kernel,
        grid=(B, T // tm),
        in_specs=[
            pl.BlockSpec((1, tm, wa), lambda b, i: (b, i, 0)),
            pl.BlockSpec((1, wh, jb, LANES), lambda b, i: (b, 0, i, 0)),
            pl.BlockSpec((1, wa + wh), lambda b, i: (0, 0)),
        ],
        out_specs=pl.BlockSpec((1, tm, wa + wh), lambda b, i: (b, i, 0)),
        out_shape=jax.ShapeDtypeStruct((B, T, wa + wh), BF16),
        compiler_params=_params(("arbitrary", "arbitrary")),
        name="mixnorm_cm",
    )(y_attn, y_hy, g.reshape(1, wa + wh))


def _cast_weight_once(w_ref, wb_ref):
    @pl.when((pl.program_id(1) == 0) & (pl.program_id(2) == 0))
    def _():
        wb_ref[...] = w_ref[...].astype(BF16)


def _mm_kernel(x_ref, w_ref, o_ref, wb_ref):
    _cast_weight_once(w_ref, wb_ref)
    o_ref[0] = jnp.dot(x_ref[0], wb_ref[...], preferred_element_type=F32).astype(o_ref.dtype)


def _mm_res_kernel(x_ref, w_ref, r_ref, g_ref, o_ref, wb_ref):
    _cast_weight_once(w_ref, wb_ref)
    acc = jnp.dot(x_ref[0], wb_ref[...], preferred_element_type=F32)
    o_ref[0] = r_ref[0] + g_ref[0] * acc


MM_TN = 512


def matmul(x, w, layer, col0=0, n_out=None, out_dtype=F32):
    B, T, K = x.shape
    N = n_out or w.shape[2]
    tm, tn = min(T, 1024), MM_TN
    c0 = col0 // tn
    assert c0 * tn == col0 and N % tn == 0
    return pl.pallas_call(
        _mm_kernel,
        grid=(N // tn, B, T // tm),
        in_specs=[
            pl.BlockSpec((1, tm, K), lambda j, b, i: (b, i, 0)),
            pl.BlockSpec((None, K, tn), lambda j, b, i: (layer, 0, c0 + j)),
        ],
        out_specs=pl.BlockSpec((1, tm, tn), lambda j, b, i: (b, i, j)),
        out_shape=jax.ShapeDtypeStruct((B, T, N), out_dtype),
        scratch_shapes=[pltpu.VMEM((K, tn), BF16)],
        compiler_params=_params(("arbitrary", "arbitrary", "arbitrary")),
        name="matmul",
    )(x, w)


def matmul_residual(x, w, layer, res, gate):
    B, T, K = x.shape
    N = w.shape[2]
    tm, tn = min(T, 1024), MM_TN
    gmap = (lambda j, b, i: (b, 0, j)) if gate.shape[0] > 1 else (lambda j, b, i: (0, 0, j))
    return pl.pallas_call(
        _mm_res_kernel,
        grid=(N // tn, B, T // tm),
        in_specs=[
            pl.BlockSpec((1, tm, K), lambda j, b, i: (b, i, 0)),
            pl.BlockSpec((None, K, tn), lambda j, b, i: (layer, 0, j)),
            pl.BlockSpec((1, tm, tn), lambda j, b, i: (b, i, j)),
            pl.BlockSpec((1, 1, tn), gmap),
        ],
        out_specs=pl.BlockSpec((1, tm, tn), lambda j, b, i: (b, i, j)),
        out_shape=jax.ShapeDtypeStruct((B, T, N), F32),
        scratch_shapes=[pltpu.VMEM((K, tn), BF16)],
        compiler_params=_params(("arbitrary", "arbitrary", "arbitrary")),
        name="matmul_residual",
    )(x, w, res, gate)


def _head_norm_kernel(x_ref, g_ref, cos_ref, sin_ref, o_ref, *, n_heads, rope, scale):
    g = g_ref[...]
    if rope:
        cs = cos_ref[...]
        sn = sin_ref[...]
        lane = lax.broadcasted_iota(jnp.int32, cs.shape, 1)
        first = (lane % 64) < 32
    for h in range(n_heads):
        xh = x_ref[0, :, h * HEAD_DIM:(h + 1) * HEAD_DIM]
        y = xh * lax.rsqrt(jnp.mean(xh * xh, axis=-1, keepdims=True) + EPS) * g
        if rope:
            swapped = jnp.where(first, pltpu.roll(y, 96, 1), pltpu.roll(y, 32, 1))
            y = y * cs + swapped * sn
        o_ref[0, :, h * HEAD_DIM:(h + 1) * HEAD_DIM] = (y * scale).astype(o_ref.dtype)


def head_norm(x, col0, n_heads, g, cos_t, sin_t, rope, scale):
    B, T, _ = x.shape
    w = n_heads * HEAD_DIM
    tm = min(T, 256)
    cb = col0 // w
    assert cb * w == col0
    kern = functools.partial(_head_norm_kernel, n_heads=n_heads, rope=rope, scale=scale)
    return pl.pallas_call(
        kern,
        grid=(B, T // tm),
        in_specs=[
            pl.BlockSpec((1, tm, w), lambda b, i: (b, i, cb)),
            pl.BlockSpec((1, HEAD_DIM), lambda b, i: (0, 0)),
            pl.BlockSpec((tm, HEAD_DIM), lambda b, i: (i, 0)),
            pl.BlockSpec((tm, HEAD_DIM), lambda b, i: (i, 0)),
        ],
        out_specs=pl.BlockSpec((1, tm, w), lambda b, i: (b, i, 0)),
        out_shape=jax.ShapeDtypeStruct((B, T, w), BF16),
        compiler_params=_params(("arbitrary", "arbitrary")),
        name="head_norm",
    )(x, g.reshape(1, HEAD_DIM), cos_t, sin_t)


def rope_tables(T):
    pos = np.arange(T)
    r, col = pos // GRID_W, pos % GRID_W
    n_freq = HEAD_DIM // 4
    inv = ROPE_THETA ** (-np.arange(n_freq, dtype=np.float64) / n_freq)
    ar, ac = r[:, None] * inv, col[:, None] * inv
    cos_t = np.concatenate([np.cos(ar), np.cos(ar), np.cos(ac), np.cos(ac)], axis=1)
    sin_t = np.concatenate([-np.sin(ar), np.sin(ar), -np.sin(ac), np.sin(ac)], axis=1)
    return jnp.asarray(cos_t, F32), jnp.asarray(sin_t, F32)


ATTN_TK = 512
ATTN_SCALE = HEAD_DIM ** -0.5 * math.log2(math.e)


def _attn_kernel(q_ref, k_ref, v_ref, o_ref):
    tq = q_ref.shape[1]
    S = k_ref.shape[1]
    q = q_ref[0]
    qs = jnp.concatenate([q[:, j * HEAD_DIM:(j + 1) * HEAD_DIM] for j in range(GROUP)], axis=0)
    rows = GROUP * tq
    m = jnp.full((rows, 1), -jnp.inf, F32)
    acc = jnp.zeros((rows, 2 * HEAD_DIM), F32)
    for k0 in range(0, S, ATTN_TK):
        ks = slice(k0, min(k0 + ATTN_TK, S))
        s = lax.dot_general(qs, k_ref[0, ks, :], (((1,), (1,)), ((), ())), preferred_element_type=F32)
        m_new = jnp.maximum(m, jnp.max(s, axis=-1, keepdims=True))
        p = jnp.exp2(s - m_new)
        acc = jnp.exp2(m - m_new) * acc + jnp.dot(p.astype(BF16), v_ref[0, ks, :], preferred_element_type=F32)
        m = m_new
    o = acc[:, :HEAD_DIM] / acc[:, HEAD_DIM:HEAD_DIM + 1]
    for j in range(GROUP):
        o_ref[0, :, j * HEAD_DIM:(j + 1) * HEAD_DIM] = o[j * tq:(j + 1) * tq, :].astype(o_ref.dtype)


def values_with_ones(v):
    B, S, _ = v.shape
    v4 = v.reshape(B, S, N_KV_HEADS, HEAD_DIM).astype(BF16)
    pad = jnp.zeros((B, S, N_KV_HEADS, HEAD_DIM), BF16).at[..., 0].set(1.0)
    return jnp.concatenate([v4, pad], axis=-1).reshape(B, S, N_KV_HEADS * 2 * HEAD_DIM)


def attention(q, k, v1):
    B, T, _ = q.shape
    S = k.shape[1]
    assert S % 128 == 0
    tq = min(T, 256)
    gw = GROUP * HEAD_DIM
    return pl.pallas_call(
        _attn_kernel,
        grid=(B, N_KV_HEADS, T // tq),
        in_specs=[
            pl.BlockSpec((1, tq, gw), lambda b, g, i: (b, i, g)),
            pl.BlockSpec((1, S, HEAD_DIM), lambda b, g, i: (b, 0, g)),
            pl.BlockSpec((1, S, 2 * HEAD_DIM), lambda b, g, i: (b, 0, g)),
        ],
        out_specs=pl.BlockSpec((1, tq, gw), lambda b, g, i: (b, i, g)),
        out_shape=jax.ShapeDtypeStruct((B, T, N_HEADS * HEAD_DIM), BF16),
        compiler_params=_params(("arbitrary", "arbitrary", "arbitrary")),
        name="attention",
    )(q, k, v1)


PITCH_PAD = 4


def _store_channel_major(o_ref, chunk, n_chan, nta, stage):
    pitch = n_chan + PITCH_PAD
    for ta in range(nta):
        stage[ta * pitch:ta * pitch + n_chan, :] = chunk(ta)
    for c in range(n_chan):
        o_ref[0, c * nta:(c + 1) * nta, :] = stage[pl.ds(c, nta, stride=pitch), :]


def _short_conv_kernel(u_ref, w_ref, b_ref, o_ref, *scratch, channel_major):
    u = u_ref[0]
    L = u.shape[0]
    row = lax.broadcasted_iota(jnp.int32, u.shape, 0)
    prev = jnp.where(row == 0, 0.0, pltpu.roll(u, 1, 0))
    nxt = jnp.where(row == L - 1, 0.0, pltpu.roll(u, L - 1, 0))
    w = w_ref[...]
    y = prev * w[0:1, :] + u * w[1:2, :] + nxt * w[2:3, :] + b_ref[...]
    if not channel_major:
        o_ref[0] = y
    else:
        _store_channel_major(o_ref, lambda ta: y[ta * LANES:(ta + 1) * LANES, :].T, y.shape[1], L // LANES, scratch[0])


def short_conv(x, col0, width, w, b, channel_major):
    B, L, _ = x.shape
    cb = 256 if L > 1024 else 512
    assert col0 % cb == 0 and width % cb == 0
    off = col0 // cb
    scratch = []
    if channel_major:
        nta = L // LANES
        out_spec = pl.BlockSpec((1, cb * nta, LANES), lambda bi, j: (bi, j, 0))
        out_shape = jax.ShapeDtypeStruct((B, width * nta, LANES), F32)
        scratch = [pltpu.VMEM((nta * (cb + PITCH_PAD), LANES), F32)]
    else:
        out_spec = pl.BlockSpec((1, L, cb), lambda bi, j: (bi, 0, j))
        out_shape = jax.ShapeDtypeStruct((B, L, width), F32)
    return pl.pallas_call(
        functools.partial(_short_conv_kernel, channel_major=channel_major),
        grid=(B, width // cb),
        in_specs=[
            pl.BlockSpec((1, L, cb), lambda bi, j: (bi, 0, off + j)),
            pl.BlockSpec((3, cb), lambda bi, j: (0, j)),
            pl.BlockSpec((1, cb), lambda bi, j: (0, j)),
        ],
        out_specs=out_spec,
        out_shape=out_shape,
        scratch_shapes=scratch,
        compiler_params=_params(("arbitrary", "arbitrary")),
        name="short_conv",
    )(x, w, b.reshape(1, width))


def _filter_hidden_kernel(f_ref, w1_ref, b1_ref, w2_ref, b2_ref, fr_ref, o_ref):
    hp = lax.Precision.HIGHEST
    fr = fr_ref[...]
    h1 = jnp.sin(fr * (jnp.dot(w1_ref[...], f_ref[...], preferred_element_type=F32, precision=hp) + b1_ref[...]))
    o_ref[...] = jnp.sin(fr * (jnp.dot(w2_ref[...], h1, preferred_element_type=F32, precision=hp) + b2_ref[...]))


def filter_hidden(feats_t, w1, b1, w2, b2, freq):
    H = FILTER_HIDDEN
    P = feats_t.shape[1]
    fe = feats_t.shape[0]
    w1t = jnp.zeros((H, fe), F32).at[:, :FILTER_EMB].set(w1.T)
    return pl.pallas_call(
        _filter_hidden_kernel,
        out_shape=jax.ShapeDtypeStruct((H, P), F32),
        compiler_params=pltpu.CompilerParams(vmem_limit_bytes=VMEM_LIMIT),
        name="filter_hidden",
    )(feats_t, w1t, b1.reshape(H, 1), w2.T, b2.reshape(H, 1), freq.reshape(H, 1))


def _filter_k_kernel(w3_ref, hid_ref, tpos_ref, mask_ref, e0_ref, dl_ref, bias_ref, o_ref, *scratch, channel_major):
    w = w3_ref[0]
    w_hi = w.astype(BF16)
    w_lo = (w - w_hi.astype(F32)).astype(BF16)
    hd = hid_ref[0]
    h_hi = hd.astype(BF16)
    h_lo = (hd - h_hi.astype(F32)).astype(BF16)
    h = jnp.dot(jnp.concatenate([w_hi, w_lo, w_hi], axis=1), jnp.concatenate([h_hi, h_hi, h_lo], axis=0),
                preferred_element_type=F32)
    decay = jnp.exp(-tpos_ref[0] * dl_ref[...])
    k = mask_ref[0] * decay * h + bias_ref[0] * e0_ref[...]
    if not channel_major:
        o_ref[0] = k
    else:
        rc, L = k.shape
        _store_channel_major(o_ref, lambda ta: k[:, ta * LANES:(ta + 1) * LANES], rc, L // LANES, scratch[0])


def filter_k(w3t, hid, tpos, mask, e0, deltas, bias_aug, channel_major):
    OD, C, H = w3t.shape
    L = hid.shape[-1]
    rc = 256
    scratch = []
    if channel_major:
        nta = L // LANES
        out_spec = pl.BlockSpec((1, rc * nta, LANES), lambda od, j: (od, j, 0))
        out_shape = jax.ShapeDtypeStruct((OD, C * nta, LANES), F32)
        scratch = [pltpu.VMEM((nta * (rc + PITCH_PAD), LANES), F32)]
    else:
        out_spec = pl.BlockSpec((1, rc, L), lambda od, j: (od, j, 0))
        out_shape = jax.ShapeDtypeStruct((OD, C, L), F32)
    return pl.pallas_call(
        functools.partial(_filter_k_kernel, channel_major=channel_major),
        grid=(OD, C // rc),
        in_specs=[
            pl.BlockSpec((1, rc, H), lambda od, j: (od, j, 0)),
            pl.BlockSpec((1, H, L), lambda od, j: (od % 2, 0, 0)),
            pl.BlockSpec((1, 1, L), lambda od, j: (od % 2, 0, 0)),
            pl.BlockSpec((1, 1, L), lambda od, j: (od % 2, 0, 0)),
            pl.BlockSpec((1, L), lambda od, j: (0, 0)),
            pl.BlockSpec((rc, 1), lambda od, j: (j, 0)),
            pl.BlockSpec((1, rc, 1), lambda od, j: (od, j, 0)),
        ],
        out_specs=out_spec,
        out_shape=out_shape,
        scratch_shapes=scratch,
        compiler_params=_params(("arbitrary", "arbitrary")),
        name="filter_k",
    )(w3t, hid, tpos, mask, e0, deltas, bias_aug)


def hyena_filter_taps(L, C, w1, b1, w2, b2, w3, freq, bias, channel_major):
    pos = np.arange(L, dtype=np.float64)
    posr = np.where(pos == 0, 0.0, L - pos)

    def feats(p):
        t = p / max(L - 1, 1)
        bands = np.linspace(1e-4, FILTER_BANDS - 1, FILTER_BANDS)
        ang = (2 * math.pi / L) * p[:, None] * bands
        return np.concatenate([t[:, None], np.cos(ang), np.sin(ang)], axis=-1), t

    f0, t0 = feats(pos)
    f1, t1 = feats(posr)
    fe = 40
    ft = np.zeros((fe, 2 * L))
    ft[:FILTER_EMB, :L] = f0.T
    ft[:FILTER_EMB, L:] = f1.T
    hid = filter_hidden(jnp.asarray(ft, F32), w1, b1, w2, b2, freq)
    hid = jnp.stack([hid[:, :L], hid[:, L:]], axis=0)
    tpos = jnp.asarray(np.stack([t0, t1])[:, None, :], F32)
    mask = jnp.asarray(np.stack([np.ones(L), (pos > 0).astype(np.float64)])[:, None, :], F32)
    e0 = jnp.asarray((pos == 0).astype(np.float64)[None, :], F32)
    max_decay = math.log(DECAY_TARGET) / FAST_DECAY_PCT
    min_decay = math.log(DECAY_TARGET) / SLOW_DECAY_PCT
    deltas = jnp.asarray(np.abs(np.linspace(min_decay, max_decay, C))[:, None], F32)
    w3t = w3.T.reshape(HYENA_ORDER * 2, C, FILTER_HIDDEN)
    bias_aug = jnp.stack([bias, jnp.zeros_like(bias)], axis=1).reshape(HYENA_ORDER * 2, C, 1)
    k = filter_k(w3t, hid, tpos, mask, e0, deltas, bias_aug, channel_major)
    return k.reshape((HYENA_ORDER, 2) + k.shape[1:])


def _fft_consts():
    def emb(fr, fi):
        return np.block([[fr, fi], [-fi, fr]])

    n2 = LANES
    a = 2 * np.pi * np.outer(np.arange(n2), np.arange(n2)) / n2
    g128f = emb(np.cos(a), -np.sin(a))
    g128i = emb(np.cos(a), np.sin(a)) / (2.0 * FFT_TA * LANES)
    ta = np.arange(FFT_TA)
    eye = np.eye(LANES // FFT_TA)
    be = 2 * np.pi * np.outer(ta, ta) / FFT_TA
    bo_f = be + 2 * np.pi * ta[:, None] / (2 * FFT_TA)
    bo_i = be + 2 * np.pi * ta[None, :] / (2 * FFT_TA)
    g32 = [emb(np.kron(eye, np.cos(be)), np.kron(eye, -np.sin(be))), emb(np.kron(eye, np.cos(be)), np.kron(eye, np.sin(be))),
           emb(np.kron(eye, np.cos(bo_f)), np.kron(eye, -np.sin(bo_f))), emb(np.kron(eye, np.cos(bo_i)), np.kron(eye, np.sin(bo_i)))]
    mats = jnp.asarray(np.stack([g128f, g128i] + g32), BF16)
    n = FFT_TA * LANES
    fa = np.tile(ta, LANES // FFT_TA)[:, None]
    tb = np.arange(LANES)[None, :]
    th_e = 2 * np.pi * fa * tb / n
    th_o = 2 * np.pi * (2 * fa + 1) * tb / (2 * n)
    tw = jnp.asarray(np.stack([np.cos(th_e), np.sin(th_e), np.cos(th_o), np.sin(th_o)]), F32)
    return mats, tw


def _rows_to_lanes(re, im):
    parts = []
    for g in range(re.shape[0] // LANES):
        sl = slice(g * LANES, (g + 1) * LANES)
        parts.append(re[sl, :].T if im is None else jnp.concatenate([re[sl, :].T, im[sl, :].T], axis=1))
    return jnp.concatenate(parts, axis=0).astype(BF16)


def _dft_fwd(lhs, g32, g128, twc, tws):
    gm = g32 if lhs.shape[1] == 2 * LANES else g32[:LANES, :]
    o1 = jnp.dot(lhs, gm, preferred_element_type=F32)
    parts = []
    for g in range(lhs.shape[0] // LANES):
        sl = slice(g * LANES, (g + 1) * LANES)
        r = o1[sl, :LANES].T
        i = o1[sl, LANES:].T
        parts.append(jnp.concatenate([r * twc + i * tws, i * twc - r * tws], axis=1))
    o2 = jnp.dot(jnp.concatenate(parts, axis=0).astype(BF16), g128, preferred_element_type=F32)
    return o2[:, :LANES], o2[:, LANES:]


def _inv_stage1(re, im, g128, twc, tws):
    o1 = jnp.dot(jnp.concatenate([re, im], axis=1).astype(BF16), g128, preferred_element_type=F32)
    parts = []
    for g in range(re.shape[0] // LANES):
        sl = slice(g * LANES, (g + 1) * LANES)
        r = o1[sl, :LANES]
        i = o1[sl, LANES:]
        parts.append(jnp.concatenate([(r * twc - i * tws).T, (i * twc + r * tws).T], axis=1))
    return jnp.concatenate(parts, axis=0).astype(BF16)


def _lanes_to_rows(o2):
    rs, is_ = [], []
    for g in range(o2.shape[0] // LANES):
        sl = slice(g * LANES, (g + 1) * LANES)
        rs.append(o2[sl, :LANES].T)
        is_.append(o2[sl, LANES:].T)
    return jnp.concatenate(rs, axis=0), jnp.concatenate(is_, axis=0)


def _cmul(ar, ai, br, bi):
    return ar * br - ai * bi, ar * bi + ai * br


def _hyena_long_kernel(vr_ref, vi_ref, x1r_ref, x1i_ref, x2r_ref, x2i_ref,
                       k0lo_ref, k0hi_ref, k1lo_ref, k1hi_ref, mats_ref, tw_ref, o_ref):
    g128f, g128i = mats_ref[0], mats_ref[1]
    g32e_f, g32e_i, g32o_f, g32o_i = mats_ref[2], mats_ref[3], mats_ref[4], mats_ref[5]
    tce, tse, tco, tso = tw_ref[0], tw_ref[1], tw_ref[2], tw_ref[3]

    def conv(zr, zi, klo, khi):
        ke = _dft_fwd(_rows_to_lanes(klo + khi, None), g32e_f, g128f, tce, tse)
        ko = _dft_fwd(_rows_to_lanes(klo - khi, None), g32o_f, g128f, tco, tso)
        z_t = _rows_to_lanes(zr, zi)
        ze = _dft_fwd(z_t, g32e_f, g128f, tce, tse)
        zo = _dft_fwd(z_t, g32o_f, g128f, tco, tso)
        ye = _inv_stage1(*_cmul(*ze, *ke), g128i, tce, tse)
        yo = _inv_stage1(*_cmul(*zo, *ko), g128i, tco, tso)
        return _lanes_to_rows(jnp.dot(ye, g32e_i, preferred_element_type=F32)
                              + jnp.dot(yo, g32o_i, preferred_element_type=F32))

    yr, yi = conv(vr_ref[0, 0], vi_ref[0, 0], k0lo_ref[0, 0], k0hi_ref[0, 0])
    z1r = x1r_ref[0, 0] * yr
    z1i = x1i_ref[0, 0] * yi
    yr, yi = conv(z1r, z1i, k1lo_ref[0, 0], k1hi_ref[0, 0])
    o_ref[0] = x2r_ref[0, 0] * yr
    o_ref[1] = x2i_ref[0, 0] * yi


def hyena_long(ut, kt):
    _, _, R, _ = ut.shape
    rb = 32 * FFT_TA
    mats, tw = _fft_consts()

    def uspec(b, p):
        return pl.BlockSpec((1, 1, rb, LANES), lambda i: (b, p, i, 0))

    def kspec(o, d):
        return pl.BlockSpec((1, 1, rb, LANES), lambda i: (o, d, i, 0))

    return pl.pallas_call(
        _hyena_long_kernel,
        grid=(R // rb,),
        in_specs=[uspec(0, 0), uspec(1, 0), uspec(0, 1), uspec(1, 1), uspec(0, 2), uspec(1, 2),
                  kspec(0, 0), kspec(0, 1), kspec(1, 0), kspec(1, 1),
                  pl.BlockSpec(mats.shape, lambda i: (0, 0, 0)),
                  pl.BlockSpec(tw.shape, lambda i: (0, 0, 0))],
        out_specs=pl.BlockSpec((2, rb, LANES), lambda i: (0, i, 0)),
        out_shape=jax.ShapeDtypeStruct((2, R, LANES), F32),
        compiler_params=_params(("arbitrary",)),
        name="hyena_long",
    )(ut, ut, ut, ut, ut, ut, kt, kt, kt, kt, mats, tw)


def _hyena_short_kernel(vr_ref, vi_ref, x1r_ref, x1i_ref, x2r_ref, x2i_ref,
                        k0lo_ref, k0hi_ref, k1lo_ref, k1hi_ref, gf_ref, gi_ref, or_ref, oi_ref):
    L = vr_ref.shape[-1]
    n = 2 * L

    def conv(zr, zi, klo, khi):
        kk = jnp.concatenate([klo, khi], axis=1).astype(BF16)
        ks = jnp.dot(kk, gf_ref[:n, :], preferred_element_type=F32)
        kr, ki = ks[:, :n], ks[:, n:]
        zz = jnp.concatenate([zr, zi], axis=1).astype(BF16)
        zs = jnp.dot(zz, gf_ref[n:, :], preferred_element_type=F32)
        sr, si = zs[:, :n], zs[:, n:]
        pr = sr * kr - si * ki
        pi = sr * ki + si * kr
        y = jnp.dot(jnp.concatenate([pr, pi], axis=1).astype(BF16), gi_ref[...], preferred_element_type=F32)
        return y[:, :L], y[:, L:]

    yr, yi = conv(vr_ref[0, 0], vi_ref[0, 0], k0lo_ref[0, 0], k0hi_ref[0, 0])
    z1r = x1r_ref[0, 0] * yr
    z1i = x1i_ref[0, 0] * yi
    yr, yi = conv(z1r, z1i, k1lo_ref[0, 0], k1hi_ref[0, 0])
    or_ref[0] = x2r_ref[0, 0] * yr
    oi_ref[0] = x2i_ref[0, 0] * yi


def hyena_short(ut, kt):
    _, _, C, L = ut.shape
    n = 2 * L
    t = np.arange(n)
    a = 2 * np.pi * np.outer(t, t) / n
    co, si = np.cos(a), np.sin(a)
    gf = np.concatenate([
        np.concatenate([co, -si], axis=1),
        np.concatenate([co[:L], -si[:L]], axis=1),
        np.concatenate([si[:L], co[:L]], axis=1)], axis=0)
    gi = np.concatenate([
        np.concatenate([co[:, :L], si[:, :L]], axis=1),
        np.concatenate([-si[:, :L], co[:, :L]], axis=1)], axis=0) / n
    rc = 256

    def uspec(b, p):
        return pl.BlockSpec((1, 1, rc, L), lambda i: (b, p, i, 0))

    outr, outi = pl.pallas_call(
        _hyena_short_kernel,
        grid=(C // rc,),
        in_specs=[uspec(0, 0), uspec(1, 0), uspec(0, 1), uspec(1, 1), uspec(0, 2), uspec(1, 2),
                  uspec(0, 0), uspec(0, 1), uspec(1, 0), uspec(1, 1),
                  pl.BlockSpec((2 * n, 2 * n), lambda i: (0, 0)),
                  pl.BlockSpec((2 * n, n), lambda i: (0, 0))],
        out_specs=[pl.BlockSpec((1, rc, L), lambda i: (0, i, 0)),
                   pl.BlockSpec((1, rc, L), lambda i: (0, i, 0))],
        out_shape=[jax.ShapeDtypeStruct((1, C, L), F32), jax.ShapeDtypeStruct((1, C, L), F32)],
        compiler_params=_params(("arbitrary",)),
        name="hyena_short",
    )(ut, ut, ut, ut, ut, ut, kt, kt, kt, kt, jnp.asarray(gf, BF16), jnp.asarray(gi, BF16))
    return jnp.concatenate([outr, outi], axis=0)


def hyena_mixer(x, col0, conv_w, conv_b, w1, b1, w2, b2, w3, freq, bias):
    B, L, _ = x.shape
    C = bias.shape[-1]
    assert B == 2
    if L == FFT_TA * LANES:
        ut = short_conv(x, col0, 3 * C, conv_w, conv_b, True).reshape(B, 3, C * FFT_TA, LANES)
        kt = hyena_filter_taps(L, C, w1, b1, w2, b2, w3, freq, bias, True)
        return hyena_long(ut, kt).reshape(B, C, FFT_TA, LANES)
    uc = short_conv(x, col0, 3 * C, conv_w, conv_b, False)
    ut = uc.reshape(B, L, 3, C).transpose(0, 2, 3, 1)
    kt = hyena_filter_taps(L, C, w1, b1, w2, b2, w3, freq, bias, False)
    return hyena_short(ut, kt).transpose(0, 2, 1)


def _topk_kernel(l_ref, idx_ref, gate_ref):
    l = l_ref[...]
    E = l.shape[1]
    lane = lax.broadcasted_iota(jnp.int32, l.shape, 1).astype(F32)
    vals, idxs = [], []
    for _ in range(TOP_K):
        m = jnp.max(l, axis=-1, keepdims=True)
        am = jnp.min(jnp.where(l == m, lane, float(E)), axis=-1, keepdims=True)
        vals.append(m)
        idxs.append(am)
        l = jnp.where(lane == am, -jnp.inf, l)
    v = jnp.concatenate(vals, axis=1)
    e = jnp.exp(v - vals[0])
    gate_ref[...] = e / jnp.sum(e, axis=-1, keepdims=True)
    idx_ref[...] = jnp.concatenate(idxs, axis=1).astype(jnp.int32)


def topk_gates(logits):
    N, E = logits.shape
    tm = 512
    return pl.pallas_call(
        _topk_kernel,
        grid=(N // tm,),
        in_specs=[pl.BlockSpec((tm, E), lambda i: (i, 0))],
        out_specs=[pl.BlockSpec((tm, TOP_K), lambda i: (i, 0)), pl.BlockSpec((tm, TOP_K), lambda i: (i, 0))],
        out_shape=[jax.ShapeDtypeStruct((N, TOP_K), jnp.int32), jax.ShapeDtypeStruct((N, TOP_K), F32)],
        compiler_params=_params(("arbitrary",)),
        name="topk_gates",
    )(logits)


def _rank_kernel(idx_ref, rank_ref, cnt_ref, carry_ref):
    @pl.when(pl.program_id(0) == 0)
    def _():
        carry_ref[...] = jnp.zeros_like(carry_ref)

    idx = idx_ref[...]
    tm = idx.shape[0]
    E = carry_ref.shape[1]
    e_iota = lax.broadcasted_iota(jnp.int32, (tm, E), 1)
    sel = [idx[:, k:k + 1] == e_iota for k in range(TOP_K)]
    m = jnp.zeros((tm, E), F32)
    for s in sel:
        m = m + jnp.where(s, 1.0, 0.0)
    r_i = lax.broadcasted_iota(jnp.int32, (tm, tm), 0)
    c_i = lax.broadcasted_iota(jnp.int32, (tm, tm), 1)
    tri = jnp.where(r_i > c_i, 1.0, 0.0).astype(BF16)
    before = jnp.dot(tri, m.astype(BF16), preferred_element_type=F32) + carry_ref[...]
    ranks = [jnp.sum(jnp.where(s, before, 0.0), axis=-1, keepdims=True) for s in sel]
    rank_ref[...] = jnp.concatenate(ranks, axis=1).astype(jnp.int32)
    carry_ref[...] = carry_ref[...] + jnp.sum(m, axis=0, keepdims=True)
    cnt_ref[...] = carry_ref[...]


def expert_ranks(top_idx):
    N, _ = top_idx.shape
    tm = 512
    return pl.pallas_call(
        _rank_kernel,
        grid=(N // tm,),
        in_specs=[pl.BlockSpec((tm, TOP_K), lambda i: (i, 0))],
        out_specs=[pl.BlockSpec((tm, TOP_K), lambda i: (i, 0)), pl.BlockSpec((1, N_EXPERTS), lambda i: (0, 0))],
        out_shape=[jax.ShapeDtypeStruct((N, TOP_K), jnp.int32), jax.ShapeDtypeStruct((1, N_EXPERTS), F32)],
        scratch_shapes=[pltpu.VMEM((1, N_EXPERTS), F32)],
        compiler_params=_params(("arbitrary",)),
        name="expert_ranks",
    )(top_idx)


def _gather_params(sem):
    return pltpu.CompilerParams(dimension_semantics=sem, vmem_limit_bytes=VMEM_LIMIT, disable_bounds_checks=True)


def _moe_mm_kernel(rowtok_ref, be_ref, nreal_ref, h_ref, w1_ref, b1_ref, w2_ref, b2_ref, o_ref, xbuf, sems):
    i = pl.program_id(0)
    n_real = nreal_ref[0]
    tb = xbuf.shape[1]
    de = w2_ref.shape[0]

    def issue(blk, slot):
        for r in range(tb):
            tok = rowtok_ref[blk * tb + r]
            pltpu.make_async_copy(h_ref.at[pl.ds(tok, 1), :], xbuf.at[slot, pl.ds(r, 1), :], sems.at[slot]).start()

    def wait(slot):
        pltpu.make_async_copy(h_ref.at[pl.ds(0, tb), :], xbuf.at[slot], sems.at[slot]).wait()

    n_slots = MOE_AHEAD + 1
    last = n_real - 1

    @pl.when(i == 0)
    def _():
        for a in range(MOE_AHEAD):
            issue(jnp.minimum(i + a, last), a)

    @pl.when(i < n_real)
    def _():
        slot = i % n_slots
        wait(slot)
        issue(jnp.minimum(i + MOE_AHEAD, last), (i + MOE_AHEAD) % n_slots)
        x = jnp.concatenate(_unpack_bf16_pair(xbuf[slot]), axis=1).astype(BF16)
        gu = b1_ref[...]
        for k0 in range(0, x.shape[1], MOE_KCHUNK):
            ks = slice(k0, k0 + MOE_KCHUNK)
            gu = gu + jnp.dot(x[:, ks], w1_ref[ks, :].astype(BF16), preferred_element_type=F32)
        g = jnp.minimum(gu[:, :de], SWIGLU_LIMIT)
        up = jnp.clip(gu[:, de:], -SWIGLU_LIMIT, SWIGLU_LIMIT)
        act = (up + 1.0) * g * jax.nn.sigmoid(SWIGLU_ALPHA * g)
        o_ref[...] = _pack_bf16_pair(jnp.dot(act.astype(BF16), w2_ref[...], preferred_element_type=F32) + b2_ref[...])

    @pl.when(i == last)
    def _():
        for a in range(1, n_slots):
            wait((i + a) % n_slots)

    @pl.when(i >= n_real)
    def _():
        o_ref[...] = jnp.zeros_like(o_ref)


def moe_grouped_mm(row_tok, block_e, n_real, n_blocks, h, layer, w1, b1, w2, b2):
    D = 2 * h.shape[1]
    _, E, _, de2 = w1.shape
    de = de2 // 2
    tb = MOE_ROWS
    return pl.pallas_call(
        _moe_mm_kernel,
        grid_spec=pltpu.PrefetchScalarGridSpec(
            num_scalar_prefetch=3,
            grid=(n_blocks,),
            in_specs=[
                pl.BlockSpec(memory_space=pl.ANY),
                pl.BlockSpec((None, None, D, de2), lambda i, rt, be, nr: (layer, be[i], 0, 0)),
                pl.BlockSpec((None, None, 1, de2), lambda i, rt, be, nr: (layer, be[i], 0, 0)),
                pl.BlockSpec((None, None, de, D), lambda i, rt, be, nr: (layer, be[i], 0, 0)),
                pl.BlockSpec((None, None, 1, D), lambda i, rt, be, nr: (layer, be[i], 0, 0)),
            ],
            out_specs=pl.BlockSpec((tb, D // 2), lambda i, rt, be, nr: (i, 0)),
            scratch_shapes=[pltpu.VMEM((MOE_AHEAD + 1, tb, D // 2), jnp.uint32),
                            pltpu.SemaphoreType.DMA((MOE_AHEAD + 1,))],
        ),
        out_shape=jax.ShapeDtypeStruct((n_blocks * tb, D // 2), jnp.uint32),
        compiler_params=_gather_params(("arbitrary",)),
        name="moe_grouped_mm",
    )(row_tok, block_e, n_real, h, w1, b1.reshape(b1.shape[0], E, 1, de2), w2, b2.reshape(b2.shape[0], E, 1, D))


def _combine_kernel(dest_ref, ys_ref, gates_ref, x_ref, gt_ref, *refs, tok0, final_norm):
    fg_ref = refs[0] if final_norm else None
    o_ref, buf, sems = refs[-3:]
    nt = pl.num_programs(1)
    step = pl.program_id(0) * nt + pl.program_id(1)
    n_steps = pl.num_programs(0) * nt
    tm = buf.shape[2]

    def issue(st, slot):
        base = (tok0 + st * tm) * TOP_K
        for r in range(tm):
            for k in range(TOP_K):
                d = dest_ref[base + r * TOP_K + k]
                pltpu.make_async_copy(ys_ref.at[pl.ds(d, 1), :], buf.at[slot, k, pl.ds(r, 1), :], sems.at[slot]).start()

    def wait(slot):
        for k in range(TOP_K):
            pltpu.make_async_copy(ys_ref.at[pl.ds(0, tm), :], buf.at[slot, k], sems.at[slot]).wait()

    @pl.when(step == 0)
    def _():
        issue(step, 0)

    slot = step % 2
    wait(slot)
    issue(jnp.minimum(step + 1, n_steps - 1), 1 - slot)
    gates = gates_ref[...]
    acc_lo, acc_hi = 0.0, 0.0
    for k in range(TOP_K):
        lo, hi = _unpack_bf16_pair(buf[slot, k])
        acc_lo = acc_lo + gates[:, k:k + 1] * lo
        acc_hi = acc_hi + gates[:, k:k + 1] * hi
    y = x_ref[0] + gt_ref[0] * jnp.concatenate([acc_lo, acc_hi], axis=1)
    if final_norm:
        y = y * lax.rsqrt(jnp.mean(y * y, axis=-1, keepdims=True) + EPS) * fg_ref[...]
    o_ref[0] = y

    @pl.when(step == n_steps - 1)
    def _():
        wait(1 - slot)


def moe_combine(dest, ys, gates, x, gate_vec, tok0, final_g=None):
    B, T, D = x.shape
    tm = 128
    nt = T // tm
    g0 = tok0 // tm
    gmap = (lambda b, i, d: (b, 0, 0)) if gate_vec.shape[0] > 1 else (lambda b, i, d: (0, 0, 0))
    in_specs = [
        pl.BlockSpec(memory_space=pl.ANY),
        pl.BlockSpec((tm, TOP_K), lambda b, i, d: (g0 + b * nt + i, 0)),
        pl.BlockSpec((1, tm, D), lambda b, i, d: (b, i, 0)),
        pl.BlockSpec((1, 1, D), gmap),
    ]
    args = [dest, ys, gates, x, gate_vec]
    if final_g is not None:
        in_specs.append(pl.BlockSpec((1, D), lambda b, i, d: (0, 0)))
        args.append(final_g.reshape(1, D))
    return pl.pallas_call(
        functools.partial(_combine_kernel, tok0=tok0, final_norm=final_g is not None),
        grid_spec=pltpu.PrefetchScalarGridSpec(
            num_scalar_prefetch=1,
            grid=(B, nt),
            in_specs=in_specs,
            out_specs=pl.BlockSpec((1, tm, D), lambda b, i, d: (b, i, 0)),
            scratch_shapes=[pltpu.VMEM((2, TOP_K, tm, D // 2), jnp.uint32), pltpu.SemaphoreType.DMA((2,))],
        ),
        out_shape=jax.ShapeDtypeStruct((B, T, D), F32),
        compiler_params=_gather_params(("arbitrary", "arbitrary")),
        name="moe_combine",
    )(*args)


def moe_route(logits):
    N = logits.shape[0]
    tb = MOE_ROWS
    top_idx, gates = topk_gates(logits)
    rank, counts = expert_ranks(top_idx)
    counts = counts.reshape(N_EXPERTS).astype(jnp.int32)
    padded = (counts + tb - 1) // tb * tb
    pad_end = jnp.cumsum(padded)
    pad_start = pad_end - padded
    dest = pad_start[top_idx] + rank
    n_blocks = -(-(N * TOP_K) // tb) + N_EXPERTS
    n_real = (pad_end[-1] // tb).astype(jnp.int32).reshape(1)
    starts = jnp.arange(n_blocks, dtype=jnp.int32) * tb
    block_e = jnp.minimum(jnp.sum(pad_end[None, :] <= starts[:, None], axis=1), N_EXPERTS - 1).astype(jnp.int32)
    tok_of = jnp.arange(N * TOP_K, dtype=jnp.int32) // TOP_K
    row_tok = jnp.zeros((n_blocks * tb,), jnp.int32).at[dest.reshape(-1)].set(tok_of)
    return dest.reshape(-1).astype(jnp.int32), gates, row_tok, block_e, n_real, n_blocks


def moe_ffn(h, logits, layer, w1, b1, w2, b2):
    dest, gates, row_tok, block_e, n_real, n_blocks = moe_route(logits)
    ys = moe_grouped_mm(row_tok, block_e, n_real, n_blocks, h, layer, w1, b1, w2, b2)
    return dest, gates, ys


def kernel(x, c, ctx, c_ctx, w_ada, b_ada, g_mix, g_ffn, w_in, q_norm, k_norm, hy_conv_w, hy_conv_b, hy_w1, hy_b1, hy_w2, hy_b2, hy_w3, hy_freq, hy_bias, g_out, w_out, w_router, b_router, moe_w1, moe_b1, moe_w2, moe_b2, g_final):
    B, T, D = x.shape
    C = ctx.shape[1]
    depth = w_ada.shape[0]
    wq = N_HEADS * HEAD_DIM
    wkv = N_KV_HEADS * HEAD_DIM
    col_k, col_v, col_u = wq, wq + wkv, wq + 2 * wkv
    cos_t, sin_t = rope_tables(T)
    ones_c = jnp.ones((C, HEAD_DIM), F32)

    cvecs = jnp.concatenate([c, c_ctx[None, :], jnp.zeros((8 - B - 1, D), F32)], axis=0)
    ada = adaln_all(cvecs, w_ada, b_ada)

    moe_w2_b = moe_w2.astype(BF16)

    xc = ctx
    for i in range(depth):
        last = i == depth - 1
        mod = [ada[i, :B, j * D:(j + 1) * D].reshape(B, 1, D) for j in range(6)]
        cmod = [ada[i, B:B + 1, j * D:(j + 1) * D].reshape(1, 1, D) for j in range(6)]
        hp = (hy_conv_w[i], hy_conv_b[i], hy_w1[i], hy_b1[i], hy_w2[i], hy_b2[i], hy_w3[i], hy_freq[i], hy_bias[i])

        h = norm_mod(x, g_mix[i], mod[0], mod[1])
        hc = norm_mod(xc, g_mix[i], cmod[0], cmod[1])
        qkvu = matmul(h, w_in, i)
        q = head_norm(qkvu, 0, N_HEADS, q_norm[i], cos_t, sin_t, True, ATTN_SCALE)
        k = head_norm(qkvu, col_k, N_KV_HEADS, k_norm[i], cos_t, sin_t, True, 1.0)
        v = qkvu[:, :, col_v:col_u]
        hc1 = hc.reshape(1, B * C, D)
        if last:
            kvc = matmul(hc1, w_in, i, col_k, col_u - col_k).reshape(B, C, col_u - col_k)
            kc = head_norm(kvc, 0, N_KV_HEADS, k_norm[i], ones_c, ones_c, False, 1.0)
            vc = kvc[:, :, wkv:]
        else:
            qkvuc = matmul(hc1, w_in, i).reshape(B, C, -1)
            qc = head_norm(qkvuc, 0, N_HEADS, q_norm[i], ones_c, ones_c, False, ATTN_SCALE)
            kc = head_norm(qkvuc, col_k, N_KV_HEADS, k_norm[i], ones_c, ones_c, False, 1.0)
            vc = qkvuc[:, :, col_v:col_u]
        y_attn = attention(q, jnp.concatenate([kc, k], axis=1), values_with_ones(jnp.concatenate([vc, v], axis=1)))
        y_hy = hyena_mixer(qkvu, col_u, *hp)
        x = matmul_residual(mixnorm_channel_major(y_attn, y_hy, g_out[i]), w_out, i, x, mod[2])
        if not last:
            yc_attn = attention(qc, kc, values_with_ones(vc))
            yc_hy = hyena_mixer(qkvuc, col_u, *hp)
            xc = matmul_residual(mixnorm(yc_attn, yc_hy, g_out[i]).reshape(1, B * C, D), w_out, i,
                                 xc.reshape(1, B * C, D), cmod[2]).reshape(B, C, D)

        if last:
            hf, logits = norm_mod_router([x], g_ffn[i], [mod[3]], [mod[4]], w_router[i], b_router[i])
        else:
            hf, logits = norm_mod_router([x, xc], g_ffn[i], [mod[3], cmod[3]], [mod[4], cmod[4]],
                                         w_router[i], b_router[i])
        dest, gates, ys = moe_ffn(hf, logits, i, moe_w1, moe_b1, moe_w2_b, moe_b2)
        x = moe_combine(dest, ys, gates, x, mod[5], 0, g_final if last else None)
        if not last:
            xc = moe_combine(dest, ys, gates, xc, cmod[5], B * T)
    return x
```

```python
import functools
import math

import numpy as np
import jax
import jax.numpy as jnp
from jax import lax
from jax.experimental import pallas as pl
from jax.experimental.pallas import tpu as pltpu

F32 = jnp.float32
BF16 = jnp.bfloat16

GRID_W = 64
HEAD_DIM = 128
N_HEADS = 16
N_KV_HEADS = 4
GROUP = N_HEADS // N_KV_HEADS
HYENA_ORDER = 2
FILTER_BANDS = 16
FILTER_EMB = 1 + 2 * FILTER_BANDS
FILTER_HIDDEN = 64
DECAY_TARGET = 1e-2
FAST_DECAY_PCT = 0.3
SLOW_DECAY_PCT = 1.5
ROPE_THETA = 10000.0
N_EXPERTS = 32
TOP_K = 4
SWIGLU_LIMIT = 7.0
SWIGLU_ALPHA = 1.702
EPS = 1e-6

LANES = 128
V7X_VMEM_BYTES = 64 * 1024 * 1024
VMEM_LIMIT = V7X_VMEM_BYTES - 8 * 1024 * 1024
FFT_TA = 32
MOE_ROWS = 256
MOE_KCHUNK = 1024
MOE_AHEAD = 3
COMBINE_AHEAD = 2


def _params(sem):
    return pltpu.CompilerParams(dimension_semantics=sem, vmem_limit_bytes=VMEM_LIMIT)


def _pack_bf16_pair(x):
    n = x.shape[1] // 2
    return _pack_bf16_words(x[:, :n], x[:, n:])


def _pack_bf16_words(lo, hi):
    def rne(x):
        b = lax.bitcast_convert_type(x, jnp.uint32)
        return b + jnp.uint32(0x7FFF) + ((b >> 16) & jnp.uint32(1))

    return (rne(lo) >> 16) | (rne(hi) & jnp.uint32(0xFFFF0000))


def _unpack_bf16_pair(w):
    lo = lax.bitcast_convert_type(w << 16, F32)
    hi = lax.bitcast_convert_type(w & jnp.uint32(0xFFFF0000), F32)
    return lo, hi


def _adaln_kernel(c_ref, w_ref, b_ref, o_ref):
    c = c_ref[...]
    s = (c * jax.nn.sigmoid(c)).astype(BF16)
    o_ref[0] = jnp.dot(s, w_ref[0].astype(BF16), preferred_element_type=F32) + b_ref[0]


def adaln_all(cvecs, w_ada, b_ada):
    L, D, N = w_ada.shape
    tn = 512
    return pl.pallas_call(
        _adaln_kernel,
        grid=(L, N // tn),
        in_specs=[
            pl.BlockSpec((8, D), lambda l, j: (0, 0)),
            pl.BlockSpec((1, D, tn), lambda l, j: (l, 0, j)),
            pl.BlockSpec((1, 1, tn), lambda l, j: (l, 0, j)),
        ],
        out_specs=pl.BlockSpec((1, 8, tn), lambda l, j: (l, 0, j)),
        out_shape=jax.ShapeDtypeStruct((L, 8, N), F32),
        compiler_params=_params(("arbitrary", "arbitrary")),
        name="adaln",
    )(cvecs, w_ada, b_ada.reshape(L, 1, N))


def _norm_mod_kernel(x_ref, g_ref, sh_ref, sc_ref, h_ref):
    x = x_ref[0]
    y = x * lax.rsqrt(jnp.mean(x * x, axis=-1, keepdims=True) + EPS) * g_ref[...]
    h_ref[0] = (y * (1.0 + sc_ref[0]) + sh_ref[0]).astype(h_ref.dtype)


def _norm_mod_router_kernel(*refs, n_first):
    xs = refs[:-7]
    g_ref, sh_ref, sc_ref, wr_ref, br_ref, h_ref, l_ref = refs[-7:]

    def run(x_ref):
        x = x_ref[...]
        y = x * lax.rsqrt(jnp.mean(x * x, axis=-1, keepdims=True) + EPS) * g_ref[...]
        h = y * (1.0 + sc_ref[0]) + sh_ref[0]
        h_ref[...] = _pack_bf16_pair(h)
        e = l_ref.shape[1]
        h_hi = h.astype(BF16)
        h_lo = (h - h_hi.astype(F32)).astype(BF16)
        a = jnp.dot(h_hi, wr_ref[...], preferred_element_type=F32)
        b = jnp.dot(h_lo, wr_ref[:, :e], preferred_element_type=F32)
        l_ref[...] = a[:, :e] + a[:, e:] + b + br_ref[...]

    if len(xs) == 1:
        run(xs[0])
    else:
        i = pl.program_id(0)
        pl.when(i < n_first)(lambda: run(xs[0]))
        pl.when(i >= n_first)(lambda: run(xs[1]))


def _mod_map(bm):
    return (lambda b, i: (b, 0, 0)) if bm > 1 else (lambda b, i: (0, 0, 0))


def norm_mod(x, g, shift, scale, out_dtype=BF16):
    B, T, D = x.shape
    tm = min(T, 256)
    return pl.pallas_call(
        _norm_mod_kernel,
        grid=(B, T // tm),
        in_specs=[
            pl.BlockSpec((1, tm, D), lambda b, i: (b, i, 0)),
            pl.BlockSpec((1, D), lambda b, i: (0, 0)),
            pl.BlockSpec((1, 1, D), _mod_map(shift.shape[0])),
            pl.BlockSpec((1, 1, D), _mod_map(scale.shape[0])),
        ],
        out_specs=pl.BlockSpec((1, tm, D), lambda b, i: (b, i, 0)),
        out_shape=jax.ShapeDtypeStruct((B, T, D), out_dtype),
        compiler_params=_params(("arbitrary", "arbitrary")),
        name="norm_mod",
    )(x, g.reshape(1, D), shift, scale)


def norm_mod_router(xs, g, shifts, scales, w_router, b_router):
    D = xs[0].shape[-1]
    E = w_router.shape[1]
    tm = 256
    sizes = [a.shape[0] * a.shape[1] for a in xs]
    n_first = sizes[0] // tm
    n_tiles = sum(sizes) // tm
    assert all(a.shape[1] % tm == 0 for a in xs) and len(xs) <= 2

    def mod_rows(ms):
        return jnp.concatenate([jnp.broadcast_to(m, (a.shape[0], 1, D)) for m, a in zip(ms, xs)], axis=0)

    per0 = xs[0].shape[1] // tm

    def tile_row(i):
        if len(xs) == 1:
            return i // per0
        per1 = xs[1].shape[1] // tm
        return jnp.where(i < n_first, i // per0, xs[0].shape[0] + (i - n_first) // per1)

    x_specs = [pl.BlockSpec((tm, D), lambda i: (jnp.minimum(i, n_first - 1), 0))]
    if len(xs) == 2:
        x_specs.append(pl.BlockSpec((tm, D), lambda i: (jnp.maximum(i - n_first, 0), 0)))
    w_hi = w_router.astype(BF16)
    w_hilo = jnp.concatenate([w_hi, (w_router - w_hi.astype(F32)).astype(BF16)], axis=1)
    return pl.pallas_call(
        functools.partial(_norm_mod_router_kernel, n_first=n_first),
        grid=(n_tiles,),
        in_specs=x_specs + [
            pl.BlockSpec((1, D), lambda i: (0, 0)),
            pl.BlockSpec((1, 1, D), lambda i: (tile_row(i), 0, 0)),
            pl.BlockSpec((1, 1, D), lambda i: (tile_row(i), 0, 0)),
            pl.BlockSpec((D, 2 * E), lambda i: (0, 0)),
            pl.BlockSpec((1, E), lambda i: (0, 0)),
        ],
        out_specs=[pl.BlockSpec((tm, D // 2), lambda i: (i, 0)), pl.BlockSpec((tm, E), lambda i: (i, 0))],
        out_shape=[jax.ShapeDtypeStruct((n_tiles * tm, D // 2), jnp.uint32),
                   jax.ShapeDtypeStruct((n_tiles * tm, E), F32)],
        compiler_params=_params(("arbitrary",)),
        name="norm_mod_router",
    )(*[a.reshape(-1, D) for a in xs], g.reshape(1, D), mod_rows(shifts), mod_rows(scales),
      w_hilo, b_router.reshape(1, E))


def _mixnorm_kernel(a_ref, hy_ref, g_ref, o_ref):
    wa = a_ref.shape[-1]
    a = a_ref[0].astype(F32)
    b = hy_ref[0]
    g = g_ref[...]
    o_ref[0, :, :wa] = (a * lax.rsqrt(jnp.mean(a * a, axis=-1, keepdims=True) + EPS) * g[:, :wa]).astype(o_ref.dtype)
    o_ref[0, :, wa:] = (b * lax.rsqrt(jnp.mean(b * b, axis=-1, keepdims=True) + EPS) * g[:, wa:]).astype(o_ref.dtype)


def mixnorm(y_attn, y_hy, g):
    B, T, wa = y_attn.shape
    wh = y_hy.shape[-1]
    tm = min(T, 256)
    return pl.pallas_call(
        _mixnorm_kernel,
        grid=(B, T // tm),
        in_specs=[
            pl.BlockSpec((1, tm, wa), lambda b, i: (b, i, 0)),
            pl.BlockSpec((1, tm, wh), lambda b, i: (b, i, 0)),
            pl.BlockSpec((1, wa + wh), lambda b, i: (0, 0)),
        ],
        out_specs=pl.BlockSpec((1, tm, wa + wh), lambda b, i: (b, i, 0)),
        out_shape=jax.ShapeDtypeStruct((B, T, wa + wh), BF16),
        compiler_params=_params(("arbitrary", "arbitrary")),
        name="mixnorm",
    )(y_attn, y_hy, g.reshape(1, wa + wh))


def _mixnorm_cm_kernel(a_ref, hy_ref, g_ref, o_ref):
    wa = a_ref.shape[-1]
    g = g_ref[...]
    a = a_ref[0].astype(F32)
    o_ref[0, :, :wa] = (a * lax.rsqrt(jnp.mean(a * a, axis=-1, keepdims=True) + EPS) * g[:, :wa]).astype(o_ref.dtype)
    for j in range(hy_ref.shape[2]):
        b = hy_ref[0, :, j, :].T
        y = b * lax.rsqrt(jnp.mean(b * b, axis=-1, keepdims=True) + EPS) * g[:, wa:]
        o_ref[0, j * LANES:(j + 1) * LANES, wa:] = y.astype(o_ref.dtype)


def mixnorm_channel_major(y_attn, y_hy, g):
    B, T, wa = y_attn.shape
    wh, nta = y_hy.shape[1], y_hy.shape[2]
    jb = 8
    tm = jb * LANES
    assert nta * LANES == T and nta % jb == 0
    return pl.pallas_call(
        _mixnorm_cm_kernel,
        grid=(B, T // tm),
        in_specs=[
            pl.BlockSpec((1, tm, wa), lambda b, i: (b, i, 0)),
            pl.BlockSpec((1, wh, jb, LANES), lambda b, i: (b, 0, i, 0)),
            pl.BlockSpec((1, wa + wh), lambda b, i: (0, 0)),
        ],
        out_specs=pl.BlockSpec((1, tm, wa + wh), lambda b, i: (b, i, 0)),
        out_shape=jax.ShapeDtypeStruct((B, T, wa + wh), BF16),
        compiler_params=_params(("arbitrary", "arbitrary")),
        name="mixnorm_cm",
    )(y_attn, y_hy, g.reshape(1, wa + wh))


def _cast_weight_once(w_ref, wb_ref):
    @pl.when((pl.program_id(1) == 0) & (pl.program_id(2) == 0))
    def _():
        wb_ref[...] = w_ref[...].astype(BF16)


def _mm_kernel(x_ref, w_ref, o_ref, wb_ref):
    _cast_weight_once(w_ref, wb_ref)
    o_ref[0] = jnp.dot(x_ref[0], wb_ref[...], preferred_element_type=F32).astype(o_ref.dtype)


def _mm_res_kernel(x_ref, w_ref, r_ref, g_ref, o_ref, wb_ref):
    _cast_weight_once(w_ref, wb_ref)
    acc = jnp.dot(x_ref[0], wb_ref[...], preferred_element_type=F32)
    o_ref[0] = r_ref[0] + g_ref[0] * acc


MM_TN = 512


def matmul(x, w, layer, col0=0, n_out=None, out_dtype=F32):
    B, T, K = x.shape
    N = n_out or w.shape[2]
    tm, tn = min(T, 1024), MM_TN
    c0 = col0 // tn
    assert c0 * tn == col0 and N % tn == 0
    return pl.pallas_call(
        _mm_kernel,
        grid=(N // tn, B, T // tm),
        in_specs=[
            pl.BlockSpec((1, tm, K), lambda j, b, i: (b, i, 0)),
            pl.BlockSpec((None, K, tn), lambda j, b, i: (layer, 0, c0 + j)),
        ],
        out_specs=pl.BlockSpec((1, tm, tn), lambda j, b, i: (b, i, j)),
        out_shape=jax.ShapeDtypeStruct((B, T, N), out_dtype),
        scratch_shapes=[pltpu.VMEM((K, tn), BF16)],
        compiler_params=_params(("arbitrary", "arbitrary", "arbitrary")),
        name="matmul",
    )(x, w)


def matmul_residual(x, w, layer, res, gate):
    B, T, K = x.shape
    N = w.shape[2]
    tm, tn = min(T, 1024), MM_TN
    gmap = (lambda j, b, i: (b, 0, j)) if gate.shape[0] > 1 else (lambda j, b, i: (0, 0, j))
    return pl.pallas_call(
        _mm_res_kernel,
        grid=(N // tn, B, T // tm),
        in_specs=[
            pl.BlockSpec((1, tm, K), lambda j, b, i: (b, i, 0)),
            pl.BlockSpec((None, K, tn), lambda j, b, i: (layer, 0, j)),
            pl.BlockSpec((1, tm, tn), lambda j, b, i: (b, i, j)),
            pl.BlockSpec((1, 1, tn), gmap),
        ],
        out_specs=pl.BlockSpec((1, tm, tn), lambda j, b, i: (b, i, j)),
        out_shape=jax.ShapeDtypeStruct((B, T, N), F32),
        scratch_shapes=[pltpu.VMEM((K, tn), BF16)],
        compiler_params=_params(("arbitrary", "arbitrary", "arbitrary")),
        name="matmul_residual",
    )(x, w, res, gate)


def _head_norm_kernel(x_ref, g_ref, cos_ref, sin_ref, o_ref, *, n_heads, rope, scale):
    g = g_ref[...]
    if rope:
        cs = cos_ref[...]
        sn = sin_ref[...]
        lane = lax.broadcasted_iota(jnp.int32, cs.shape, 1)
        first = (lane % 64) < 32
    for h in range(n_heads):
        xh = x_ref[0, :, h * HEAD_DIM:(h + 1) * HEAD_DIM]
        y = xh * lax.rsqrt(jnp.mean(xh * xh, axis=-1, keepdims=True) + EPS) * g
        if rope:
            swapped = jnp.where(first, pltpu.roll(y, 96, 1), pltpu.roll(y, 32, 1))
            y = y * cs + swapped * sn
        o_ref[0, :, h * HEAD_DIM:(h + 1) * HEAD_DIM] = (y * scale).astype(o_ref.dtype)


def head_norm(x, col0, n_heads, g, cos_t, sin_t, rope, scale):
    B, T, _ = x.shape
    w = n_heads * HEAD_DIM
    tm = min(T, 256)
    cb = col0 // w
    assert cb * w == col0
    kern = functools.partial(_head_norm_kernel, n_heads=n_heads, rope=rope, scale=scale)
    return pl.pallas_call(
        kern,
        grid=(B, T // tm),
        in_specs=[
            pl.BlockSpec((1, tm, w), lambda b, i: (b, i, cb)),
            pl.BlockSpec((1, HEAD_DIM), lambda b, i: (0, 0)),
            pl.BlockSpec((tm, HEAD_DIM), lambda b, i: (i, 0)),
            pl.BlockSpec((tm, HEAD_DIM), lambda b, i: (i, 0)),
        ],
        out_specs=pl.BlockSpec((1, tm, w), lambda b, i: (b, i, 0)),
        out_shape=jax.ShapeDtypeStruct((B, T, w), BF16),
        compiler_params=_params(("arbitrary", "arbitrary")),
        name="head_norm",
    )(x, g.reshape(1, HEAD_DIM), cos_t, sin_t)


def rope_tables(T):
    pos = np.arange(T)
    r, col = pos // GRID_W, pos % GRID_W
    n_freq = HEAD_DIM // 4
    inv = ROPE_THETA ** (-np.arange(n_freq, dtype=np.float64) / n_freq)
    ar, ac = r[:, None] * inv, col[:, None] * inv
    cos_t = np.concatenate([np.cos(ar), np.cos(ar), np.cos(ac), np.cos(ac)], axis=1)
    sin_t = np.concatenate([-np.sin(ar), np.sin(ar), -np.sin(ac), np.sin(ac)], axis=1)
    return jnp.asarray(cos_t, F32), jnp.asarray(sin_t, F32)


ATTN_TK = 512
ATTN_SCALE = HEAD_DIM ** -0.5 * math.log2(math.e)


def _attn_kernel(q_ref, k_ref, v_ref, o_ref):
    tq = q_ref.shape[1]
    S = k_ref.shape[1]
    q = q_ref[0]
    qs = jnp.concatenate([q[:, j * HEAD_DIM:(j + 1) * HEAD_DIM] for j in range(GROUP)], axis=0)
    rows = GROUP * tq
    m = jnp.full((rows, 1), -jnp.inf, F32)
    acc = jnp.zeros((rows, 2 * HEAD_DIM), F32)
    for k0 in range(0, S, ATTN_TK):
        ks = slice(k0, min(k0 + ATTN_TK, S))
        s = lax.dot_general(qs, k_ref[0, ks, :], (((1,), (1,)), ((), ())), preferred_element_type=F32)
        m_new = jnp.maximum(m, jnp.max(s, axis=-1, keepdims=True))
        p = jnp.exp2(s - m_new)
        acc = jnp.exp2(m - m_new) * acc + jnp.dot(p.astype(BF16), v_ref[0, ks, :], preferred_element_type=F32)
        m = m_new
    o = acc[:, :HEAD_DIM] / acc[:, HEAD_DIM:HEAD_DIM + 1]
    for j in range(GROUP):
        o_ref[0, :, j * HEAD_DIM:(j + 1) * HEAD_DIM] = o[j * tq:(j + 1) * tq, :].astype(o_ref.dtype)


def values_with_ones(v):
    B, S, _ = v.shape
    v4 = v.reshape(B, S, N_KV_HEADS, HEAD_DIM).astype(BF16)
    pad = jnp.zeros((B, S, N_KV_HEADS, HEAD_DIM), BF16).at[..., 0].set(1.0)
    return jnp.concatenate([v4, pad], axis=-1).reshape(B, S, N_KV_HEADS * 2 * HEAD_DIM)


def attention(q, k, v1):
    B, T, _ = q.shape
    S = k.shape[1]
    assert S % 128 == 0
    tq = min(T, 256)
    gw = GROUP * HEAD_DIM
    return pl.pallas_call(
        _attn_kernel,
        grid=(B, N_KV_HEADS, T // tq),
        in_specs=[
            pl.BlockSpec((1, tq, gw), lambda b, g, i: (b, i, g)),
            pl.BlockSpec((1, S, HEAD_DIM), lambda b, g, i: (b, 0, g)),
            pl.BlockSpec((1, S, 2 * HEAD_DIM), lambda b, g, i: (b, 0, g)),
        ],
        out_specs=pl.BlockSpec((1, tq, gw), lambda b, g, i: (b, i, g)),
        out_shape=jax.ShapeDtypeStruct((B, T, N_HEADS * HEAD_DIM), BF16),
        compiler_params=_params(("arbitrary", "arbitrary", "arbitrary")),
        name="attention",
    )(q, k, v1)


PITCH_PAD = 4


def _store_channel_major(o_ref, chunk, n_chan, nta, stage):
    pitch = n_chan + PITCH_PAD
    for ta in range(nta):
        stage[ta * pitch:ta * pitch + n_chan, :] = chunk(ta)
    for c in range(n_chan):
        o_ref[0, c * nta:(c + 1) * nta, :] = stage[pl.ds(c, nta, stride=pitch), :]


def _short_conv_kernel(u_ref, w_ref, b_ref, o_ref, *scratch, channel_major):
    u = u_ref[0]
    L = u.shape[0]
    row = lax.broadcasted_iota(jnp.int32, u.shape, 0)
    prev = jnp.where(row == 0, 0.0, pltpu.roll(u, 1, 0))
    nxt = jnp.where(row == L - 1, 0.0, pltpu.roll(u, L - 1, 0))
    w = w_ref[...]
    y = prev * w[0:1, :] + u * w[1:2, :] + nxt * w[2:3, :] + b_ref[...]
    if not channel_major:
        o_ref[0] = y
    else:
        _store_channel_major(o_ref, lambda ta: y[ta * LANES:(ta + 1) * LANES, :].T, y.shape[1], L // LANES, scratch[0])


def short_conv(x, col0, width, w, b, channel_major):
    B, L, _ = x.shape
    cb = 256 if L > 1024 else 512
    assert col0 % cb == 0 and width % cb == 0
    off = col0 // cb
    scratch = []
    if channel_major:
        nta = L // LANES
        out_spec = pl.BlockSpec((1, cb * nta, LANES), lambda bi, j: (bi, j, 0))
        out_shape = jax.ShapeDtypeStruct((B, width * nta, LANES), F32)
        scratch = [pltpu.VMEM((nta * (cb + PITCH_PAD), LANES), F32)]
    else:
        out_spec = pl.BlockSpec((1, L, cb), lambda bi, j: (bi, 0, j))
        out_shape = jax.ShapeDtypeStruct((B, L, width), F32)
    return pl.pallas_call(
        functools.partial(_short_conv_kernel, channel_major=channel_major),
        grid=(B, width // cb),
        in_specs=[
            pl.BlockSpec((1, L, cb), lambda bi, j: (bi, 0, off + j)),
            pl.BlockSpec((3, cb), lambda bi, j: (0, j)),
            pl.BlockSpec((1, cb), lambda bi, j: (0, j)),
        ],
        out_specs=out_spec,
        out_shape=out_shape,
        scratch_shapes=scratch,
        compiler_params=_params(("arbitrary", "arbitrary")),
        name="short_conv",
    )(x, w, b.reshape(1, width))


def _filter_hidden_kernel(f_ref, w1_ref, b1_ref, w2_ref, b2_ref, fr_ref, o_ref):
    hp = lax.Precision.HIGHEST
    fr = fr_ref[...]
    h1 = jnp.sin(fr * (jnp.dot(w1_ref[...], f_ref[...], preferred_element_type=F32, precision=hp) + b1_ref[...]))
    o_ref[...] = jnp.sin(fr * (jnp.dot(w2_ref[...], h1, preferred_element_type=F32, precision=hp) + b2_ref[...]))


def filter_hidden(feats_t, w1, b1, w2, b2, freq):
    H = FILTER_HIDDEN
    P = feats_t.shape[1]
    fe = feats_t.shape[0]
    w1t = jnp.zeros((H, fe), F32).at[:, :FILTER_EMB].set(w1.T)
    return pl.pallas_call(
        _filter_hidden_kernel,
        out_shape=jax.ShapeDtypeStruct((H, P), F32),
        compiler_params=pltpu.CompilerParams(vmem_limit_bytes=VMEM_LIMIT),
        name="filter_hidden",
    )(feats_t, w1t, b1.reshape(H, 1), w2.T, b2.reshape(H, 1), freq.reshape(H, 1))


def _filter_k_kernel(w3_ref, hid_ref, tpos_ref, mask_ref, e0_ref, dl_ref, bias_ref, o_ref, *scratch, channel_major):
    w = w3_ref[0]
    w_hi = w.astype(BF16)
    w_lo = (w - w_hi.astype(F32)).astype(BF16)
    hd = hid_ref[0]
    h_hi = hd.astype(BF16)
    h_lo = (hd - h_hi.astype(F32)).astype(BF16)
    h = jnp.dot(jnp.concatenate([w_hi, w_lo, w_hi], axis=1), jnp.concatenate([h_hi, h_hi, h_lo], axis=0),
                preferred_element_type=F32)
    decay = jnp.exp(-tpos_ref[0] * dl_ref[...])
    k = mask_ref[0] * decay * h + bias_ref[0] * e0_ref[...]
    if not channel_major:
        o_ref[0] = k
    else:
        rc, L = k.shape
        _store_channel_major(o_ref, lambda ta: k[:, ta * LANES:(ta + 1) * LANES], rc, L // LANES, scratch[0])


def filter_k(w3t, hid, tpos, mask, e0, deltas, bias_aug, channel_major):
    OD, C, H = w3t.shape
    L = hid.shape[-1]
    rc = 256
    scratch = []
    if channel_major:
        nta = L // LANES
        out_spec = pl.BlockSpec((1, rc * nta, LANES), lambda od, j: (od, j, 0))
        out_shape = jax.ShapeDtypeStruct((OD, C * nta, LANES), F32)
        scratch = [pltpu.VMEM((nta * (rc + PITCH_PAD), LANES), F32)]
    else:
        out_spec = pl.BlockSpec((1, rc, L), lambda od, j: (od, j, 0))
        out_shape = jax.ShapeDtypeStruct((OD, C, L), F32)
    return pl.pallas_call(
        functools.partial(_filter_k_kernel, channel_major=channel_major),
        grid=(OD, C // rc),
        in_specs=[
            pl.BlockSpec((1, rc, H), lambda od, j: (od, j, 0)),
            pl.BlockSpec((1, H, L), lambda od, j: (od % 2, 0, 0)),
            pl.BlockSpec((1, 1, L), lambda od, j: (od % 2, 0, 0)),
            pl.BlockSpec((1, 1, L), lambda od, j: (od % 2, 0, 0)),
            pl.BlockSpec((1, L), lambda od, j: (0, 0)),
            pl.BlockSpec((rc, 1), lambda od, j: (j, 0)),
            pl.BlockSpec((1, rc, 1), lambda od, j: (od, j, 0)),
        ],
        out_specs=out_spec,
        out_shape=out_shape,
        scratch_shapes=scratch,
        compiler_params=_params(("arbitrary", "arbitrary")),
        name="filter_k",
    )(w3t, hid, tpos, mask, e0, deltas, bias_aug)


def hyena_filter_taps(L, C, w1, b1, w2, b2, w3, freq, bias, channel_major):
    pos = np.arange(L, dtype=np.float64)
    posr = np.where(pos == 0, 0.0, L - pos)

    def feats(p):
        t = p / max(L - 1, 1)
        bands = np.linspace(1e-4, FILTER_BANDS - 1, FILTER_BANDS)
        ang = (2 * math.pi / L) * p[:, None] * bands
        return np.concatenate([t[:, None], np.cos(ang), np.sin(ang)], axis=-1), t

    f0, t0 = feats(pos)
    f1, t1 = feats(posr)
    fe = 40
    ft = np.zeros((fe, 2 * L))
    ft[:FILTER_EMB, :L] = f0.T
    ft[:FILTER_EMB, L:] = f1.T
    hid = filter_hidden(jnp.asarray(ft, F32), w1, b1, w2, b2, freq)
    hid = jnp.stack([hid[:, :L], hid[:, L:]], axis=0)
    tpos = jnp.asarray(np.stack([t0, t1])[:, None, :], F32)
    mask = jnp.asarray(np.stack([np.ones(L), (pos > 0).astype(np.float64)])[:, None, :], F32)
    e0 = jnp.asarray((pos == 0).astype(np.float64)[None, :], F32)
    max_decay = math.log(DECAY_TARGET) / FAST_DECAY_PCT
    min_decay = math.log(DECAY_TARGET) / SLOW_DECAY_PCT
    deltas = jnp.asarray(np.abs(np.linspace(min_decay, max_decay, C))[:, None], F32)
    w3t = w3.T.reshape(HYENA_ORDER * 2, C, FILTER_HIDDEN)
    bias_aug = jnp.stack([bias, jnp.zeros_like(bias)], axis=1).reshape(HYENA_ORDER * 2, C, 1)
    k = filter_k(w3t, hid, tpos, mask, e0, deltas, bias_aug, channel_major)
    return k.reshape((HYENA_ORDER, 2) + k.shape[1:])


def _fft_consts():
    def emb(fr, fi):
        return np.block([[fr, fi], [-fi, fr]])

    n2 = LANES
    a = 2 * np.pi * np.outer(np.arange(n2), np.arange(n2)) / n2
    g128f = emb(np.cos(a), -np.sin(a))
    g128i = emb(np.cos(a), np.sin(a)) / (2.0 * FFT_TA * LANES)
    ta = np.arange(FFT_TA)
    eye = np.eye(LANES // FFT_TA)
    be = 2 * np.pi * np.outer(ta, ta) / FFT_TA
    bo_f = be + 2 * np.pi * ta[:, None] / (2 * FFT_TA)
    bo_i = be + 2 * np.pi * ta[None, :] / (2 * FFT_TA)
    g32 = [emb(np.kron(eye, np.cos(be)), np.kron(eye, -np.sin(be))), emb(np.kron(eye, np.cos(be)), np.kron(eye, np.sin(be))),
           emb(np.kron(eye, np.cos(bo_f)), np.kron(eye, -np.sin(bo_f))), emb(np.kron(eye, np.cos(bo_i)), np.kron(eye, np.sin(bo_i)))]
    mats = jnp.asarray(np.stack([g128f, g128i] + g32), BF16)
    n = FFT_TA * LANES
    fa = np.tile(ta, LANES // FFT_TA)[:, None]
    tb = np.arange(LANES)[None, :]
    th_e = 2 * np.pi * fa * tb / n
    th_o = 2 * np.pi * (2 * fa + 1) * tb / (2 * n)
    tw = jnp.asarray(np.stack([np.cos(th_e), np.sin(th_e), np.cos(th_o), np.sin(th_o)]), F32)
    return mats, tw


def _rows_to_lanes(re, im):
    parts = []
    for g in range(re.shape[0] // LANES):
        sl = slice(g * LANES, (g + 1) * LANES)
        parts.append(re[sl, :].T if im is None else jnp.concatenate([re[sl, :].T, im[sl, :].T], axis=1))
    return jnp.concatenate(parts, axis=0).astype(BF16)


def _dft_fwd(lhs, g32, g128, twc, tws):
    gm = g32 if lhs.shape[1] == 2 * LANES else g32[:LANES, :]
    o1 = jnp.dot(lhs, gm, preferred_element_type=F32)
    parts = []
    for g in range(lhs.shape[0] // LANES):
        sl = slice(g * LANES, (g + 1) * LANES)
        r = o1[sl, :LANES].T
        i = o1[sl, LANES:].T
        parts.append(jnp.concatenate([r * twc + i * tws, i * twc - r * tws], axis=1))
    o2 = jnp.dot(jnp.concatenate(parts, axis=0).astype(BF16), g128, preferred_element_type=F32)
    return o2[:, :LANES], o2[:, LANES:]


def _inv_stage1(re, im, g128, twc, tws):
    o1 = jnp.dot(jnp.concatenate([re, im], axis=1).astype(BF16), g128, preferred_element_type=F32)
    parts = []
    for g in range(re.shape[0] // LANES):
        sl = slice(g * LANES, (g + 1) * LANES)
        r = o1[sl, :LANES]
        i = o1[sl, LANES:]
        parts.append(jnp.concatenate([(r * twc - i * tws).T, (i * twc + r * tws).T], axis=1))
    return jnp.concatenate(parts, axis=0).astype(BF16)


def _lanes_to_rows(o2):
    rs, is_ = [], []
    for g in range(o2.shape[0] // LANES):
        sl = slice(g * LANES, (g + 1) * LANES)
        rs.append(o2[sl, :LANES].T)
        is_.append(o2[sl, LANES:].T)
    return jnp.concatenate(rs, axis=0), jnp.concatenate(is_, axis=0)


def _cmul(ar, ai, br, bi):
    return ar * br - ai * bi, ar * bi + ai * br


def _hyena_long_kernel(vr_ref, vi_ref, x1r_ref, x1i_ref, x2r_ref, x2i_ref,
                       k0lo_ref, k0hi_ref, k1lo_ref, k1hi_ref, mats_ref, tw_ref, o_ref):
    g128f, g128i = mats_ref[0], mats_ref[1]
    g32e_f, g32e_i, g32o_f, g32o_i = mats_ref[2], mats_ref[3], mats_ref[4], mats_ref[5]
    tce, tse, tco, tso = tw_ref[0], tw_ref[1], tw_ref[2], tw_ref[3]

    def conv(zr, zi, klo, khi):
        ke = _dft_fwd(_rows_to_lanes(klo + khi, None), g32e_f, g128f, tce, tse)
        ko = _dft_fwd(_rows_to_lanes(klo - khi, None), g32o_f, g128f, tco, tso)
        z_t = _rows_to_lanes(zr, zi)
        ze = _dft_fwd(z_t, g32e_f, g128f, tce, tse)
        zo = _dft_fwd(z_t, g32o_f, g128f, tco, tso)
        ye = _inv_stage1(*_cmul(*ze, *ke), g128i, tce, tse)
        yo = _inv_stage1(*_cmul(*zo, *ko), g128i, tco, tso)
        return _lanes_to_rows(jnp.dot(ye, g32e_i, preferred_element_type=F32)
                              + jnp.dot(yo, g32o_i, preferred_element_type=F32))

    yr, yi = conv(vr_ref[0, 0], vi_ref[0, 0], k0lo_ref[0, 0], k0hi_ref[0, 0])
    z1r = x1r_ref[0, 0] * yr
    z1i = x1i_ref[0, 0] * yi
    yr, yi = conv(z1r, z1i, k1lo_ref[0, 0], k1hi_ref[0, 0])
    o_ref[0] = x2r_ref[0, 0] * yr
    o_ref[1] = x2i_ref[0, 0] * yi


def hyena_long(ut, kt):
    _, _, R, _ = ut.shape
    rb = 32 * FFT_TA
    mats, tw = _fft_consts()

    def uspec(b, p):
        return pl.BlockSpec((1, 1, rb, LANES), lambda i: (b, p, i, 0))

    def kspec(o, d):
        return pl.BlockSpec((1, 1, rb, LANES), lambda i: (o, d, i, 0))

    return pl.pallas_call(
        _hyena_long_kernel,
        grid=(R // rb,),
        in_specs=[uspec(0, 0), uspec(1, 0), uspec(0, 1), uspec(1, 1), uspec(0, 2), uspec(1, 2),
                  kspec(0, 0), kspec(0, 1), kspec(1, 0), kspec(1, 1),
                  pl.BlockSpec(mats.shape, lambda i: (0, 0, 0)),
                  pl.BlockSpec(tw.shape, lambda i: (0, 0, 0))],
        out_specs=pl.BlockSpec((2, rb, LANES), lambda i: (0, i, 0)),
        out_shape=jax.ShapeDtypeStruct((2, R, LANES), F32),
        compiler_params=_params(("arbitrary",)),
        name="hyena_long",
    )(ut, ut, ut, ut, ut, ut, kt, kt, kt, kt, mats, tw)


def _hyena_short_kernel(vr_ref, vi_ref, x1r_ref, x1i_ref, x2r_ref, x2i_ref,
                        k0lo_ref, k0hi_ref, k1lo_ref, k1hi_ref, gf_ref, gi_ref, or_ref, oi_ref):
    L = vr_ref.shape[-1]
    n = 2 * L

    def conv(zr, zi, klo, khi):
        kk = jnp.concatenate([klo, khi], axis=1).astype(BF16)
        ks = jnp.dot(kk, gf_ref[:n, :], preferred_element_type=F32)
        kr, ki = ks[:, :n], ks[:, n:]
        zz = jnp.concatenate([zr, zi], axis=1).astype(BF16)
        zs = jnp.dot(zz, gf_ref[n:, :], preferred_element_type=F32)
        sr, si = zs[:, :n], zs[:, n:]
        pr = sr * kr - si * ki
        pi = sr * ki + si * kr
        y = jnp.dot(jnp.concatenate([pr, pi], axis=1).astype(BF16), gi_ref[...], preferred_element_type=F32)
        return y[:, :L], y[:, L:]

    yr, yi = conv(vr_ref[0, 0], vi_ref[0, 0], k0lo_ref[0, 0], k0hi_ref[0, 0])
    z1r = x1r_ref[0, 0] * yr
    z1i = x1i_ref[0, 0] * yi
    yr, yi = conv(z1r, z1i, k1lo_ref[0, 0], k1hi_ref[0, 0])
    or_ref[0] = x2r_ref[0, 0] * yr
    oi_ref[0] = x2i_ref[0, 0] * yi


def hyena_short(ut, kt):
    _, _, C, L = ut.shape
    n = 2 * L
    t = np.arange(n)
    a = 2 * np.pi * np.outer(t, t) / n
    co, si = np.cos(a), np.sin(a)
    gf = np.concatenate([
        np.concatenate([co, -si], axis=1),
        np.concatenate([co[:L], -si[:L]], axis=1),
        np.concatenate([si[:L], co[:L]], axis=1)], axis=0)
    gi = np.concatenate([
        np.concatenate([co[:, :L], si[:, :L]], axis=1),
        np.concatenate([-si[:, :L], co[:, :L]], axis=1)], axis=0) / n
    rc = 256

    def uspec(b, p):
        return pl.BlockSpec((1, 1, rc, L), lambda i: (b, p, i, 0))

    outr, outi = pl.pallas_call(
        _hyena_short_kernel,
        grid=(C // rc,),
        in_specs=[uspec(0, 0), uspec(1, 0), uspec(0, 1), uspec(1, 1), uspec(0, 2), uspec(1, 2),
                  uspec(0, 0), uspec(0, 1), uspec(1, 0), uspec(1, 1),
                  pl.BlockSpec((2 * n, 2 * n), lambda i: (0, 0)),
                  pl.BlockSpec((2 * n, n), lambda i: (0, 0))],
        out_specs=[pl.BlockSpec((1, rc, L), lambda i: (0, i, 0)),
                   pl.BlockSpec((1, rc, L), lambda i: (0, i, 0))],
        out_shape=[jax.ShapeDtypeStruct((1, C, L), F32), jax.ShapeDtypeStruct((1, C, L), F32)],
        compiler_params=_params(("arbitrary",)),
        name="hyena_short",
    )(ut, ut, ut, ut, ut, ut, kt, kt, kt, kt, jnp.asarray(gf, BF16), jnp.asarray(gi, BF16))
    return jnp.concatenate([outr, outi], axis=0)


def hyena_mixer(x, col0, conv_w, conv_b, w1, b1, w2, b2, w3, freq, bias):
    B, L, _ = x.shape
    C = bias.shape[-1]
    assert B == 2
    if L == FFT_TA * LANES:
        ut = short_conv(x, col0, 3 * C, conv_w, conv_b, True).reshape(B, 3, C * FFT_TA, LANES)
        kt = hyena_filter_taps(L, C, w1, b1, w2, b2, w3, freq, bias, True)
        return hyena_long(ut, kt).reshape(B, C, FFT_TA, LANES)
    uc = short_conv(x, col0, 3 * C, conv_w, conv_b, False)
    ut = uc.reshape(B, L, 3, C).transpose(0, 2, 3, 1)
    kt = hyena_filter_taps(L, C, w1, b1, w2, b2, w3, freq, bias, False)
    return hyena_short(ut, kt).transpose(0, 2, 1)


def _topk_kernel(l_ref, idx_ref, gate_ref):
    l = l_ref[...]
    E = l.shape[1]
    lane = lax.broadcasted_iota(jnp.int32, l.shape, 1).astype(F32)
    vals, idxs = [], []
    for _ in range(TOP_K):
        m = jnp.max(l, axis=-1, keepdims=True)
        am = jnp.min(jnp.where(l == m, lane, float(E)), axis=-1, keepdims=True)
        vals.append(m)
        idxs.append(am)
        l = jnp.where(lane == am, -jnp.inf, l)
    v = jnp.concatenate(vals, axis=1)
    e = jnp.exp(v - vals[0])
    gate_ref[...] = e / jnp.sum(e, axis=-1, keepdims=True)
    idx_ref[...] = jnp.concatenate(idxs, axis=1).astype(jnp.int32)


def topk_gates(logits):
    N, E = logits.shape
    tm = 512
    return pl.pallas_call(
        _topk_kernel,
        grid=(N // tm,),
        in_specs=[pl.BlockSpec((tm, E), lambda i: (i, 0))],
        out_specs=[pl.BlockSpec((tm, TOP_K), lambda i: (i, 0)), pl.BlockSpec((tm, TOP_K), lambda i: (i, 0))],
        out_shape=[jax.ShapeDtypeStruct((N, TOP_K), jnp.int32), jax.ShapeDtypeStruct((N, TOP_K), F32)],
        compiler_params=_params(("arbitrary",)),
        name="topk_gates",
    )(logits)


def _rank_kernel(idx_ref, rank_ref, cnt_ref, carry_ref):
    @pl.when(pl.program_id(0) == 0)
    def _():
        carry_ref[...] = jnp.zeros_like(carry_ref)

    idx = idx_ref[...]
    tm = idx.shape[0]
    E = carry_ref.shape[1]
    e_iota = lax.broadcasted_iota(jnp.int32, (tm, E), 1)
    sel = [idx[:, k:k + 1] == e_iota for k in range(TOP_K)]
    m = jnp.zeros((tm, E), F32)
    for s in sel:
        m = m + jnp.where(s, 1.0, 0.0)
    r_i = lax.broadcasted_iota(jnp.int32, (tm, tm), 0)
    c_i = lax.broadcasted_iota(jnp.int32, (tm, tm), 1)
    tri = jnp.where(r_i > c_i, 1.0, 0.0).astype(BF16)
    before = jnp.dot(tri, m.astype(BF16), preferred_element_type=F32) + carry_ref[...]
    ranks = [jnp.sum(jnp.where(s, before, 0.0), axis=-1, keepdims=True) for s in sel]
    rank_ref[...] = jnp.concatenate(ranks, axis=1).astype(jnp.int32)
    carry_ref[...] = carry_ref[...] + jnp.sum(m, axis=0, keepdims=True)
    cnt_ref[...] = carry_ref[...]


def expert_ranks(top_idx):
    N, _ = top_idx.shape
    tm = 512
    return pl.pallas_call(
        _rank_kernel,
        grid=(N // tm,),
        in_specs=[pl.BlockSpec((tm, TOP_K), lambda i: (i, 0))],
        out_specs=[pl.BlockSpec((tm, TOP_K), lambda i: (i, 0)), pl.BlockSpec((1, N_EXPERTS), lambda i: (0, 0))],
        out_shape=[jax.ShapeDtypeStruct((N, TOP_K), jnp.int32), jax.ShapeDtypeStruct((1, N_EXPERTS), F32)],
        scratch_shapes=[pltpu.VMEM((1, N_EXPERTS), F32)],
        compiler_params=_params(("arbitrary",)),
        name="expert_ranks",
    )(top_idx)


def _gather_params(sem):
    return pltpu.CompilerParams(dimension_semantics=sem, vmem_limit_bytes=VMEM_LIMIT, disable_bounds_checks=True)


def _moe_mm_kernel(rowtok_ref, be_ref, nreal_ref, h_ref, w1_ref, b1_ref, w2_ref, b2_ref, o_ref, xbuf, sems):
    i = pl.program_id(0)
    n_real = nreal_ref[0]
    tb = xbuf.shape[1]
    de = w2_ref.shape[0]

    def issue(blk, slot):
        for r in range(tb):
            tok = rowtok_ref[blk * tb + r]
            pltpu.make_async_copy(h_ref.at[pl.ds(tok, 1), :], xbuf.at[slot, pl.ds(r, 1), :], sems.at[slot]).start()

    def wait(slot):
        pltpu.make_async_copy(h_ref.at[pl.ds(0, tb), :], xbuf.at[slot], sems.at[slot]).wait()

    n_slots = MOE_AHEAD + 1
    last = n_real - 1

    @pl.when(i == 0)
    def _():
        for a in range(MOE_AHEAD):
            issue(jnp.minimum(i + a, last), a)

    @pl.when(i < n_real)
    def _():
        slot = i % n_slots
        wait(slot)
        issue(jnp.minimum(i + MOE_AHEAD, last), (i + MOE_AHEAD) % n_slots)
        x = jnp.concatenate(_unpack_bf16_pair(xbuf[slot]), axis=1).astype(BF16)
        gu = b1_ref[...]
        for k0 in range(0, x.shape[1], MOE_KCHUNK):
            ks = slice(k0, k0 + MOE_KCHUNK)
            gu = gu + jnp.dot(x[:, ks], w1_ref[ks, :].astype(BF16), preferred_element_type=F32)
        g = jnp.minimum(gu[:, :de], SWIGLU_LIMIT)
        up = jnp.clip(gu[:, de:], -SWIGLU_LIMIT, SWIGLU_LIMIT)
        act = (up + 1.0) * g * jax.nn.sigmoid(SWIGLU_ALPHA * g)
        o_ref[...] = _pack_bf16_pair(jnp.dot(act.astype(BF16), w2_ref[...], preferred_element_type=F32) + b2_ref[...])

    @pl.when(i == last)
    def _():
        for a in range(1, n_slots):
            wait((i + a) % n_slots)

    @pl.when(i >= n_real)
    def _():
        o_ref[...] = jnp.zeros_like(o_ref)


def moe_grouped_mm(row_tok, block_e, n_real, n_blocks, h, layer, w1, b1, w2, b2):
    D = 2 * h.shape[1]
    _, E, _, de2 = w1.shape
    de = de2 // 2
    tb = MOE_ROWS
    return pl.pallas_call(
        _moe_mm_kernel,
        grid_spec=pltpu.PrefetchScalarGridSpec(
            num_scalar_prefetch=3,
            grid=(n_blocks,),
            in_specs=[
                pl.BlockSpec(memory_space=pl.ANY),
                pl.BlockSpec((None, None, D, de2), lambda i, rt, be, nr: (layer, be[i], 0, 0)),
                pl.BlockSpec((None, None, 1, de2), lambda i, rt, be, nr: (layer, be[i], 0, 0)),
                pl.BlockSpec((None, None, de, D), lambda i, rt, be, nr: (layer, be[i], 0, 0)),
                pl.BlockSpec((None, None, 1, D), lambda i, rt, be, nr: (layer, be[i], 0, 0)),
            ],
            out_specs=pl.BlockSpec((tb, D // 2), lambda i, rt, be, nr: (i, 0)),
            scratch_shapes=[pltpu.VMEM((MOE_AHEAD + 1, tb, D // 2), jnp.uint32),
                            pltpu.SemaphoreType.DMA((MOE_AHEAD + 1,))],
        ),
        out_shape=jax.ShapeDtypeStruct((n_blocks * tb, D // 2), jnp.uint32),
        compiler_params=_gather_params(("arbitrary",)),
        name="moe_grouped_mm",
    )(row_tok, block_e, n_real, h, w1, b1.reshape(b1.shape[0], E, 1, de2), w2, b2.reshape(b2.shape[0], E, 1, D))


def _combine_kernel(dest_ref, ys_ref, gates_ref, x_ref, gt_ref, *refs, tok0, final_norm):
    fg_ref = refs[0] if final_norm else None
    o_ref, buf, sems = refs[-3:]
    nt = pl.num_programs(1)
    step = pl.program_id(0) * nt + pl.program_id(1)
    n_steps = pl.num_programs(0) * nt
    tm = buf.shape[2]

    def issue(st, slot):
        base = (tok0 + st * tm) * TOP_K
        for r in range(tm):
            for k in range(TOP_K):
                d = dest_ref[base + r * TOP_K + k]
                pltpu.make_async_copy(ys_ref.at[pl.ds(d, 1), :], buf.at[slot, k, pl.ds(r, 1), :], sems.at[slot]).start()

    def wait(slot):
        for k in range(TOP_K):
            pltpu.make_async_copy(ys_ref.at[pl.ds(0, tm), :], buf.at[slot, k], sems.at[slot]).wait()

    n_slots = COMBINE_AHEAD + 1
    last = n_steps - 1

    @pl.when(step == 0)
    def _():
        for a in range(COMBINE_AHEAD):
            issue(jnp.minimum(step + a, last), a)

    slot = step % n_slots
    wait(slot)
    issue(jnp.minimum(step + COMBINE_AHEAD, last), (step + COMBINE_AHEAD) % n_slots)
    gates = gates_ref[...]
    acc_lo, acc_hi = 0.0, 0.0
    for k in range(TOP_K):
        lo, hi = _unpack_bf16_pair(buf[slot, k])
        acc_lo = acc_lo + gates[:, k:k + 1] * lo
        acc_hi = acc_hi + gates[:, k:k + 1] * hi
    y = x_ref[0] + gt_ref[0] * jnp.concatenate([acc_lo, acc_hi], axis=1)
    if final_norm:
        y = y * lax.rsqrt(jnp.mean(y * y, axis=-1, keepdims=True) + EPS) * fg_ref[...]
    o_ref[0] = y

    @pl.when(step == last)
    def _():
        for a in range(1, n_slots):
            wait((step + a) % n_slots)


def moe_combine(dest, ys, gates, x, gate_vec, tok0, final_g=None):
    B, T, D = x.shape
    tm = 128
    nt = T // tm
    g0 = tok0 // tm
    gmap = (lambda b, i, d: (b, 0, 0)) if gate_vec.shape[0] > 1 else (lambda b, i, d: (0, 0, 0))
    in_specs = [
        pl.BlockSpec(memory_space=pl.ANY),
        pl.BlockSpec((tm, TOP_K), lambda b, i, d: (g0 + b * nt + i, 0)),
        pl.BlockSpec((1, tm, D), lambda b, i, d: (b, i, 0)),
        pl.BlockSpec((1, 1, D), gmap),
    ]
    args = [dest, ys, gates, x, gate_vec]
    if final_g is not None:
        in_specs.append(pl.BlockSpec((1, D), lambda b, i, d: (0, 0)))
        args.append(final_g.reshape(1, D))
    return pl.pallas_call(
        functools.partial(_combine_kernel, tok0=tok0, final_norm=final_g is not None),
        grid_spec=pltpu.PrefetchScalarGridSpec(
            num_scalar_prefetch=1,
            grid=(B, nt),
            in_specs=in_specs,
            out_specs=pl.BlockSpec((1, tm, D), lambda b, i, d: (b, i, 0)),
            scratch_shapes=[pltpu.VMEM((COMBINE_AHEAD + 1, TOP_K, tm, D // 2), jnp.uint32),
                            pltpu.SemaphoreType.DMA((COMBINE_AHEAD + 1,))],
        ),
        out_shape=jax.ShapeDtypeStruct((B, T, D), F32),
        compiler_params=_gather_params(("arbitrary", "arbitrary")),
        name="moe_combine",
    )(*args)


def moe_route(logits):
    N = logits.shape[0]
    tb = MOE_ROWS
    top_idx, gates = topk_gates(logits)
    rank, counts = expert_ranks(top_idx)
    counts = counts.reshape(N_EXPERTS).astype(jnp.int32)
    padded = (counts + tb - 1) // tb * tb
    pad_end = jnp.cumsum(padded)
    pad_start = pad_end - padded
    dest = pad_start[top_idx] + rank
    n_blocks = -(-(N * TOP_K) // tb) + N_EXPERTS
    n_real = (pad_end[-1] // tb).astype(jnp.int32).reshape(1)
    starts = jnp.arange(n_blocks, dtype=jnp.int32) * tb
    block_e = jnp.minimum(jnp.sum(pad_end[None, :] <= starts[:, None], axis=1), N_EXPERTS - 1).astype(jnp.int32)
    tok_of = jnp.arange(N * TOP_K, dtype=jnp.int32) // TOP_K
    row_tok = jnp.zeros((n_blocks * tb,), jnp.int32).at[dest.reshape(-1)].set(tok_of)
    return dest.reshape(-1).astype(jnp.int32), gates, row_tok, block_e, n_real, n_blocks


def moe_ffn(h, logits, layer, w1, b1, w2, b2):
    dest, gates, row_tok, block_e, n_real, n_blocks = moe_route(logits)
    ys = moe_grouped_mm(row_tok, block_e, n_real, n_blocks, h, layer, w1, b1, w2, b2)
    return dest, gates, ys


def kernel(x, c, ctx, c_ctx, w_ada, b_ada, g_mix, g_ffn, w_in, q_norm, k_norm, hy_conv_w, hy_conv_b, hy_w1, hy_b1, hy_w2, hy_b2, hy_w3, hy_freq, hy_bias, g_out, w_out, w_router, b_router, moe_w1, moe_b1, moe_w2, moe_b2, g_final):
    B, T, D = x.shape
    C = ctx.shape[1]
    depth = w_ada.shape[0]
    wq = N_HEADS * HEAD_DIM
    wkv = N_KV_HEADS * HEAD_DIM
    col_k, col_v, col_u = wq, wq + wkv, wq + 2 * wkv
    cos_t, sin_t = rope_tables(T)
    ones_c = jnp.ones((C, HEAD_DIM), F32)

    cvecs = jnp.concatenate([c, c_ctx[None, :], jnp.zeros((8 - B - 1, D), F32)], axis=0)
    ada = adaln_all(cvecs, w_ada, b_ada)

    moe_w2_b = moe_w2.astype(BF16)

    xc = ctx
    for i in range(depth):
        last = i == depth - 1
        mod = [ada[i, :B, j * D:(j + 1) * D].reshape(B, 1, D) for j in range(6)]
        cmod = [ada[i, B:B + 1, j * D:(j + 1) * D].reshape(1, 1, D) for j in range(6)]
        hp = (hy_conv_w[i], hy_conv_b[i], hy_w1[i], hy_b1[i], hy_w2[i], hy_b2[i], hy_w3[i], hy_freq[i], hy_bias[i])

        h = norm_mod(x, g_mix[i], mod[0], mod[1])
        hc = norm_mod(xc, g_mix[i], cmod[0], cmod[1])
        qkvu = matmul(h, w_in, i)
        q = head_norm(qkvu, 0, N_HEADS, q_norm[i], cos_t, sin_t, True, ATTN_SCALE)
        k = head_norm(qkvu, col_k, N_KV_HEADS, k_norm[i], cos_t, sin_t, True, 1.0)
        v = qkvu[:, :, col_v:col_u]
        hc1 = hc.reshape(1, B * C, D)
        if last:
            kvc = matmul(hc1, w_in, i, col_k, col_u - col_k).reshape(B, C, col_u - col_k)
            kc = head_norm(kvc, 0, N_KV_HEADS, k_norm[i], ones_c, ones_c, False, 1.0)
            vc = kvc[:, :, wkv:]
        else:
            qkvuc = matmul(hc1, w_in, i).reshape(B, C, -1)
            qc = head_norm(qkvuc, 0, N_HEADS, q_norm[i], ones_c, ones_c, False, ATTN_SCALE)
            kc = head_norm(qkvuc, col_k, N_KV_HEADS, k_norm[i], ones_c, ones_c, False, 1.0)
            vc = qkvuc[:, :, col_v:col_u]
        y_attn = attention(q, jnp.concatenate([kc, k], axis=1), values_with_ones(jnp.concatenate([vc, v], axis=1)))
        y_hy = hyena_mixer(qkvu, col_u, *hp)
        x = matmul_residual(mixnorm_channel_major(y_attn, y_hy, g_out[i]), w_out, i, x, mod[2])
        if not last:
            yc_attn = attention(qc, kc, values_with_ones(vc))
            yc_hy = hyena_mixer(qkvuc, col_u, *hp)
            xc = matmul_residual(mixnorm(yc_attn, yc_hy, g_out[i]).reshape(1, B * C, D), w_out, i,
                                 xc.reshape(1, B * C, D), cmod[2]).reshape(B, C, D)

        if last:
            hf, logits = norm_mod_router([x], g_ffn[i], [mod[3]], [mod[4]], w_router[i], b_router[i])
        else:
            hf, logits = norm_mod_router([x, xc], g_ffn[i], [mod[3], cmod[3]], [mod[4], cmod[4]],
                                         w_router[i], b_router[i])
        dest, gates, ys = moe_ffn(hf, logits, i, moe_w1, moe_b1, moe_w2_b, moe_b2)
        x = moe_combine(dest, ys, gates, x, mod[5], 0, g_final if last else None)
        if not last:
            xc = moe_combine(dest, ys, gates, xc, cmod[5], B * T)
    return x
```

```python
import functools
import math

import numpy as np
import jax
import jax.numpy as jnp
from jax import lax
from jax.experimental import pallas as pl
from jax.experimental.pallas import tpu as pltpu

F32 = jnp.float32
BF16 = jnp.bfloat16

GRID_W = 64
HEAD_DIM = 128
N_HEADS = 16
N_KV_HEADS = 4
GROUP = N_HEADS // N_KV_HEADS
HYENA_ORDER = 2
FILTER_BANDS = 16
FILTER_EMB = 1 + 2 * FILTER_BANDS
FILTER_HIDDEN = 64
DECAY_TARGET = 1e-2
FAST_DECAY_PCT = 0.3
SLOW_DECAY_PCT = 1.5
ROPE_THETA = 10000.0
N_EXPERTS = 32
TOP_K = 4
SWIGLU_LIMIT = 7.0
SWIGLU_ALPHA = 1.702
EPS = 1e-6

LANES = 128
V7X_VMEM_BYTES = 64 * 1024 * 1024
VMEM_LIMIT = V7X_VMEM_BYTES - 8 * 1024 * 1024
FFT_TA = 32
MOE_ROWS = 256
MOE_KCHUNK = 1024
MOE_AHEAD = 2
COMBINE_AHEAD = 2


def _params(sem):
    return pltpu.CompilerParams(dimension_semantics=sem, vmem_limit_bytes=VMEM_LIMIT)


def _pack_bf16_pair(x):
    n = x.shape[1] // 2
    return _pack_bf16_words(x[:, :n], x[:, n:])


def _pack_bf16_words(lo, hi):
    def rne(x):
        b = lax.bitcast_convert_type(x, jnp.uint32)
        return b + jnp.uint32(0x7FFF) + ((b >> 16) & jnp.uint32(1))

    return (rne(lo) >> 16) | (rne(hi) & jnp.uint32(0xFFFF0000))


def _unpack_bf16_pair(w):
    lo = lax.bitcast_convert_type(w << 16, F32)
    hi = lax.bitcast_convert_type(w & jnp.uint32(0xFFFF0000), F32)
    return lo, hi


def _adaln_kernel(c_ref, w_ref, b_ref, o_ref):
    c = c_ref[...]
    s = (c * jax.nn.sigmoid(c)).astype(BF16)
    o_ref[0] = jnp.dot(s, w_ref[0].astype(BF16), preferred_element_type=F32) + b_ref[0]


def adaln_all(cvecs, w_ada, b_ada):
    L, D, N = w_ada.shape
    tn = 512
    return pl.pallas_call(
        _adaln_kernel,
        grid=(L, N // tn),
        in_specs=[
            pl.BlockSpec((8, D), lambda l, j: (0, 0)),
            pl.BlockSpec((1, D, tn), lambda l, j: (l, 0, j)),
            pl.BlockSpec((1, 1, tn), lambda l, j: (l, 0, j)),
        ],
        out_specs=pl.BlockSpec((1, 8, tn), lambda l, j: (l, 0, j)),
        out_shape=jax.ShapeDtypeStruct((L, 8, N), F32),
        compiler_params=_params(("arbitrary", "arbitrary")),
        name="adaln",
    )(cvecs, w_ada, b_ada.reshape(L, 1, N))


def _norm_mod_kernel(x_ref, g_ref, sh_ref, sc_ref, h_ref):
    x = x_ref[0]
    y = x * lax.rsqrt(jnp.mean(x * x, axis=-1, keepdims=True) + EPS) * g_ref[...]
    h_ref[0] = (y * (1.0 + sc_ref[0]) + sh_ref[0]).astype(h_ref.dtype)


def _norm_mod_router_kernel(*refs, n_first):
    xs = refs[:-7]
    g_ref, sh_ref, sc_ref, wr_ref, br_ref, h_ref, l_ref = refs[-7:]

    def run(x_ref):
        x = x_ref[...]
        y = x * lax.rsqrt(jnp.mean(x * x, axis=-1, keepdims=True) + EPS) * g_ref[...]
        h = y * (1.0 + sc_ref[0]) + sh_ref[0]
        h_ref[...] = _pack_bf16_pair(h)
        e = l_ref.shape[1]
        h_hi = h.astype(BF16)
        h_lo = (h - h_hi.astype(F32)).astype(BF16)
        a = jnp.dot(h_hi, wr_ref[...], preferred_element_type=F32)
        b = jnp.dot(h_lo, wr_ref[:, :e], preferred_element_type=F32)
        l_ref[...] = a[:, :e] + a[:, e:] + b + br_ref[...]

    if len(xs) == 1:
        run(xs[0])
    else:
        i = pl.program_id(0)
        pl.when(i < n_first)(lambda: run(xs[0]))
        pl.when(i >= n_first)(lambda: run(xs[1]))


def _mod_map(bm):
    return (lambda b, i: (b, 0, 0)) if bm > 1 else (lambda b, i: (0, 0, 0))


def norm_mod(x, g, shift, scale, out_dtype=BF16):
    B, T, D = x.shape
    tm = min(T, 256)
    return pl.pallas_call(
        _norm_mod_kernel,
        grid=(B, T // tm),
        in_specs=[
            pl.BlockSpec((1, tm, D), lambda b, i: (b, i, 0)),
            pl.BlockSpec((1, D), lambda b, i: (0, 0)),
            pl.BlockSpec((1, 1, D), _mod_map(shift.shape[0])),
            pl.BlockSpec((1, 1, D), _mod_map(scale.shape[0])),
        ],
        out_specs=pl.BlockSpec((1, tm, D), lambda b, i: (b, i, 0)),
        out_shape=jax.ShapeDtypeStruct((B, T, D), out_dtype),
        compiler_params=_params(("arbitrary", "arbitrary")),
        name="norm_mod",
    )(x, g.reshape(1, D), shift, scale)


def norm_mod_router(xs, g, shifts, scales, w_router, b_router):
    D = xs[0].shape[-1]
    E = w_router.shape[1]
    tm = 256
    sizes = [a.shape[0] * a.shape[1] for a in xs]
    n_first = sizes[0] // tm
    n_tiles = sum(sizes) // tm
    assert all(a.shape[1] % tm == 0 for a in xs) and len(xs) <= 2

    def mod_rows(ms):
        return jnp.concatenate([jnp.broadcast_to(m, (a.shape[0], 1, D)) for m, a in zip(ms, xs)], axis=0)

    per0 = xs[0].shape[1] // tm

    def tile_row(i):
        if len(xs) == 1:
            return i // per0
        per1 = xs[1].shape[1] // tm
        return jnp.where(i < n_first, i // per0, xs[0].shape[0] + (i - n_first) // per1)

    x_specs = [pl.BlockSpec((tm, D), lambda i: (jnp.minimum(i, n_first - 1), 0))]
    if len(xs) == 2:
        x_specs.append(pl.BlockSpec((tm, D), lambda i: (jnp.maximum(i - n_first, 0), 0)))
    w_hi = w_router.astype(BF16)
    w_hilo = jnp.concatenate([w_hi, (w_router - w_hi.astype(F32)).astype(BF16)], axis=1)
    return pl.pallas_call(
        functools.partial(_norm_mod_router_kernel, n_first=n_first),
        grid=(n_tiles,),
        in_specs=x_specs + [
            pl.BlockSpec((1, D), lambda i: (0, 0)),
            pl.BlockSpec((1, 1, D), lambda i: (tile_row(i), 0, 0)),
            pl.BlockSpec((1, 1, D), lambda i: (tile_row(i), 0, 0)),
            pl.BlockSpec((D, 2 * E), lambda i: (0, 0)),
            pl.BlockSpec((1, E), lambda i: (0, 0)),
        ],
        out_specs=[pl.BlockSpec((tm, D // 2), lambda i: (i, 0)), pl.BlockSpec((tm, E), lambda i: (i, 0))],
        out_shape=[jax.ShapeDtypeStruct((n_tiles * tm, D // 2), jnp.uint32),
                   jax.ShapeDtypeStruct((n_tiles * tm, E), F32)],
        compiler_params=_params(("arbitrary",)),
        name="norm_mod_router",
    )(*[a.reshape(-1, D) for a in xs], g.reshape(1, D), mod_rows(shifts), mod_rows(scales),
      w_hilo, b_router.reshape(1, E))


def _mixnorm_kernel(a_ref, hy_ref, g_ref, o_ref):
    wa = a_ref.shape[-1]
    a = a_ref[0].astype(F32)
    b = hy_ref[0]
    g = g_ref[...]
    o_ref[0, :, :wa] = (a * lax.rsqrt(jnp.mean(a * a, axis=-1, keepdims=True) + EPS) * g[:, :wa]).astype(o_ref.dtype)
    o_ref[0, :, wa:] = (b * lax.rsqrt(jnp.mean(b * b, axis=-1, keepdims=True) + EPS) * g[:, wa:]).astype(o_ref.dtype)


def mixnorm(y_attn, y_hy, g):
    B, T, wa = y_attn.shape
    wh = y_hy.shape[-1]
    tm = min(T, 256)
    return pl.pallas_call(
        _mixnorm_kernel,
        grid=(B, T // tm),
        in_specs=[
            pl.BlockSpec((1, tm, wa), lambda b, i: (b, i, 0)),
            pl.BlockSpec((1, tm, wh), lambda b, i: (b, i, 0)),
            pl.BlockSpec((1, wa + wh), lambda b, i: (0, 0)),
        ],
        out_specs=pl.BlockSpec((1, tm, wa + wh), lambda b, i: (b, i, 0)),
        out_shape=jax.ShapeDtypeStruct((B, T, wa + wh), BF16),
        compiler_params=_params(("arbitrary", "arbitrary")),
        name="mixnorm",
    )(y_attn, y_hy, g.reshape(1, wa + wh))


def _mixnorm_cm_kernel(a_ref, hy_ref, g_ref, o_ref):
    wa = a_ref.shape[-1]
    g = g_ref[...]
    a = a_ref[0].astype(F32)
    o_ref[0, :, :wa] = (a * lax.rsqrt(jnp.mean(a * a, axis=-1, keepdims=True) + EPS) * g[:, :wa]).astype(o_ref.dtype)
    for j in range(hy_ref.shape[2]):
        b = hy_ref[0, :, j, :].T
        y = b * lax.rsqrt(jnp.mean(b * b, axis=-1, keepdims=True) + EPS) * g[:, wa:]
        o_ref[0, j * LANES:(j + 1) * LANES, wa:] = y.astype(o_ref.dtype)


def mixnorm_channel_major(y_attn, y_hy, g):
    B, T, wa = y_attn.shape
    wh, nta = y_hy.shape[1], y_hy.shape[2]
    jb = 8
    tm = jb * LANES
    assert nta * LANES == T and nta % jb == 0
    return pl.pallas_call(
        _mixnorm_cm_kernel,
        grid=(B, T // tm),
        in_specs=[
            pl.BlockSpec((1, tm, wa), lambda b, i: (b, i, 0)),
            pl.BlockSpec((1, wh, jb, LANES), lambda b, i: (b, 0, i, 0)),
            pl.BlockSpec((1, wa + wh), lambda b, i: (0, 0)),
        ],
        out_specs=pl.BlockSpec((1, tm, wa + wh), lambda b, i: (b, i, 0)),
        out_shape=jax.ShapeDtypeStruct((B, T, wa + wh), BF16),
        compiler_params=_params(("arbitrary", "arbitrary")),
        name="mixnorm_cm",
    )(y_attn, y_hy, g.reshape(1, wa + wh))


def _cast_weight_once(w_ref, wb_ref):
    @pl.when((pl.program_id(1) == 0) & (pl.program_id(2) == 0))
    def _():
        wb_ref[...] = w_ref[...].astype(BF16)


def _mm_kernel(x_ref, w_ref, o_ref, wb_ref):
    _cast_weight_once(w_ref, wb_ref)
    o_ref[0] = jnp.dot(x_ref[0], wb_ref[...], preferred_element_type=F32).astype(o_ref.dtype)


def _mm_res_kernel(x_ref, w_ref, r_ref, g_ref, o_ref, wb_ref):
    _cast_weight_once(w_ref, wb_ref)
    acc = jnp.dot(x_ref[0], wb_ref[...], preferred_element_type=F32)
    o_ref[0] = r_ref[0] + g_ref[0] * acc


MM_TN = 512


def matmul(x, w, layer, col0=0, n_out=None, out_dtype=F32):
    B, T, K = x.shape
    N = n_out or w.shape[2]
    tm, tn = min(T, 1024), MM_TN
    c0 = col0 // tn
    assert c0 * tn == col0 and N % tn == 0
    return pl.pallas_call(
        _mm_kernel,
        grid=(N // tn, B, T // tm),
        in_specs=[
            pl.BlockSpec((1, tm, K), lambda j, b, i: (b, i, 0)),
            pl.BlockSpec((None, K, tn), lambda j, b, i: (layer, 0, c0 + j)),
        ],
        out_specs=pl.BlockSpec((1, tm, tn), lambda j, b, i: (b, i, j)),
        out_shape=jax.ShapeDtypeStruct((B, T, N), out_dtype),
        scratch_shapes=[pltpu.VMEM((K, tn), BF16)],
        compiler_params=_params(("arbitrary", "arbitrary", "arbitrary")),
        name="matmul",
    )(x, w)


def matmul_residual(x, w, layer, res, gate):
    B, T, K = x.shape
    N = w.shape[2]
    tm, tn = min(T, 1024), MM_TN
    gmap = (lambda j, b, i: (b, 0, j)) if gate.shape[0] > 1 else (lambda j, b, i: (0, 0, j))
    return pl.pallas_call(
        _mm_res_kernel,
        grid=(N // tn, B, T // tm),
        in_specs=[
            pl.BlockSpec((1, tm, K), lambda j, b, i: (b, i, 0)),
            pl.BlockSpec((None, K, tn), lambda j, b, i: (layer, 0, j)),
            pl.BlockSpec((1, tm, tn), lambda j, b, i: (b, i, j)),
            pl.BlockSpec((1, 1, tn), gmap),
        ],
        out_specs=pl.BlockSpec((1, tm, tn), lambda j, b, i: (b, i, j)),
        out_shape=jax.ShapeDtypeStruct((B, T, N), F32),
        scratch_shapes=[pltpu.VMEM((K, tn), BF16)],
        compiler_params=_params(("arbitrary", "arbitrary", "arbitrary")),
        name="matmul_residual",
    )(x, w, res, gate)


def _head_norm_kernel(x_ref, g_ref, cos_ref, sin_ref, o_ref, *, n_heads, rope, scale):
    g = g_ref[...]
    if rope:
        cs = cos_ref[...]
        sn = sin_ref[...]
        lane = lax.broadcasted_iota(jnp.int32, cs.shape, 1)
        first = (lane % 64) < 32
    for h in range(n_heads):
        xh = x_ref[0, :, h * HEAD_DIM:(h + 1) * HEAD_DIM]
        y = xh * lax.rsqrt(jnp.mean(xh * xh, axis=-1, keepdims=True) + EPS) * g
        if rope:
            swapped = jnp.where(first, pltpu.roll(y, 96, 1), pltpu.roll(y, 32, 1))
            y = y * cs + swapped * sn
        o_ref[0, :, h * HEAD_DIM:(h + 1) * HEAD_DIM] = (y * scale).astype(o_ref.dtype)


def head_norm(x, col0, n_heads, g, cos_t, sin_t, rope, scale):
    B, T, _ = x.shape
    w = n_heads * HEAD_DIM
    tm = min(T, 256)
    cb = col0 // w
    assert cb * w == col0
    kern = functools.partial(_head_norm_kernel, n_heads=n_heads, rope=rope, scale=scale)
    return pl.pallas_call(
        kern,
        grid=(B, T // tm),
        in_specs=[
            pl.BlockSpec((1, tm, w), lambda b, i: (b, i, cb)),
            pl.BlockSpec((1, HEAD_DIM), lambda b, i: (0, 0)),
            pl.BlockSpec((tm, HEAD_DIM), lambda b, i: (i, 0)),
            pl.BlockSpec((tm, HEAD_DIM), lambda b, i: (i, 0)),
        ],
        out_specs=pl.BlockSpec((1, tm, w), lambda b, i: (b, i, 0)),
        out_shape=jax.ShapeDtypeStruct((B, T, w), BF16),
        compiler_params=_params(("arbitrary", "arbitrary")),
        name="head_norm",
    )(x, g.reshape(1, HEAD_DIM), cos_t, sin_t)


def rope_tables(T):
    pos = np.arange(T)
    r, col = pos // GRID_W, pos % GRID_W
    n_freq = HEAD_DIM // 4
    inv = ROPE_THETA ** (-np.arange(n_freq, dtype=np.float64) / n_freq)
    ar, ac = r[:, None] * inv, col[:, None] * inv
    cos_t = np.concatenate([np.cos(ar), np.cos(ar), np.cos(ac), np.cos(ac)], axis=1)
    sin_t = np.concatenate([-np.sin(ar), np.sin(ar), -np.sin(ac), np.sin(ac)], axis=1)
    return jnp.asarray(cos_t, F32), jnp.asarray(sin_t, F32)


ATTN_TK = 512
ATTN_SCALE = HEAD_DIM ** -0.5 * math.log2(math.e)


def _attn_kernel(q_ref, k_ref, v_ref, o_ref):
    tq = q_ref.shape[1]
    S = k_ref.shape[1]
    q = q_ref[0]
    qs = jnp.concatenate([q[:, j * HEAD_DIM:(j + 1) * HEAD_DIM] for j in range(GROUP)], axis=0)
    rows = GROUP * tq
    m = jnp.full((rows, 1), -jnp.inf, F32)
    acc = jnp.zeros((rows, 2 * HEAD_DIM), F32)
    for k0 in range(0, S, ATTN_TK):
        ks = slice(k0, min(k0 + ATTN_TK, S))
        s = lax.dot_general(qs, k_ref[0, ks, :], (((1,), (1,)), ((), ())), preferred_element_type=F32)
        m_new = jnp.maximum(m, jnp.max(s, axis=-1, keepdims=True))
        p = jnp.exp2(s - m_new)
        acc = jnp.exp2(m - m_new) * acc + jnp.dot(p.astype(BF16), v_ref[0, ks, :], preferred_element_type=F32)
        m = m_new
    o = acc[:, :HEAD_DIM] / acc[:, HEAD_DIM:HEAD_DIM + 1]
    for j in range(GROUP):
        o_ref[0, :, j * HEAD_DIM:(j + 1) * HEAD_DIM] = o[j * tq:(j + 1) * tq, :].astype(o_ref.dtype)


def values_with_ones(v):
    B, S, _ = v.shape
    v4 = v.reshape(B, S, N_KV_HEADS, HEAD_DIM).astype(BF16)
    pad = jnp.zeros((B, S, N_KV_HEADS, HEAD_DIM), BF16).at[..., 0].set(1.0)
    return jnp.concatenate([v4, pad], axis=-1).reshape(B, S, N_KV_HEADS * 2 * HEAD_DIM)


def attention(q, k, v1):
    B, T, _ = q.shape
    S = k.shape[1]
    assert S % 128 == 0
    tq = min(T, 256)
    gw = GROUP * HEAD_DIM
    return pl.pallas_call(
        _attn_kernel,
        grid=(B, N_KV_HEADS, T // tq),
        in_specs=[
            pl.BlockSpec((1, tq, gw), lambda b, g, i: (b, i, g)),
            pl.BlockSpec((1, S, HEAD_DIM), lambda b, g, i: (b, 0, g)),
            pl.BlockSpec((1, S, 2 * HEAD_DIM), lambda b, g, i: (b, 0, g)),
        ],
        out_specs=pl.BlockSpec((1, tq, gw), lambda b, g, i: (b, i, g)),
        out_shape=jax.ShapeDtypeStruct((B, T, N_HEADS * HEAD_DIM), BF16),
        compiler_params=_params(("arbitrary", "arbitrary", "arbitrary")),
        name="attention",
    )(q, k, v1)


PITCH_PAD = 4


def _store_channel_major(o_ref, chunk, n_chan, nta, stage):
    pitch = n_chan + PITCH_PAD
    for ta in range(nta):
        stage[ta * pitch:ta * pitch + n_chan, :] = chunk(ta)
    for c in range(n_chan):
        o_ref[0, c * nta:(c + 1) * nta, :] = stage[pl.ds(c, nta, stride=pitch), :]


def _short_conv_kernel(u_ref, w_ref, b_ref, o_ref, *scratch, channel_major):
    u = u_ref[0]
    L = u.shape[0]
    row = lax.broadcasted_iota(jnp.int32, u.shape, 0)
    prev = jnp.where(row == 0, 0.0, pltpu.roll(u, 1, 0))
    nxt = jnp.where(row == L - 1, 0.0, pltpu.roll(u, L - 1, 0))
    w = w_ref[...]
    y = prev * w[0:1, :] + u * w[1:2, :] + nxt * w[2:3, :] + b_ref[...]
    if not channel_major:
        o_ref[0] = y
    else:
        _store_channel_major(o_ref, lambda ta: y[ta * LANES:(ta + 1) * LANES, :].T, y.shape[1], L // LANES, scratch[0])


def short_conv(x, col0, width, w, b, channel_major):
    B, L, _ = x.shape
    cb = 256 if L > 1024 else 512
    assert col0 % cb == 0 and width % cb == 0
    off = col0 // cb
    scratch = []
    if channel_major:
        nta = L // LANES
        out_spec = pl.BlockSpec((1, cb * nta, LANES), lambda bi, j: (bi, j, 0))
        out_shape = jax.ShapeDtypeStruct((B, width * nta, LANES), F32)
        scratch = [pltpu.VMEM((nta * (cb + PITCH_PAD), LANES), F32)]
    else:
        out_spec = pl.BlockSpec((1, L, cb), lambda bi, j: (bi, 0, j))
        out_shape = jax.ShapeDtypeStruct((B, L, width), F32)
    return pl.pallas_call(
        functools.partial(_short_conv_kernel, channel_major=channel_major),
        grid=(B, width // cb),
        in_specs=[
            pl.BlockSpec((1, L, cb), lambda bi, j: (bi, 0, off + j)),
            pl.BlockSpec((3, cb), lambda bi, j: (0, j)),
            pl.BlockSpec((1, cb), lambda bi, j: (0, j)),
        ],
        out_specs=out_spec,
        out_shape=out_shape,
        scratch_shapes=scratch,
        compiler_params=_params(("arbitrary", "arbitrary")),
        name="short_conv",
    )(x, w, b.reshape(1, width))


def _filter_hidden_kernel(f_ref, w1_ref, b1_ref, w2_ref, b2_ref, fr_ref, o_ref):
    hp = lax.Precision.HIGHEST
    fr = fr_ref[...]
    h1 = jnp.sin(fr * (jnp.dot(w1_ref[...], f_ref[...], preferred_element_type=F32, precision=hp) + b1_ref[...]))
    o_ref[...] = jnp.sin(fr * (jnp.dot(w2_ref[...], h1, preferred_element_type=F32, precision=hp) + b2_ref[...]))


def filter_hidden(feats_t, w1, b1, w2, b2, freq):
    H = FILTER_HIDDEN
    P = feats_t.shape[1]
    fe = feats_t.shape[0]
    w1t = jnp.zeros((H, fe), F32).at[:, :FILTER_EMB].set(w1.T)
    return pl.pallas_call(
        _filter_hidden_kernel,
        out_shape=jax.ShapeDtypeStruct((H, P), F32),
        compiler_params=pltpu.CompilerParams(vmem_limit_bytes=VMEM_LIMIT),
        name="filter_hidden",
    )(feats_t, w1t, b1.reshape(H, 1), w2.T, b2.reshape(H, 1), freq.reshape(H, 1))


def _filter_k_kernel(w3_ref, hid_ref, tpos_ref, mask_ref, e0_ref, dl_ref, bias_ref, o_ref, *scratch, channel_major):
    w = w3_ref[0]
    w_hi = w.astype(BF16)
    w_lo = (w - w_hi.astype(F32)).astype(BF16)
    hd = hid_ref[0]
    h_hi = hd.astype(BF16)
    h_lo = (hd - h_hi.astype(F32)).astype(BF16)
    h = jnp.dot(jnp.concatenate([w_hi, w_lo, w_hi], axis=1), jnp.concatenate([h_hi, h_hi, h_lo], axis=0),
                preferred_element_type=F32)
    decay = jnp.exp(-tpos_ref[0] * dl_ref[...])
    k = mask_ref[0] * decay * h + bias_ref[0] * e0_ref[...]
    if not channel_major:
        o_ref[0] = k
    else:
        rc, L = k.shape
        _store_channel_major(o_ref, lambda ta: k[:, ta * LANES:(ta + 1) * LANES], rc, L // LANES, scratch[0])


def filter_k(w3t, hid, tpos, mask, e0, deltas, bias_aug, channel_major):
    OD, C, H = w3t.shape
    L = hid.shape[-1]
    rc = 256
    scratch = []
    if channel_major:
        nta = L // LANES
        out_spec = pl.BlockSpec((1, rc * nta, LANES), lambda od, j: (od, j, 0))
        out_shape = jax.ShapeDtypeStruct((OD, C * nta, LANES), F32)
        scratch = [pltpu.VMEM((nta * (rc + PITCH_PAD), LANES), F32)]
    else:
        out_spec = pl.BlockSpec((1, rc, L), lambda od, j: (od, j, 0))
        out_shape = jax.ShapeDtypeStruct((OD, C, L), F32)
    return pl.pallas_call(
        functools.partial(_filter_k_kernel, channel_major=channel_major),
        grid=(OD, C // rc),
        in_specs=[
            pl.BlockSpec((1, rc, H), lambda od, j: (od, j, 0)),
            pl.BlockSpec((1, H, L), lambda od, j: (od % 2, 0, 0)),
            pl.BlockSpec((1, 1, L), lambda od, j: (od % 2, 0, 0)),
            pl.BlockSpec((1, 1, L), lambda od, j: (od % 2, 0, 0)),
            pl.BlockSpec((1, L), lambda od, j: (0, 0)),
            pl.BlockSpec((rc, 1), lambda od, j: (j, 0)),
            pl.BlockSpec((1, rc, 1), lambda od, j: (od, j, 0)),
        ],
        out_specs=out_spec,
        out_shape=out_shape,
        scratch_shapes=scratch,
        compiler_params=_params(("arbitrary", "arbitrary")),
        name="filter_k",
    )(w3t, hid, tpos, mask, e0, deltas, bias_aug)


def hyena_filter_taps(L, C, w1, b1, w2, b2, w3, freq, bias, channel_major):
    pos = np.arange(L, dtype=np.float64)
    posr = np.where(pos == 0, 0.0, L - pos)

    def feats(p):
        t = p / max(L - 1, 1)
        bands = np.linspace(1e-4, FILTER_BANDS - 1, FILTER_BANDS)
        ang = (2 * math.pi / L) * p[:, None] * bands
        return np.concatenate([t[:, None], np.cos(ang), np.sin(ang)], axis=-1), t

    f0, t0 = feats(pos)
    f1, t1 = feats(posr)
    fe = 40
    ft = np.zeros((fe, 2 * L))
    ft[:FILTER_EMB, :L] = f0.T
    ft[:FILTER_EMB, L:] = f1.T
    hid = filter_hidden(jnp.asarray(ft, F32), w1, b1, w2, b2, freq)
    hid = jnp.stack([hid[:, :L], hid[:, L:]], axis=0)
    tpos = jnp.asarray(np.stack([t0, t1])[:, None, :], F32)
    mask = jnp.asarray(np.stack([np.ones(L), (pos > 0).astype(np.float64)])[:, None, :], F32)
    e0 = jnp.asarray((pos == 0).astype(np.float64)[None, :], F32)
    max_decay = math.log(DECAY_TARGET) / FAST_DECAY_PCT
    min_decay = math.log(DECAY_TARGET) / SLOW_DECAY_PCT
    deltas = jnp.asarray(np.abs(np.linspace(min_decay, max_decay, C))[:, None], F32)
    w3t = w3.T.reshape(HYENA_ORDER * 2, C, FILTER_HIDDEN)
    bias_aug = jnp.stack([bias, jnp.zeros_like(bias)], axis=1).reshape(HYENA_ORDER * 2, C, 1)
    k = filter_k(w3t, hid, tpos, mask, e0, deltas, bias_aug, channel_major)
    return k.reshape((HYENA_ORDER, 2) + k.shape[1:])


def _fft_consts():
    def emb(fr, fi):
        return np.block([[fr, fi], [-fi, fr]])

    n2 = LANES
    a = 2 * np.pi * np.outer(np.arange(n2), np.arange(n2)) / n2
    g128f = emb(np.cos(a), -np.sin(a))
    g128i = emb(np.cos(a), np.sin(a)) / (2.0 * FFT_TA * LANES)
    ta = np.arange(FFT_TA)
    eye = np.eye(LANES // FFT_TA)
    be = 2 * np.pi * np.outer(ta, ta) / FFT_TA
    bo_f = be + 2 * np.pi * ta[:, None] / (2 * FFT_TA)
    bo_i = be + 2 * np.pi * ta[None, :] / (2 * FFT_TA)
    g32 = [emb(np.kron(eye, np.cos(be)), np.kron(eye, -np.sin(be))), emb(np.kron(eye, np.cos(be)), np.kron(eye, np.sin(be))),
           emb(np.kron(eye, np.cos(bo_f)), np.kron(eye, -np.sin(bo_f))), emb(np.kron(eye, np.cos(bo_i)), np.kron(eye, np.sin(bo_i)))]
    mats = jnp.asarray(np.stack([g128f, g128i] + g32), BF16)
    n = FFT_TA * LANES
    fa = np.tile(ta, LANES // FFT_TA)[:, None]
    tb = np.arange(LANES)[None, :]
    th_e = 2 * np.pi * fa * tb / n
    th_o = 2 * np.pi * (2 * fa + 1) * tb / (2 * n)
    tw = jnp.asarray(np.stack([np.cos(th_e), np.sin(th_e), np.cos(th_o), np.sin(th_o)]), F32)
    return mats, tw


def _rows_to_lanes(re, im):
    parts = []
    for g in range(re.shape[0] // LANES):
        sl = slice(g * LANES, (g + 1) * LANES)
        parts.append(re[sl, :].T if im is None else jnp.concatenate([re[sl, :].T, im[sl, :].T], axis=1))
    return jnp.concatenate(parts, axis=0).astype(BF16)


def _dft_fwd(lhs, g32, g128, twc, tws):
    gm = g32 if lhs.shape[1] == 2 * LANES else g32[:LANES, :]
    o1 = jnp.dot(lhs, gm, preferred_element_type=F32)
    parts = []
    for g in range(lhs.shape[0] // LANES):
        sl = slice(g * LANES, (g + 1) * LANES)
        r = o1[sl, :LANES].T
        i = o1[sl, LANES:].T
        parts.append(jnp.concatenate([r * twc + i * tws, i * twc - r * tws], axis=1))
    o2 = jnp.dot(jnp.concatenate(parts, axis=0).astype(BF16), g128, preferred_element_type=F32)
    return o2[:, :LANES], o2[:, LANES:]


def _inv_stage1(re, im, g128, twc, tws):
    o1 = jnp.dot(jnp.concatenate([re, im], axis=1).astype(BF16), g128, preferred_element_type=F32)
    parts = []
    for g in range(re.shape[0] // LANES):
        sl = slice(g * LANES, (g + 1) * LANES)
        r = o1[sl, :LANES]
        i = o1[sl, LANES:]
        parts.append(jnp.concatenate([(r * twc - i * tws).T, (i * twc + r * tws).T], axis=1))
    return jnp.concatenate(parts, axis=0).astype(BF16)


def _lanes_to_rows(o2):
    rs, is_ = [], []
    for g in range(o2.shape[0] // LANES):
        sl = slice(g * LANES, (g + 1) * LANES)
        rs.append(o2[sl, :LANES].T)
        is_.append(o2[sl, LANES:].T)
    return jnp.concatenate(rs, axis=0), jnp.concatenate(is_, axis=0)


def _cmul(ar, ai, br, bi):
    return ar * br - ai * bi, ar * bi + ai * br


def _hyena_long_kernel(vr_ref, vi_ref, x1r_ref, x1i_ref, x2r_ref, x2i_ref,
                       k0lo_ref, k0hi_ref, k1lo_ref, k1hi_ref, mats_ref, tw_ref, o_ref):
    g128f, g128i = mats_ref[0], mats_ref[1]
    g32e_f, g32e_i, g32o_f, g32o_i = mats_ref[2], mats_ref[3], mats_ref[4], mats_ref[5]
    tce, tse, tco, tso = tw_ref[0], tw_ref[1], tw_ref[2], tw_ref[3]

    def conv(zr, zi, klo, khi):
        ke = _dft_fwd(_rows_to_lanes(klo + khi, None), g32e_f, g128f, tce, tse)
        ko = _dft_fwd(_rows_to_lanes(klo - khi, None), g32o_f, g128f, tco, tso)
        z_t = _rows_to_lanes(zr, zi)
        ze = _dft_fwd(z_t, g32e_f, g128f, tce, tse)
        zo = _dft_fwd(z_t, g32o_f, g128f, tco, tso)
        ye = _inv_stage1(*_cmul(*ze, *ke), g128i, tce, tse)
        yo = _inv_stage1(*_cmul(*zo, *ko), g128i, tco, tso)
        return _lanes_to_rows(jnp.dot(ye, g32e_i, preferred_element_type=F32)
                              + jnp.dot(yo, g32o_i, preferred_element_type=F32))

    yr, yi = conv(vr_ref[0, 0], vi_ref[0, 0], k0lo_ref[0, 0], k0hi_ref[0, 0])
    z1r = x1r_ref[0, 0] * yr
    z1i = x1i_ref[0, 0] * yi
    yr, yi = conv(z1r, z1i, k1lo_ref[0, 0], k1hi_ref[0, 0])
    o_ref[0] = x2r_ref[0, 0] * yr
    o_ref[1] = x2i_ref[0, 0] * yi


def hyena_long(ut, kt):
    _, _, R, _ = ut.shape
    rb = 32 * FFT_TA
    mats, tw = _fft_consts()

    def uspec(b, p):
        return pl.BlockSpec((1, 1, rb, LANES), lambda i: (b, p, i, 0))

    def kspec(o, d):
        return pl.BlockSpec((1, 1, rb, LANES), lambda i: (o, d, i, 0))

    return pl.pallas_call(
        _hyena_long_kernel,
        grid=(R // rb,),
        in_specs=[uspec(0, 0), uspec(1, 0), uspec(0, 1), uspec(1, 1), uspec(0, 2), uspec(1, 2),
                  kspec(0, 0), kspec(0, 1), kspec(1, 0), kspec(1, 1),
                  pl.BlockSpec(mats.shape, lambda i: (0, 0, 0)),
                  pl.BlockSpec(tw.shape, lambda i: (0, 0, 0))],
        out_specs=pl.BlockSpec((2, rb, LANES), lambda i: (0, i, 0)),
        out_shape=jax.ShapeDtypeStruct((2, R, LANES), F32),
        compiler_params=_params(("arbitrary",)),
        name="hyena_long",
    )(ut, ut, ut, ut, ut, ut, kt, kt, kt, kt, mats, tw)


def _hyena_short_kernel(vr_ref, vi_ref, x1r_ref, x1i_ref, x2r_ref, x2i_ref,
                        k0lo_ref, k0hi_ref, k1lo_ref, k1hi_ref, gf_ref, gi_ref, or_ref, oi_ref):
    L = vr_ref.shape[-1]
    n = 2 * L

    def conv(zr, zi, klo, khi):
        kk = jnp.concatenate([klo, khi], axis=1).astype(BF16)
        ks = jnp.dot(kk, gf_ref[:n, :], preferred_element_type=F32)
        kr, ki = ks[:, :n], ks[:, n:]
        zz = jnp.concatenate([zr, zi], axis=1).astype(BF16)
        zs = jnp.dot(zz, gf_ref[n:, :], preferred_element_type=F32)
        sr, si = zs[:, :n], zs[:, n:]
        pr = sr * kr - si * ki
        pi = sr * ki + si * kr
        y = jnp.dot(jnp.concatenate([pr, pi], axis=1).astype(BF16), gi_ref[...], preferred_element_type=F32)
        return y[:, :L], y[:, L:]

    yr, yi = conv(vr_ref[0, 0], vi_ref[0, 0], k0lo_ref[0, 0], k0hi_ref[0, 0])
    z1r = x1r_ref[0, 0] * yr
    z1i = x1i_ref[0, 0] * yi
    yr, yi = conv(z1r, z1i, k1lo_ref[0, 0], k1hi_ref[0, 0])
    or_ref[0] = x2r_ref[0, 0] * yr
    oi_ref[0] = x2i_ref[0, 0] * yi


def hyena_short(ut, kt):
    _, _, C, L = ut.shape
    n = 2 * L
    t = np.arange(n)
    a = 2 * np.pi * np.outer(t, t) / n
    co, si = np.cos(a), np.sin(a)
    gf = np.concatenate([
        np.concatenate([co, -si], axis=1),
        np.concatenate([co[:L], -si[:L]], axis=1),
        np.concatenate([si[:L], co[:L]], axis=1)], axis=0)
    gi = np.concatenate([
        np.concatenate([co[:, :L], si[:, :L]], axis=1),
        np.concatenate([-si[:, :L], co[:, :L]], axis=1)], axis=0) / n
    rc = 256

    def uspec(b, p):
        return pl.BlockSpec((1, 1, rc, L), lambda i: (b, p, i, 0))

    outr, outi = pl.pallas_call(
        _hyena_short_kernel,
        grid=(C // rc,),
        in_specs=[uspec(0, 0), uspec(1, 0), uspec(0, 1), uspec(1, 1), uspec(0, 2), uspec(1, 2),
                  uspec(0, 0), uspec(0, 1), uspec(1, 0), uspec(1, 1),
                  pl.BlockSpec((2 * n, 2 * n), lambda i: (0, 0)),
                  pl.BlockSpec((2 * n, n), lambda i: (0, 0))],
        out_specs=[pl.BlockSpec((1, rc, L), lambda i: (0, i, 0)),
                   pl.BlockSpec((1, rc, L), lambda i: (0, i, 0))],
        out_shape=[jax.ShapeDtypeStruct((1, C, L), F32), jax.ShapeDtypeStruct((1, C, L), F32)],
        compiler_params=_params(("arbitrary",)),
        name="hyena_short",
    )(ut, ut, ut, ut, ut, ut, kt, kt, kt, kt, jnp.asarray(gf, BF16), jnp.asarray(gi, BF16))
    return jnp.concatenate([outr, outi], axis=0)


def hyena_mixer(x, col0, conv_w, conv_b, w1, b1, w2, b2, w3, freq, bias):
    B, L, _ = x.shape
    C = bias.shape[-1]
    assert B == 2
    if L == FFT_TA * LANES:
        ut = short_conv(x, col0, 3 * C, conv_w, conv_b, True).reshape(B, 3, C * FFT_TA, LANES)
        kt = hyena_filter_taps(L, C, w1, b1, w2, b2, w3, freq, bias, True)
        return hyena_long(ut, kt).reshape(B, C, FFT_TA, LANES)
    uc = short_conv(x, col0, 3 * C, conv_w, conv_b, False)
    ut = uc.reshape(B, L, 3, C).transpose(0, 2, 3, 1)
    kt = hyena_filter_taps(L, C, w1, b1, w2, b2, w3, freq, bias, False)
    return hyena_short(ut, kt).transpose(0, 2, 1)


def _topk_kernel(l_ref, idx_ref, gate_ref):
    l = l_ref[...]
    E = l.shape[1]
    lane = lax.broadcasted_iota(jnp.int32, l.shape, 1).astype(F32)
    vals, idxs = [], []
    for _ in range(TOP_K):
        m = jnp.max(l, axis=-1, keepdims=True)
        am = jnp.min(jnp.where(l == m, lane, float(E)), axis=-1, keepdims=True)
        vals.append(m)
        idxs.append(am)
        l = jnp.where(lane == am, -jnp.inf, l)
    v = jnp.concatenate(vals, axis=1)
    e = jnp.exp(v - vals[0])
    gate_ref[...] = e / jnp.sum(e, axis=-1, keepdims=True)
    idx_ref[...] = jnp.concatenate(idxs, axis=1).astype(jnp.int32)


def topk_gates(logits):
    N, E = logits.shape
    tm = 512
    return pl.pallas_call(
        _topk_kernel,
        grid=(N // tm,),
        in_specs=[pl.BlockSpec((tm, E), lambda i: (i, 0))],
        out_specs=[pl.BlockSpec((tm, TOP_K), lambda i: (i, 0)), pl.BlockSpec((tm, TOP_K), lambda i: (i, 0))],
        out_shape=[jax.ShapeDtypeStruct((N, TOP_K), jnp.int32), jax.ShapeDtypeStruct((N, TOP_K), F32)],
        compiler_params=_params(("arbitrary",)),
        name="topk_gates",
    )(logits)


def _rank_kernel(idx_ref, rank_ref, cnt_ref, carry_ref):
    @pl.when(pl.program_id(0) == 0)
    def _():
        carry_ref[...] = jnp.zeros_like(carry_ref)

    idx = idx_ref[...]
    tm = idx.shape[0]
    E = carry_ref.shape[1]
    e_iota = lax.broadcasted_iota(jnp.int32, (tm, E), 1)
    sel = [idx[:, k:k + 1] == e_iota for k in range(TOP_K)]
    m = jnp.zeros((tm, E), F32)
    for s in sel:
        m = m + jnp.where(s, 1.0, 0.0)
    r_i = lax.broadcasted_iota(jnp.int32, (tm, tm), 0)
    c_i = lax.broadcasted_iota(jnp.int32, (tm, tm), 1)
    tri = jnp.where(r_i > c_i, 1.0, 0.0).astype(BF16)
    before = jnp.dot(tri, m.astype(BF16), preferred_element_type=F32) + carry_ref[...]
    ranks = [jnp.sum(jnp.where(s, before, 0.0), axis=-1, keepdims=True) for s in sel]
    rank_ref[...] = jnp.concatenate(ranks, axis=1).astype(jnp.int32)
    carry_ref[...] = carry_ref[...] + jnp.sum(m, axis=0, keepdims=True)
    cnt_ref[...] = carry_ref[...]


def expert_ranks(top_idx):
    N, _ = top_idx.shape
    tm = 512
    return pl.pallas_call(
        _rank_kernel,
        grid=(N // tm,),
        in_specs=[pl.BlockSpec((tm, TOP_K), lambda i: (i, 0))],
        out_specs=[pl.BlockSpec((tm, TOP_K), lambda i: (i, 0)), pl.BlockSpec((1, N_EXPERTS), lambda i: (0, 0))],
        out_shape=[jax.ShapeDtypeStruct((N, TOP_K), jnp.int32), jax.ShapeDtypeStruct((1, N_EXPERTS), F32)],
        scratch_shapes=[pltpu.VMEM((1, N_EXPERTS), F32)],
        compiler_params=_params(("arbitrary",)),
        name="expert_ranks",
    )(top_idx)


def _gather_params(sem):
    return pltpu.CompilerParams(dimension_semantics=sem, vmem_limit_bytes=VMEM_LIMIT, disable_bounds_checks=True)


def _moe_mm_kernel(rowtok_ref, be_ref, nreal_ref, h_ref, w1_ref, b1_ref, w2_ref, b2_ref, o_ref, xbuf, sems):
    i = pl.program_id(0)
    n_real = nreal_ref[0]
    tb = xbuf.shape[1]
    de = w2_ref.shape[0]

    def issue(blk, slot):
        for r in range(tb):
            tok = rowtok_ref[blk * tb + r]
            pltpu.make_async_copy(h_ref.at[pl.ds(tok, 1), :], xbuf.at[slot, pl.ds(r, 1), :], sems.at[slot]).start()

    def wait(slot):
        pltpu.make_async_copy(h_ref.at[pl.ds(0, tb), :], xbuf.at[slot], sems.at[slot]).wait()

    n_slots = MOE_AHEAD + 1
    last = n_real - 1

    @pl.when(i == 0)
    def _():
        for a in range(MOE_AHEAD):
            issue(jnp.minimum(i + a, last), a)

    @pl.when(i < n_real)
    def _():
        slot = i % n_slots
        wait(slot)
        issue(jnp.minimum(i + MOE_AHEAD, last), (i + MOE_AHEAD) % n_slots)
        x = jnp.concatenate(_unpack_bf16_pair(xbuf[slot]), axis=1).astype(BF16)
        gu = b1_ref[...]
        for k0 in range(0, x.shape[1], MOE_KCHUNK):
            ks = slice(k0, k0 + MOE_KCHUNK)
            gu = gu + jnp.dot(x[:, ks], w1_ref[ks, :].astype(BF16), preferred_element_type=F32)
        g = jnp.minimum(gu[:, :de], SWIGLU_LIMIT)
        up = jnp.clip(gu[:, de:], -SWIGLU_LIMIT, SWIGLU_LIMIT)
        act = (up + 1.0) * g * jax.nn.sigmoid(SWIGLU_ALPHA * g)
        o_ref[...] = _pack_bf16_pair(
            jnp.dot(act.astype(BF16), w2_ref[...].astype(BF16), preferred_element_type=F32) + b2_ref[...])

    @pl.when(i == last)
    def _():
        for a in range(1, n_slots):
            wait((i + a) % n_slots)

    @pl.when(i >= n_real)
    def _():
        o_ref[...] = jnp.zeros_like(o_ref)


def moe_grouped_mm(row_tok, block_e, n_real, n_blocks, h, layer, w1, b1, w2, b2):
    D = 2 * h.shape[1]
    _, E, _, de2 = w1.shape
    de = de2 // 2
    tb = MOE_ROWS
    return pl.pallas_call(
        _moe_mm_kernel,
        grid_spec=pltpu.PrefetchScalarGridSpec(
            num_scalar_prefetch=3,
            grid=(n_blocks,),
            in_specs=[
                pl.BlockSpec(memory_space=pl.ANY),
                pl.BlockSpec((None, None, D, de2), lambda i, rt, be, nr: (layer, be[i], 0, 0)),
                pl.BlockSpec((None, None, 1, de2), lambda i, rt, be, nr: (layer, be[i], 0, 0)),
                pl.BlockSpec((None, None, de, D), lambda i, rt, be, nr: (layer, be[i], 0, 0),
                             pipeline_mode=pl.Buffered(1)),
                pl.BlockSpec((None, None, 1, D), lambda i, rt, be, nr: (layer, be[i], 0, 0)),
            ],
            out_specs=pl.BlockSpec((tb, D // 2), lambda i, rt, be, nr: (i, 0)),
            scratch_shapes=[pltpu.VMEM((MOE_AHEAD + 1, tb, D // 2), jnp.uint32),
                            pltpu.SemaphoreType.DMA((MOE_AHEAD + 1,))],
        ),
        out_shape=jax.ShapeDtypeStruct((n_blocks * tb, D // 2), jnp.uint32),
        compiler_params=_gather_params(("arbitrary",)),
        name="moe_grouped_mm",
    )(row_tok, block_e, n_real, h, w1, b1.reshape(b1.shape[0], E, 1, de2), w2, b2.reshape(b2.shape[0], E, 1, D))


def _combine_kernel(dest_ref, ys_ref, gates_ref, x_ref, gt_ref, *refs, tok0, final_norm):
    fg_ref = refs[0] if final_norm else None
    o_ref, buf, sems = refs[-3:]
    nt = pl.num_programs(1)
    step = pl.program_id(0) * nt + pl.program_id(1)
    n_steps = pl.num_programs(0) * nt
    tm = buf.shape[2]

    def issue(st, slot):
        base = (tok0 + st * tm) * TOP_K
        for r in range(tm):
            for k in range(TOP_K):
                d = dest_ref[base + r * TOP_K + k]
                pltpu.make_async_copy(ys_ref.at[pl.ds(d, 1), :], buf.at[slot, k, pl.ds(r, 1), :], sems.at[slot]).start()

    def wait(slot):
        for k in range(TOP_K):
            pltpu.make_async_copy(ys_ref.at[pl.ds(0, tm), :], buf.at[slot, k], sems.at[slot]).wait()

    n_slots = COMBINE_AHEAD + 1
    last = n_steps - 1

    @pl.when(step == 0)
    def _():
        for a in range(COMBINE_AHEAD):
            issue(jnp.minimum(step + a, last), a)

    slot = step % n_slots
    wait(slot)
    issue(jnp.minimum(step + COMBINE_AHEAD, last), (step + COMBINE_AHEAD) % n_slots)
    gates = gates_ref[...]
    acc_lo, acc_hi = 0.0, 0.0
    for k in range(TOP_K):
        lo, hi = _unpack_bf16_pair(buf[slot, k])
        acc_lo = acc_lo + gates[:, k:k + 1] * lo
        acc_hi = acc_hi + gates[:, k:k + 1] * hi
    y = x_ref[0] + gt_ref[0] * jnp.concatenate([acc_lo, acc_hi], axis=1)
    if final_norm:
        y = y * lax.rsqrt(jnp.mean(y * y, axis=-1, keepdims=True) + EPS) * fg_ref[...]
    o_ref[0] = y

    @pl.when(step == last)
    def _():
        for a in range(1, n_slots):
            wait((step + a) % n_slots)


def moe_combine(dest, ys, gates, x, gate_vec, tok0, final_g=None):
    B, T, D = x.shape
    tm = 128
    nt = T // tm
    g0 = tok0 // tm
    gmap = (lambda b, i, d: (b, 0, 0)) if gate_vec.shape[0] > 1 else (lambda b, i, d: (0, 0, 0))
    in_specs = [
        pl.BlockSpec(memory_space=pl.ANY),
        pl.BlockSpec((tm, TOP_K), lambda b, i, d: (g0 + b * nt + i, 0)),
        pl.BlockSpec((1, tm, D), lambda b, i, d: (b, i, 0)),
        pl.BlockSpec((1, 1, D), gmap),
    ]
    args = [dest, ys, gates, x, gate_vec]
    if final_g is not None:
        in_specs.append(pl.BlockSpec((1, D), lambda b, i, d: (0, 0)))
        args.append(final_g.reshape(1, D))
    return pl.pallas_call(
        functools.partial(_combine_kernel, tok0=tok0, final_norm=final_g is not None),
        grid_spec=pltpu.PrefetchScalarGridSpec(
            num_scalar_prefetch=1,
            grid=(B, nt),
            in_specs=in_specs,
            out_specs=pl.BlockSpec((1, tm, D), lambda b, i, d: (b, i, 0)),
            scratch_shapes=[pltpu.VMEM((COMBINE_AHEAD + 1, TOP_K, tm, D // 2), jnp.uint32),
                            pltpu.SemaphoreType.DMA((COMBINE_AHEAD + 1,))],
        ),
        out_shape=jax.ShapeDtypeStruct((B, T, D), F32),
        compiler_params=_gather_params(("arbitrary", "arbitrary")),
        name="moe_combine",
    )(*args)


def moe_route(logits):
    N = logits.shape[0]
    tb = MOE_ROWS
    top_idx, gates = topk_gates(logits)
    rank, counts = expert_ranks(top_idx)
    counts = counts.reshape(N_EXPERTS).astype(jnp.int32)
    padded = (counts + tb - 1) // tb * tb
    pad_end = jnp.cumsum(padded)
    pad_start = pad_end - padded
    dest = pad_start[top_idx] + rank
    n_blocks = -(-(N * TOP_K) // tb) + N_EXPERTS
    n_real = (pad_end[-1] // tb).astype(jnp.int32).reshape(1)
    starts = jnp.arange(n_blocks, dtype=jnp.int32) * tb
    block_e = jnp.minimum(jnp.sum(pad_end[None, :] <= starts[:, None], axis=1), N_EXPERTS - 1).astype(jnp.int32)
    tok_of = jnp.arange(N * TOP_K, dtype=jnp.int32) // TOP_K
    row_tok = jnp.zeros((n_blocks * tb,), jnp.int32).at[dest.reshape(-1)].set(tok_of)
    return dest.reshape(-1).astype(jnp.int32), gates, row_tok, block_e, n_real, n_blocks


def moe_ffn(h, logits, layer, w1, b1, w2, b2):
    dest, gates, row_tok, block_e, n_real, n_blocks = moe_route(logits)
    ys = moe_grouped_mm(row_tok, block_e, n_real, n_blocks, h, layer, w1, b1, w2, b2)
    return dest, gates, ys


def kernel(x, c, ctx, c_ctx, w_ada, b_ada, g_mix, g_ffn, w_in, q_norm, k_norm, hy_conv_w, hy_conv_b, hy_w1, hy_b1, hy_w2, hy_b2, hy_w3, hy_freq, hy_bias, g_out, w_out, w_router, b_router, moe_w1, moe_b1, moe_w2, moe_b2, g_final):
    B, T, D = x.shape
    C = ctx.shape[1]
    depth = w_ada.shape[0]
    wq = N_HEADS * HEAD_DIM
    wkv = N_KV_HEADS * HEAD_DIM
    col_k, col_v, col_u = wq, wq + wkv, wq + 2 * wkv
    cos_t, sin_t = rope_tables(T)
    ones_c = jnp.ones((C, HEAD_DIM), F32)

    cvecs = jnp.concatenate([c, c_ctx[None, :], jnp.zeros((8 - B - 1, D), F32)], axis=0)
    ada = adaln_all(cvecs, w_ada, b_ada)


    xc = ctx
    for i in range(depth):
        last = i == depth - 1
        mod = [ada[i, :B, j * D:(j + 1) * D].reshape(B, 1, D) for j in range(6)]
        cmod = [ada[i, B:B + 1, j * D:(j + 1) * D].reshape(1, 1, D) for j in range(6)]
        hp = (hy_conv_w[i], hy_conv_b[i], hy_w1[i], hy_b1[i], hy_w2[i], hy_b2[i], hy_w3[i], hy_freq[i], hy_bias[i])

        h = norm_mod(x, g_mix[i], mod[0], mod[1])
        hc = norm_mod(xc, g_mix[i], cmod[0], cmod[1])
        qkvu = matmul(h, w_in, i)
        q = head_norm(qkvu, 0, N_HEADS, q_norm[i], cos_t, sin_t, True, ATTN_SCALE)
        k = head_norm(qkvu, col_k, N_KV_HEADS, k_norm[i], cos_t, sin_t, True, 1.0)
        v = qkvu[:, :, col_v:col_u]
        hc1 = hc.reshape(1, B * C, D)
        if last:
            kvc = matmul(hc1, w_in, i, col_k, col_u - col_k).reshape(B, C, col_u - col_k)
            kc = head_norm(kvc, 0, N_KV_HEADS, k_norm[i], ones_c, ones_c, False, 1.0)
            vc = kvc[:, :, wkv:]
        else:
            qkvuc = matmul(hc1, w_in, i).reshape(B, C, -1)
            qc = head_norm(qkvuc, 0, N_HEADS, q_norm[i], ones_c, ones_c, False, ATTN_SCALE)
            kc = head_norm(qkvuc, col_k, N_KV_HEADS, k_norm[i], ones_c, ones_c, False, 1.0)
            vc = qkvuc[:, :, col_v:col_u]
        y_attn = attention(q, jnp.concatenate([kc, k], axis=1), values_with_ones(jnp.concatenate([vc, v], axis=1)))
        y_hy = hyena_mixer(qkvu, col_u, *hp)
        x = matmul_residual(mixnorm_channel_major(y_attn, y_hy, g_out[i]), w_out, i, x, mod[2])
        if not last:
            yc_attn = attention(qc, kc, values_with_ones(vc))
            yc_hy = hyena_mixer(qkvuc, col_u, *hp)
            xc = matmul_residual(mixnorm(yc_attn, yc_hy, g_out[i]).reshape(1, B * C, D), w_out, i,
                                 xc.reshape(1, B * C, D), cmod[2]).reshape(B, C, D)

        if last:
            hf, logits = norm_mod_router([x], g_ffn[i], [mod[3]], [mod[4]], w_router[i], b_router[i])
        else:
            hf, logits = norm_mod_router([x, xc], g_ffn[i], [mod[3], cmod[3]], [mod[4], cmod[4]],
                                         w_router[i], b_router[i])
        dest, gates, ys = moe_ffn(hf, logits, i, moe_w1, moe_b1, moe_w2, moe_b2)
        x = moe_combine(dest, ys, gates, x, mod[5], 0, g_final if last else None)
        if not last:
            xc = moe_combine(dest, ys, gates, xc, cmod[5], B * T)
    return x
```
